```python
import jax, jax.numpy as jnp
from jax import lax
import numpy as np

D_MODEL = 1024
BATCH = 4
SEQ = 8192
DEPTH = 1

EPS = 1e-6
D_FF = 2816
Q_BLOCK = 128
NEG = -1e30
FORCE_SCORE = 1e4
NSA_HEADS = 8
NSA_KV_GROUPS = 2
NSA_HPG = NSA_HEADS // NSA_KV_GROUPS
NSA_DK = 64
NSA_DV = 64
CMP_LEN = 32
CMP_STRIDE = 16
CMP_HID = 256
SEL_LEN = 64
SEL_TOPK = 16
WINDOW = 512
MLA_HEADS = 8
MLA_NOPE = 64
MLA_ROPE = 32
MLA_V = 64
MLA_Q_RANK = 256
MLA_KV_RANK = 128
ROPE_THETA = 10000.0
IN_SIZES = (NSA_HEADS * NSA_DK, 6 * NSA_KV_GROUPS * NSA_DK, 3 * NSA_HEADS, MLA_Q_RANK, MLA_KV_RANK, MLA_ROPE, 2 * D_MODEL)
D_IN = sum(IN_SIZES)

kernel_name = 'hybrid_nsa_mla_macaron_block'


def rms_norm(x, g):
    xf = x.astype(jnp.float32)
    y = xf * lax.rsqrt(jnp.mean(xf * xf, axis=-1, keepdims=True) + EPS)
    return (y * g.astype(jnp.float32)).astype(x.dtype)


def swiglu(x, w_gate, w_up, w_down):
    return (jax.nn.silu(x @ w_gate) * (x @ w_up)) @ w_down


def masked_softmax(s, mask):
    s = jnp.where(mask, s, NEG)
    m = jnp.max(s, axis=-1, keepdims=True)
    e = jnp.exp(s - m) * mask
    return e / jnp.maximum(jnp.sum(e, axis=-1, keepdims=True), 1e-30)


def alibi_slopes(n):
    return (2.0 ** (-8.0 * np.arange(1, n + 1, dtype=np.float32) / n)).astype(np.float32)


def apply_rope(x, pos):
    half = x.shape[-1] // 2
    freqs = jnp.asarray(ROPE_THETA ** (-np.arange(half, dtype=np.float32) / half), jnp.float32)
    ang = pos.astype(jnp.float32)[:, None] * freqs[None, :]
    cos = jnp.cos(ang)[None, :, None, :]
    sin = jnp.sin(ang)[None, :, None, :]
    xf = x.astype(jnp.float32)
    x1, x2 = xf[..., :half], xf[..., half:]
    return jnp.concatenate([x1 * cos - x2 * sin, x1 * sin + x2 * cos], axis=-1).astype(x.dtype)


def nsa_compress(kv, pos_emb, w1, w2):
    b, s, g, d = kv.shape
    n_c = (s - CMP_LEN) // CMP_STRIDE + 1
    idx = np.arange(n_c)[:, None] * CMP_STRIDE + np.arange(CMP_LEN)[None, :]
    blocks = kv[:, idx] + pos_emb[None, None, :, None, :]
    blocks = blocks.transpose(0, 1, 3, 2, 4).reshape(b, n_c, g, CMP_LEN * d)
    return jax.nn.gelu(blocks @ w1) @ w2


def nsa_attention(q, k_cmp, v_cmp, k_slc, v_slc, k_win, v_win, branch_gates,
                  cmp_pos_k, cmp_w1_k, cmp_w2_k, cmp_pos_v, cmp_w1_v, cmp_w2_v):
    b, s, h, d = q.shape
    g, hg = NSA_KV_GROUPS, NSA_HPG
    n_c = (s - CMP_LEN) // CMP_STRIDE + 1
    n_sel = s // SEL_LEN
    top_k = min(SEL_TOPK, n_sel)
    n_tok = top_k * SEL_LEN
    kc = nsa_compress(k_cmp, cmp_pos_k, cmp_w1_k, cmp_w2_k)
    vc = nsa_compress(v_cmp, cmp_pos_v, cmp_w1_v, cmp_w2_v)
    cmp_end = jnp.asarray(np.arange(n_c) * CMP_STRIDE + CMP_LEN - 1, jnp.int32)
    c0 = np.arange(n_c) * CMP_STRIDE
    s0 = np.arange(n_sel) * SEL_LEN
    overlap = np.clip(np.minimum(c0[:, None] + CMP_LEN, s0[None, :] + SEL_LEN)
                      - np.maximum(c0[:, None], s0[None, :]), 0, None)
    overlap = jnp.asarray(overlap / CMP_LEN, jnp.float32)
    ks_blocks = k_slc.reshape(b, n_sel, SEL_LEN, g, d).transpose(0, 3, 1, 2, 4)
    vs_blocks = v_slc.reshape(b, n_sel, SEL_LEN, g, NSA_DV).transpose(0, 3, 1, 2, 4)
    kw = jnp.pad(k_win, ((0, 0), (WINDOW, 0), (0, 0), (0, 0)))
    vw = jnp.pad(v_win, ((0, 0), (WINDOW, 0), (0, 0), (0, 0)))
    qg = (q * NSA_DK ** -0.5).reshape(b, s, g, hg, d)
    gates = jax.nn.sigmoid(branch_gates.astype(jnp.float32)).reshape(b, s, g, hg, 3)
    slopes = jnp.asarray(alibi_slopes(h)).reshape(g, hg)[None, :, :, None, None]
    gather_blocks = jax.vmap(jax.vmap(lambda blk, ix: blk[ix]))
    sel_offsets = jnp.arange(SEL_LEN)
    blk_ids = jnp.arange(n_sel)
    win_offsets = jnp.arange(Q_BLOCK + WINDOW) - WINDOW

    def one_block(qi):
        q0 = qi * Q_BLOCK
        t = q0 + jnp.arange(Q_BLOCK)
        qb = lax.dynamic_slice_in_dim(qg, q0, Q_BLOCK, axis=1)
        gb = lax.dynamic_slice_in_dim(gates, q0, Q_BLOCK, axis=1)
        dist_c = t[:, None] - cmp_end[None, :]
        s_c = jnp.einsum('btghd,bigd->bghti', qb, kc).astype(jnp.float32) - slopes * dist_c.astype(jnp.float32)
        p_c = masked_softmax(s_c, dist_c >= 0)
        o_c = jnp.einsum('bghti,bigd->btghd', p_c.astype(vc.dtype), vc)
        imp = jnp.einsum('bghti,ij->bgtj', p_c, overlap)
        cur = (t // SEL_LEN)[:, None]
        forced = (blk_ids[None, :] == 0) | (blk_ids[None, :] == cur) | (blk_ids[None, :] == cur - 1)
        imp = jnp.where(blk_ids[None, :] <= cur, jnp.where(forced, FORCE_SCORE, imp), NEG)
        _, sel = lax.top_k(imp, top_k)
        ksel = gather_blocks(ks_blocks, sel).reshape(b, g, Q_BLOCK, n_tok, d)
        vsel = gather_blocks(vs_blocks, sel).reshape(b, g, Q_BLOCK, n_tok, NSA_DV)
        pos_s = (sel[..., None] * SEL_LEN + sel_offsets).reshape(b, g, Q_BLOCK, n_tok)
        dist_s = (t[None, None, :, None] - pos_s)[:, :, None]
        s_s = jnp.einsum('btghd,bgtnd->bghtn', qb, ksel).astype(jnp.float32) - slopes * dist_s.astype(jnp.float32)
        p_s = masked_softmax(s_s, dist_s >= 0)
        o_s = jnp.einsum('bghtn,bgtnd->btghd', p_s.astype(vsel.dtype), vsel)
        kwb = lax.dynamic_slice_in_dim(kw, q0, Q_BLOCK + WINDOW, axis=1)
        vwb = lax.dynamic_slice_in_dim(vw, q0, Q_BLOCK + WINDOW, axis=1)
        pos_w = q0 + win_offsets
        dist_w = t[:, None] - pos_w[None, :]
        mask_w = (dist_w >= 0) & (dist_w < WINDOW) & (pos_w[None, :] >= 0)
        s_w = jnp.einsum('btghd,bsgd->bghts', qb, kwb).astype(jnp.float32) - slopes * dist_w.astype(jnp.float32)
        p_w = masked_softmax(s_w, mask_w)
        o_w = jnp.einsum('bghts,bsgd->btghd', p_w.astype(vwb.dtype), vwb)
        o = gb[..., 0:1] * o_c + gb[..., 1:2] * o_s + gb[..., 2:3] * o_w
        return o.astype(q.dtype).reshape(b, Q_BLOCK, h * NSA_DV)

    out = lax.map(one_block, jnp.arange(s // Q_BLOCK))
    return out.transpose(1, 0, 2, 3).reshape(b, s, h * NSA_DV)


def mla_attention(c_q, c_kv, k_pe, q_norm_g, w_uq, kv_norm_g, w_ukv):
    b, s, _ = c_q.shape
    h = MLA_HEADS
    pos = jnp.arange(s)
    q = (rms_norm(c_q, q_norm_g) @ w_uq).reshape(b, s, h, MLA_NOPE + MLA_ROPE)
    kv = (rms_norm(c_kv, kv_norm_g) @ w_ukv).reshape(b, s, h, MLA_NOPE + MLA_V)
    q_pe = apply_rope(q[..., MLA_NOPE:], pos)
    k_rot = apply_rope(k_pe[:, :, None, :], pos)
    qf = jnp.concatenate([q[..., :MLA_NOPE], q_pe], axis=-1) * (MLA_NOPE + MLA_ROPE) ** -0.5
    k = jnp.concatenate([kv[..., :MLA_NOPE], jnp.broadcast_to(k_rot, (b, s, h, MLA_ROPE))], axis=-1)
    v = kv[..., MLA_NOPE:]

    def one_block(qi):
        q0 = qi * Q_BLOCK
        t = q0 + jnp.arange(Q_BLOCK)
        qb = lax.dynamic_slice_in_dim(qf, q0, Q_BLOCK, axis=1)
        sc = jnp.einsum('bthd,bshd->bhts', qb, k).astype(jnp.float32)
        sc = jnp.where(pos[None, :] <= t[:, None], sc, NEG)
        p = jax.nn.softmax(sc, axis=-1)
        o = jnp.einsum('bhts,bshd->bthd', p.astype(v.dtype), v)
        return o.reshape(b, Q_BLOCK, h * MLA_V)

    out = lax.map(one_block, jnp.arange(s // Q_BLOCK))
    return out.transpose(1, 0, 2, 3).reshape(b, s, h * MLA_V)


def setup_inputs(seed: int = 0) -> dict:
    key = jax.random.key(seed)
    ks = jax.random.split(key, 32)

    def dense(k, shape, fan_in):
        return jax.random.normal(k, shape, jnp.float32) * fan_in ** -0.5

    def gain(k, n):
        return 1.0 + 0.05 * jax.random.normal(k, (n,), jnp.float32)

    d_nsa = NSA_HEADS * NSA_DV
    d_mla = MLA_HEADS * MLA_V
    return {
        'x': jax.random.normal(ks[0], (BATCH, SEQ, D_MODEL), jnp.float32),
        'ff1_pre_g': gain(ks[1], D_MODEL),
        'ff1_post_g': gain(ks[2], D_MODEL),
        'ff1_w_gate': dense(ks[3], (D_MODEL, D_FF), D_MODEL),
        'ff1_w_up': dense(ks[4], (D_MODEL, D_FF), D_MODEL),
        'ff1_w_down': dense(ks[5], (D_FF, D_MODEL), D_FF),
        'mix_pre_g': gain(ks[6], D_MODEL),
        'mix_post_g': gain(ks[7], D_MODEL),
        'w_in': dense(ks[8], (D_MODEL, D_IN), D_MODEL),
        'cmp_pos_k': 0.1 * jax.random.normal(ks[9], (CMP_LEN, NSA_DK), jnp.float32),
        'cmp_w1_k': dense(ks[10], (CMP_LEN * NSA_DK, CMP_HID), CMP_LEN * NSA_DK),
        'cmp_w2_k': dense(ks[11], (CMP_HID, NSA_DK), CMP_HID),
        'cmp_pos_v': 0.1 * jax.random.normal(ks[12], (CMP_LEN, NSA_DV), jnp.float32),
        'cmp_w1_v': dense(ks[13], (CMP_LEN * NSA_DV, CMP_HID), CMP_LEN * NSA_DV),
        'cmp_w2_v': dense(ks[14], (CMP_HID, NSA_DV), CMP_HID),
        'mla_q_norm_g': gain(ks[15], MLA_Q_RANK),
        'mla_w_uq': dense(ks[16], (MLA_Q_RANK, MLA_HEADS * (MLA_NOPE + MLA_ROPE)), MLA_Q_RANK),
        'mla_kv_norm_g': gain(ks[17], MLA_KV_RANK),
        'mla_w_ukv': dense(ks[18], (MLA_KV_RANK, MLA_HEADS * (MLA_NOPE + MLA_V)), MLA_KV_RANK),
        'w_proj_nsa': dense(ks[19], (d_nsa, D_MODEL), d_nsa),
        'w_proj_mla': dense(ks[20], (d_mla, D_MODEL), d_mla),
        'w_out': dense(ks[21], (D_MODEL, D_MODEL), D_MODEL),
        'ff2_pre_g': gain(ks[22], D_MODEL),
        'ff2_post_g': gain(ks[23], D_MODEL),
        'ff2_w_gate': dense(ks[24], (D_MODEL, D_FF), D_MODEL),
        'ff2_w_up': dense(ks[25], (D_MODEL, D_FF), D_MODEL),
        'ff2_w_down': dense(ks[26], (D_FF, D_MODEL), D_FF),
    }


def reference(x, ff1_pre_g, ff1_post_g, ff1_w_gate, ff1_w_up, ff1_w_down, mix_pre_g, mix_post_g, w_in,
              cmp_pos_k, cmp_w1_k, cmp_w2_k, cmp_pos_v, cmp_w1_v, cmp_w2_v,
              mla_q_norm_g, mla_w_uq, mla_kv_norm_g, mla_w_ukv, w_proj_nsa, w_proj_mla, w_out,
              ff2_pre_g, ff2_post_g, ff2_w_gate, ff2_w_up, ff2_w_down):
    b, s, _ = x.shape
    split_at = [int(v) for v in np.cumsum(IN_SIZES)[:-1]]
    for _layer in range(DEPTH):
        x = x + 0.5 * rms_norm(swiglu(rms_norm(x, ff1_pre_g), ff1_w_gate, ff1_w_up, ff1_w_down), ff1_post_g)
        hmix = rms_norm(x, mix_pre_g)
        z = hmix @ w_in
        q_nsa, kv_nsa, g_nsa, c_q, c_kv, k_pe, g_merge = jnp.split(z, split_at, axis=-1)
        q_nsa = q_nsa.reshape(b, s, NSA_HEADS, NSA_DK)
        kv_nsa = kv_nsa.reshape(b, s, 6, NSA_KV_GROUPS, NSA_DK)
        g_nsa = g_nsa.reshape(b, s, NSA_HEADS, 3)
        y_nsa = nsa_attention(q_nsa, kv_nsa[:, :, 0], kv_nsa[:, :, 1], kv_nsa[:, :, 2], kv_nsa[:, :, 3],
                              kv_nsa[:, :, 4], kv_nsa[:, :, 5], g_nsa,
                              cmp_pos_k, cmp_w1_k, cmp_w2_k, cmp_pos_v, cmp_w1_v, cmp_w2_v)
        y_mla = mla_attention(c_q, c_kv, k_pe, mla_q_norm_g, mla_w_uq, mla_kv_norm_g, mla_w_ukv)
        gate_a, gate_b = jnp.split(jax.nn.sigmoid(g_merge.astype(jnp.float32)), 2, axis=-1)
        merged = (gate_a * (y_nsa @ w_proj_nsa) + gate_b * (y_mla @ w_proj_mla)).astype(x.dtype)
        x = x + rms_norm(merged @ w_out, mix_post_g)
        x = x + 0.5 * rms_norm(swiglu(rms_norm(x, ff2_pre_g), ff2_w_gate, ff2_w_up, ff2_w_down), ff2_post_g)
    return x
```

```python
import functools

import numpy as np
import jax
import jax.numpy as jnp
from jax import lax
from jax.experimental import pallas as pl
from jax.experimental.pallas import tpu as pltpu

F32 = jnp.float32
BF16 = jnp.bfloat16

EPS = 1e-6
NEG = -1e30
FORCE_SCORE = 1e4
NSA_HEADS = 8
NSA_GROUPS = 2
NSA_HPG = NSA_HEADS // NSA_GROUPS
NSA_D = 64
CMP_LEN = 32
CMP_STRIDE = 16
CMP_HID = 256
SEL_LEN = 64
SEL_SHIFT = 6
SEL_TOPK = 16
WINDOW = 512
MLA_HEADS = 8
MLA_NOPE = 64
MLA_ROPE = 32
MLA_V = 64
MLA_Q_RANK = 256
MLA_KV_RANK = 128
ROPE_THETA = 10000.0
LANES = 128
HALF = LANES // 2
VMEM_LIMIT = 56 * 1024 * 1024


def _tiles(seq):
    return dict(
        tm=min(512, seq),
        tq_nsa=min(128, seq),
        tk_sel=min(256, seq),
        tk_win=min(128, seq),
        tq_mla=min(512, seq),
        tk_mla=min(256, seq),
    )


def _params(*sem):
    return pltpu.CompilerParams(dimension_semantics=sem, vmem_limit_bytes=VMEM_LIMIT)


def _const_spec(shape):
    nd = len(shape)
    return pl.BlockSpec(shape, lambda *_: (0,) * nd)


def _rms(x, g):
    return x * lax.rsqrt(jnp.mean(x * x, axis=-1, keepdims=True) + EPS) * g


def _dot(a, b):
    return jnp.dot(a, b, preferred_element_type=F32)


def _dot_nt(a, b):
    return lax.dot_general(a, b, (((1,), (1,)), ((), ())), preferred_element_type=F32)


def _ffn_kernel(x_ref, pre_ref, post_ref, wg_ref, wu_ref, wd_ref, o_ref, *, fc):
    x = x_ref[...]
    h = _rms(x, pre_ref[...]).astype(BF16)
    acc = jnp.zeros(x.shape, F32)
    for c in range(wg_ref.shape[1] // fc):
        gate = _dot(h, wg_ref[:, c * fc:(c + 1) * fc])
        up = _dot(h, wu_ref[:, c * fc:(c + 1) * fc])
        act = (gate * jax.nn.sigmoid(gate) * up).astype(BF16)
        acc = acc + _dot(act, wd_ref[c * fc:(c + 1) * fc, :])
    o_ref[...] = x + 0.5 * _rms(acc, post_ref[...])


def _ffn(x2d, pre_g, post_g, w_gate, w_up, w_down, tm):
    n, d = x2d.shape
    f = w_gate.shape[1]
    fc = f // 2 if (f // 2) % LANES == 0 else f
    return pl.pallas_call(
        functools.partial(_ffn_kernel, fc=fc),
        grid=(n // tm,),
        in_specs=[pl.BlockSpec((tm, d), lambda i: (i, 0)),
                  _const_spec((1, d)), _const_spec((1, d)),
                  _const_spec((d, f)), _const_spec((d, f)), _const_spec((f, d))],
        out_specs=pl.BlockSpec((tm, d), lambda i: (i, 0)),
        out_shape=jax.ShapeDtypeStruct((n, d), F32),
        compiler_params=_params("parallel"),
        name="ffn",
    )(x2d, pre_g.reshape(1, d), post_g.reshape(1, d),
      w_gate.astype(BF16), w_up.astype(BF16), w_down.astype(BF16))


_C_QN = 0
_C_KSLC = _C_QN + NSA_HEADS * LANES
_C_VSLC = _C_KSLC + LANES
_C_VSLC_SW = _C_VSLC + LANES
_C_KWIN = _C_VSLC_SW + LANES
_C_VWIN = _C_KWIN + LANES
_C_VWIN_SW = _C_VWIN + LANES
_C_CMP = _C_VWIN_SW + LANES
_C_CQ = _C_CMP + 2 * LANES
_C_CKV = _C_CQ + MLA_Q_RANK
_C_MISC = _C_CKV + MLA_KV_RANK
_C_END = _C_MISC + 2 * LANES


def _inproj_weights(w_in, w_uq, w_ukv):
    d = w_in.shape[0]
    o_q = 0
    o_kv = o_q + NSA_HEADS * NSA_D
    o_g = o_kv + 6 * NSA_GROUPS * NSA_D
    o_cq = o_g + 3 * NSA_HEADS
    o_ckv = o_cq + MLA_Q_RANK
    o_kpe = o_ckv + MLA_KV_RANK
    zeros = lambda n: jnp.zeros((d, n), w_in.dtype)
    cols = []
    for h in range(NSA_HEADS):
        wq = w_in[:, o_q + h * NSA_D:o_q + (h + 1) * NSA_D]
        cols += [wq, zeros(HALF)] if h // NSA_HPG == 0 else [zeros(HALF), wq]
    kv = lambda j, g: w_in[:, o_kv + (j * NSA_GROUPS + g) * NSA_D:o_kv + (j * NSA_GROUPS + g + 1) * NSA_D]
    cols += [kv(2, 0), kv(2, 1)]
    cols += [kv(3, 0), kv(3, 1), kv(3, 1), kv(3, 0)]
    cols += [kv(4, 0), kv(4, 1)]
    cols += [kv(5, 0), kv(5, 1), kv(5, 1), kv(5, 0)]
    cols += [kv(0, 0), kv(0, 1), kv(1, 0), kv(1, 1)]
    cols += [w_in[:, o_cq:o_cq + MLA_Q_RANK], w_in[:, o_ckv:o_ckv + MLA_KV_RANK]]
    half = MLA_ROPE // 2
    kp1 = w_in[:, o_kpe:o_kpe + half]
    kp2 = w_in[:, o_kpe + half:o_kpe + MLA_ROPE]
    gates = w_in[:, o_g:o_g + 3 * NSA_HEADS]
    cols += [gates, zeros(MLA_NOPE - 3 * NSA_HEADS), kp1, kp2, zeros(LANES - MLA_NOPE - MLA_ROPE)]
    cols += [zeros(MLA_NOPE), kp2, kp1, zeros(LANES - MLA_NOPE - MLA_ROPE)]
    w_b = jnp.concatenate(cols, axis=1).astype(BF16)
    assert w_b.shape[1] == _C_END

    dq = MLA_NOPE + MLA_ROPE
    zq = lambda n: jnp.zeros((w_uq.shape[0], n), w_uq.dtype)
    qa, qs = [], []
    for h in range(MLA_HEADS):
        nope = w_uq[:, h * dq:h * dq + MLA_NOPE]
        r1 = w_uq[:, h * dq + MLA_NOPE:h * dq + MLA_NOPE + half]
        r2 = w_uq[:, h * dq + MLA_NOPE + half:(h + 1) * dq]
        qa += [nope, r1, r2, zq(LANES - dq)]
        qs += [zq(MLA_NOPE), r2, r1, zq(LANES - dq)]
    w_q2 = jnp.concatenate(qa + qs, axis=1).astype(BF16)

    dkv = MLA_NOPE + MLA_V
    zk = lambda n: jnp.zeros((w_ukv.shape[0], n), w_ukv.dtype)
    ka, va = [], []
    for h in range(MLA_HEADS):
        ka += [w_ukv[:, h * dkv:h * dkv + MLA_NOPE], zk(LANES - MLA_NOPE)]
        va += [w_ukv[:, h * dkv + MLA_NOPE:(h + 1) * dkv]]
    w_kv2 = jnp.concatenate(ka + va, axis=1).astype(BF16)
    return w_b, w_q2, w_kv2


def _rope_tables(seq):
    half = MLA_ROPE // 2
    freqs = jnp.asarray(ROPE_THETA ** (-np.arange(half, dtype=np.float32) / half), F32)
    ang = jnp.arange(seq).astype(F32)[:, None] * freqs[None, :]
    cos, sin = jnp.cos(ang), jnp.sin(ang)
    pad = jnp.zeros((seq, LANES - MLA_NOPE - MLA_ROPE), F32)
    cq = jnp.concatenate([jnp.ones((seq, MLA_NOPE), F32), cos, cos, pad], axis=1)
    ck = jnp.concatenate([jnp.zeros((seq, MLA_NOPE), F32), cos, cos, pad], axis=1)
    sn = jnp.concatenate([jnp.zeros((seq, MLA_NOPE), F32), -sin, sin, pad], axis=1)
    return cq, ck, sn


def _inproj_kernel(x_ref, g_ref, wb_ref, qg_ref, wq2_ref, kvg_ref, wkv2_ref, cq_ref, ck_ref, sn_ref,
                   qn_ref, ksa_ref, vs_ref, vss_ref, kw_ref, vw_ref, vws_ref, cmp_ref, gates_ref,
                   qm_ref, km_ref, vm_ref, *, tm):
    h = _rms(x_ref[...], g_ref[...]).astype(BF16)
    z = _dot(h, wb_ref[...])
    qn_ref[...] = (z[:, _C_QN:_C_KSLC] * NSA_D ** -0.5).astype(BF16)
    pos = pl.program_id(1) * tm + lax.broadcasted_iota(jnp.int32, (tm, LANES), 0)
    onehot = (lax.broadcasted_iota(jnp.int32, (tm, LANES), 1) == lax.shift_right_logical(pos, SEL_SHIFT))
    ksa_ref[:, :LANES] = onehot.astype(BF16)
    ksa_ref[:, LANES:] = z[:, _C_KSLC:_C_VSLC].astype(BF16)
    vs_ref[...] = z[:, _C_VSLC:_C_VSLC_SW].astype(BF16)
    vss_ref[...] = z[:, _C_VSLC_SW:_C_KWIN].astype(BF16)
    kw_ref[...] = z[:, _C_KWIN:_C_VWIN].astype(BF16)
    vw_ref[...] = z[:, _C_VWIN:_C_VWIN_SW].astype(BF16)
    vws_ref[...] = z[:, _C_VWIN_SW:_C_CMP].astype(BF16)
    for j in range(4):
        cmp_ref[j] = z[:, _C_CMP + j * NSA_D:_C_CMP + (j + 1) * NSA_D]
    misc = z[:, _C_MISC:_C_MISC + LANES]
    misc_sw = z[:, _C_MISC + LANES:_C_END]
    gates_ref[...] = jax.nn.sigmoid(misc)
    cqn = _rms(z[:, _C_CQ:_C_CKV], qg_ref[...]).astype(BF16)
    q2 = _dot(cqn, wq2_ref[...])
    cq, ck, sn = cq_ref[...], ck_ref[...], sn_ref[...]
    nq = MLA_HEADS * LANES
    scale = (MLA_NOPE + MLA_ROPE) ** -0.5
    for hh in range(MLA_HEADS):
        qa = q2[:, hh * LANES:(hh + 1) * LANES]
        qs = q2[:, nq + hh * LANES:nq + (hh + 1) * LANES]
        qm_ref[hh] = ((qa * cq + qs * sn) * scale).astype(BF16)
    ckvn = _rms(z[:, _C_CKV:_C_MISC], kvg_ref[...]).astype(BF16)
    kv2 = _dot(ckvn, wkv2_ref[...])
    krot = misc * ck + misc_sw * sn
    for hh in range(MLA_HEADS):
        km_ref[hh] = (kv2[:, hh * LANES:(hh + 1) * LANES] + krot).astype(BF16)
    for hp in range(MLA_HEADS // 2):
        vm_ref[hp] = kv2[:, nq + hp * LANES:nq + (hp + 1) * LANES].astype(BF16)


def _inproj(x1, mix_pre_g, w_b, q_norm_g, w_q2, kv_norm_g, w_kv2, tables, tm):
    b, s, d = x1.shape
    cq, ck, sn = tables
    tok = lambda w: pl.BlockSpec((None, tm, w), lambda bi, i: (bi, i, 0))
    tab = pl.BlockSpec((tm, LANES), lambda bi, i: (i, 0))
    heads = lambda n, w: pl.BlockSpec((None, n, tm, w), lambda bi, i: (bi, 0, i, 0))
    sds = jax.ShapeDtypeStruct
    return pl.pallas_call(
        functools.partial(_inproj_kernel, tm=tm),
        grid=(b, s // tm),
        in_specs=[tok(d), _const_spec((1, d)), _const_spec(w_b.shape),
                  _const_spec((1, MLA_Q_RANK)), _const_spec(w_q2.shape),
                  _const_spec((1, MLA_KV_RANK)), _const_spec(w_kv2.shape), tab, tab, tab],
        out_specs=[tok(NSA_HEADS * LANES), tok(2 * LANES), tok(LANES), tok(LANES), tok(LANES),
                   tok(LANES), tok(LANES), heads(4, NSA_D), tok(LANES),
                   heads(MLA_HEADS, LANES), heads(MLA_HEADS, LANES), heads(MLA_HEADS // 2, LANES)],
        out_shape=[sds((b, s, NSA_HEADS * LANES), BF16), sds((b, s, 2 * LANES), BF16),
                   sds((b, s, LANES), BF16), sds((b, s, LANES), BF16), sds((b, s, LANES), BF16),
                   sds((b, s, LANES), BF16), sds((b, s, LANES), BF16), sds((b, 4, s, NSA_D), F32),
                   sds((b, s, LANES), F32),
                   sds((b, MLA_HEADS, s, LANES), BF16), sds((b, MLA_HEADS, s, LANES), BF16),
                   sds((b, MLA_HEADS // 2, s, LANES), BF16)],
        compiler_params=_params("parallel", "parallel"),
        name="inproj",
    )(x1, mix_pre_g.reshape(1, d), w_b, q_norm_g.reshape(1, -1), w_q2, kv_norm_g.reshape(1, -1), w_kv2,
      cq, ck, sn)


def _compress_kernel(c_ref, pos_ref, w1_ref, w2_ref, o_ref):
    nc = c_ref.shape[1]
    res = []
    for g in range(NSA_GROUPS):
        c = c_ref[g]
        a0 = _dot((c + pos_ref[0]).astype(BF16), w1_ref[0])
        a1 = _dot((c + pos_ref[1]).astype(BF16), w1_ref[1])
        pre = a0 + pltpu.roll(a1, nc - 1, 0)
        res.append(_dot(jax.nn.gelu(pre).astype(BF16), w2_ref[...]))
    o_ref[0] = jnp.concatenate(res, axis=1).astype(BF16)
    o_ref[1] = jnp.concatenate(res[::-1], axis=1).astype(BF16)


def _compress(cmp_in, pos_k, w1_k, w2_k, pos_v, w1_v, w2_v):
    b, _, s, dk = cmp_in.shape
    nc = s // CMP_STRIDE
    kdim = CMP_STRIDE * dk
    c5 = cmp_in.reshape(b, 2, NSA_GROUPS, nc, kdim)
    pos = jnp.stack([pos_k.reshape(2, 1, kdim), pos_v.reshape(2, 1, kdim)])
    w1 = jnp.stack([w1_k.reshape(2, kdim, CMP_HID), w1_v.reshape(2, kdim, CMP_HID)]).astype(BF16)
    w2 = jnp.stack([w2_k, w2_v]).astype(BF16)
    return pl.pallas_call(
        _compress_kernel,
        grid=(b, 2),
        in_specs=[pl.BlockSpec((None, None, NSA_GROUPS, nc, kdim), lambda bi, kv: (bi, kv, 0, 0, 0)),
                  pl.BlockSpec((None, 2, 1, kdim), lambda bi, kv: (kv, 0, 0, 0)),
                  pl.BlockSpec((None, 2, kdim, CMP_HID), lambda bi, kv: (kv, 0, 0, 0)),
                  pl.BlockSpec((None, CMP_HID, dk), lambda bi, kv: (kv, 0, 0))],
        out_specs=pl.BlockSpec((None, None, 2, nc, LANES), lambda bi, kv: (bi, kv, 0, 0, 0)),
        out_shape=jax.ShapeDtypeStruct((b, 2, 2, nc, LANES), BF16),
        compiler_params=_params("parallel", "parallel"),
        name="compress",
    )(c5, pos, w1, w2)


def _alibi_slopes():
    return [float(2.0 ** (-8.0 * (i + 1) / NSA_HEADS)) for i in range(NSA_HEADS)]


def _pair_select(even, odd):
    lane = lax.broadcasted_iota(jnp.int32, even.shape, 1)
    return jnp.where(lane < HALF, even, odd)


def _nsa_cmp_kernel(q_ref, kcv_ref, ov_ref, oc_ref, sb_ref, *, tq):
    nc = kcv_ref.shape[2]
    t = pl.program_id(1) * tq + lax.broadcasted_iota(jnp.int32, (tq, 1), 0)
    cmp_end = lax.broadcasted_iota(jnp.int32, (1, nc), 1) * CMP_STRIDE + (CMP_LEN - 1)
    dist = t - cmp_end
    valid = dist >= 0
    validf = valid.astype(F32)
    distf = dist.astype(F32)
    slopes = _alibi_slopes()
    kc = kcv_ref[0, 0]
    vc = (kcv_ref[1, 0], kcv_ref[1, 1])
    blk = lax.broadcasted_iota(jnp.int32, (tq, LANES), 1)
    blkf = blk.astype(F32)
    cur = lax.shift_right_logical(t, SEL_SHIFT)
    forced = (blk == 0) | (blk == cur) | (blk == cur - 1)
    for g in range(NSA_GROUPS):
        psum = jnp.zeros((tq, nc), F32)
        outs = []
        for hh in range(NSA_HPG):
            h = g * NSA_HPG + hh
            s = _dot_nt(q_ref[:, h * LANES:(h + 1) * LANES], kc) - slopes[h] * distf
            s = jnp.where(valid, s, NEG)
            e = jnp.exp(s - jnp.max(s, axis=-1, keepdims=True)) * validf
            p = e / jnp.maximum(jnp.sum(e, axis=-1, keepdims=True), 1e-30)
            psum = psum + p
            outs.append(_dot(p.astype(BF16), vc[(g + hh) % 2]))
        for j in range(NSA_HPG // 2):
            col = (g * NSA_HPG) // 2 + j
            oc_ref[:, col * LANES:(col + 1) * LANES] = _pair_select(outs[2 * j], outs[2 * j + 1])
        imp = _dot(psum.astype(BF16), ov_ref[...])
        work = jnp.where(blk <= cur, jnp.where(forced, FORCE_SCORE, imp), NEG)
        chosen = jnp.zeros((tq, LANES), jnp.bool_)
        for _ in range(SEL_TOPK):
            top = jnp.max(work, axis=-1, keepdims=True)
            idx = jnp.min(jnp.where(work == top, blkf, float(LANES)), axis=-1, keepdims=True)
            hit = blkf == idx
            chosen = chosen | hit
            work = jnp.where(hit, -jnp.inf, work)
        sb_ref[g] = jnp.where(chosen, 0.0, NEG).astype(BF16)


def _nsa_cmp(qn, kcv, overlap, tq):
    b, s, _ = qn.shape
    nc = kcv.shape[3]
    return pl.pallas_call(
        functools.partial(_nsa_cmp_kernel, tq=tq),
        grid=(b, s // tq),
        in_specs=[pl.BlockSpec((None, tq, NSA_HEADS * LANES), lambda bi, i: (bi, i, 0)),
                  pl.BlockSpec((None, 2, 2, nc, LANES), lambda bi, i: (bi, 0, 0, 0, 0)),
                  _const_spec(overlap.shape)],
        out_specs=[pl.BlockSpec((None, tq, NSA_HEADS * NSA_D), lambda bi, i: (bi, i, 0)),
                   pl.BlockSpec((None, NSA_GROUPS, tq, LANES), lambda bi, i: (bi, 0, i, 0))],
        out_shape=[jax.ShapeDtypeStruct((b, s, NSA_HEADS * NSA_D), F32),
                   jax.ShapeDtypeStruct((b, NSA_GROUPS, s, LANES), BF16)],
        compiler_params=_params("parallel", "parallel"),
        name="nsa_cmp",
    )(qn, kcv, overlap)


def _overlap_matrix(seq):
    n_c = (seq - CMP_LEN) // CMP_STRIDE + 1
    n_sel = seq // SEL_LEN
    c0 = np.arange(n_c) * CMP_STRIDE
    s0 = np.arange(n_sel) * SEL_LEN
    ov = np.clip(np.minimum(c0[:, None] + CMP_LEN, s0[None, :] + SEL_LEN)
                 - np.maximum(c0[:, None], s0[None, :]), 0, None) / CMP_LEN
    full = np.zeros((seq // CMP_STRIDE, LANES), np.float32)
    full[:n_c, :n_sel] = ov
    return jnp.asarray(full, BF16)


_HEAD_ORDER = (0, 2, 1, 3)


def _online_step(s, v_lo, v_hi, m, l, acc):
    half = s.shape[0] // 2
    m_new = jnp.maximum(m, jnp.max(s, axis=-1, keepdims=True))
    alpha = jnp.exp(m - m_new)
    p = jnp.exp(s - m_new)
    l_new = alpha * l + jnp.sum(p, axis=-1, keepdims=True)
    pb = p.astype(BF16)
    pv = jnp.concatenate([_dot(pb[:half], v_lo), _dot(pb[half:], v_hi)], axis=0)
    return m_new, l_new, alpha * acc + pv


def _nsa_main_kernel(q_ref, sb_ref, oc_ref, gates_ref, ksa_ref, vs_ref, vss_ref, kw_ref, vw_ref, vws_ref,
                     y_ref, *, tq, tk_sel, tk_win):
    i = pl.program_id(1)
    t0 = i * tq
    rows = NSA_HPG * tq
    slopes = _alibi_slopes()
    t_row = t0 + lax.broadcasted_iota(jnp.int32, (tq, 1), 0)
    t_rows = jnp.concatenate([t_row] * NSA_HPG, axis=0)
    gates = gates_ref[...]
    for g in range(NSA_GROUPS):
        heads = [g * NSA_HPG + hh for hh in _HEAD_ORDER]
        slope_rows = jnp.concatenate(
            [jnp.full((tq, 1), slopes[h], F32) for h in heads], axis=0)
        q_pad = jnp.concatenate([q_ref[:, h * LANES:(h + 1) * LANES] for h in heads], axis=0)
        q_aug = jnp.concatenate([jnp.concatenate([sb_ref[g]] * NSA_HPG, axis=0), q_pad], axis=1)
        v_lo_ref, v_hi_ref = (vs_ref, vss_ref) if g == 0 else (vss_ref, vs_ref)
        w_lo_ref, w_hi_ref = (vw_ref, vws_ref) if g == 0 else (vws_ref, vw_ref)
        init = (jnp.full((rows, 1), NEG, F32), jnp.zeros((rows, 1), F32), jnp.zeros((rows, LANES), F32))

        def sel_body(kt, carry):
            k0 = pl.multiple_of(kt * tk_sel, tk_sel)
            s = _dot_nt(q_aug, ksa_ref[pl.ds(k0, tk_sel), :])
            rel = k0 + lax.broadcasted_iota(jnp.int32, (1, tk_sel), 1) - t_rows
            s = jnp.where(rel <= 0, s + slope_rows * rel.astype(F32), NEG)
            return _online_step(s, v_lo_ref[pl.ds(k0, tk_sel), :], v_hi_ref[pl.ds(k0, tk_sel), :], *carry)

        n_sel_tiles = (t0 + tq + tk_sel - 1) // tk_sel
        _, l_s, acc_s = lax.fori_loop(0, n_sel_tiles, sel_body, init)

        def win_body(kt, carry):
            k0 = pl.multiple_of(kt * tk_win, tk_win)
            s = _dot_nt(q_pad, kw_ref[pl.ds(k0, tk_win), :])
            rel = k0 + lax.broadcasted_iota(jnp.int32, (1, tk_win), 1) - t_rows
            s = jnp.where((rel <= 0) & (rel > -WINDOW), s + slope_rows * rel.astype(F32), NEG)
            return _online_step(s, w_lo_ref[pl.ds(k0, tk_win), :], w_hi_ref[pl.ds(k0, tk_win), :], *carry)

        first_win = jnp.maximum(t0 - WINDOW + 1, 0) // tk_win
        n_win_end = (t0 + tq + tk_win - 1) // tk_win
        _, l_w, acc_w = lax.fori_loop(first_win, n_win_end, win_body, init)

        o_s = acc_s / l_s
        o_w = acc_w / l_w
        for j in range(NSA_HPG // 2):
            col = (g * NSA_HPG) // 2 + j
            h_even, h_odd = g * NSA_HPG + 2 * j, g * NSA_HPG + 2 * j + 1
            sel = _pair_select(o_s[j * tq:(j + 1) * tq], o_s[(2 + j) * tq:(3 + j) * tq])
            win = _pair_select(o_w[j * tq:(j + 1) * tq], o_w[(2 + j) * tq:(3 + j) * tq])
            cmp_o = oc_ref[:, col * LANES:(col + 1) * LANES]
            gate = lambda br: _pair_select(
                jnp.broadcast_to(gates[:, 3 * h_even + br:3 * h_even + br + 1], (tq, LANES)),
                jnp.broadcast_to(gates[:, 3 * h_odd + br:3 * h_odd + br + 1], (tq, LANES)))
            y_ref[:, col * LANES:(col + 1) * LANES] = (
                gate(0) * cmp_o + gate(1) * sel + gate(2) * win).astype(BF16)


def _nsa_main(qn, selbias, o_cmp, gates, ksa, vs, vss, kw, vw, vws, tq, tk_sel, tk_win):
    b, s, _ = qn.shape
    tok = lambda w: pl.BlockSpec((None, tq, w), lambda bi, i: (bi, i, 0))
    seq = lambda w: pl.BlockSpec((None, s, w), lambda bi, i: (bi, 0, 0))
    return pl.pallas_call(
        functools.partial(_nsa_main_kernel, tq=tq, tk_sel=tk_sel, tk_win=tk_win),
        grid=(b, s // tq),
        in_specs=[tok(NSA_HEADS * LANES),
                  pl.BlockSpec((None, NSA_GROUPS, tq, LANES), lambda bi, i: (bi, 0, i, 0)),
                  tok(NSA_HEADS * NSA_D), tok(LANES),
                  seq(2 * LANES), seq(LANES), seq(LANES), seq(LANES), seq(LANES), seq(LANES)],
        out_specs=tok(NSA_HEADS * NSA_D),
        out_shape=jax.ShapeDtypeStruct((b, s, NSA_HEADS * NSA_D), BF16),
        compiler_params=_params("parallel", "arbitrary"),
        name="nsa_main",
    )(qn, selbias, o_cmp, gates, ksa, vs, vss, kw, vw, vws)


def _mla_kernel(q_ref, k_ref, v_ref, y_ref, *, tq, tk):
    i = pl.program_id(2)
    t0 = i * tq
    t_row = t0 + lax.broadcasted_iota(jnp.int32, (tq, 1), 0)
    n_tiles = (t0 + tq + tk - 1) // tk
    outs = []
    for hh in range(2):
        q = q_ref[hh]

        def body(kt, carry):
            m, l, acc = carry
            k0 = pl.multiple_of(kt * tk, tk)
            s = _dot_nt(q, k_ref[hh, pl.ds(k0, tk), :])
            key = k0 + lax.broadcasted_iota(jnp.int32, (1, tk), 1)
            s = jnp.where(key <= t_row, s, NEG)
            m_new = jnp.maximum(m, jnp.max(s, axis=-1, keepdims=True))
            alpha = jnp.exp(m - m_new)
            p = jnp.exp(s - m_new)
            l_new = alpha * l + jnp.sum(p, axis=-1, keepdims=True)
            acc_new = alpha * acc + _dot(p.astype(BF16), v_ref[pl.ds(k0, tk), :])
            return m_new, l_new, acc_new

        init = (jnp.full((tq, 1), NEG, F32), jnp.zeros((tq, 1), F32), jnp.zeros((tq, LANES), F32))
        _, l, acc = lax.fori_loop(0, n_tiles, body, init)
        outs.append(acc / l)
    y_ref[...] = _pair_select(outs[0], outs[1]).astype(BF16)


def _mla(qm, km, vm, tq, tk):
    b, h, s, _ = qm.shape
    q5 = qm.reshape(b, h // 2, 2, s, LANES)
    k5 = km.reshape(b, h // 2, 2, s, LANES)
    return pl.pallas_call(
        functools.partial(_mla_kernel, tq=tq, tk=tk),
        grid=(b, h // 2, s // tq),
        in_specs=[pl.BlockSpec((None, None, 2, tq, LANES), lambda bi, hp, i: (bi, hp, 0, i, 0)),
                  pl.BlockSpec((None, None, 2, s, LANES), lambda bi, hp, i: (bi, hp, 0, 0, 0)),
                  pl.BlockSpec((None, None, s, LANES), lambda bi, hp, i: (bi, hp, 0, 0))],
        out_specs=pl.BlockSpec((None, tq, LANES), lambda bi, hp, i: (bi, i, hp)),
        out_shape=jax.ShapeDtypeStruct((b, s, h * MLA_V), BF16),
        compiler_params=_params("parallel", "parallel", "arbitrary"),
        name="mla",
    )(q5, k5, vm)


def _merge_kernel(x_ref, yn_ref, ym_ref, pre_ref, post_ref, wgm_ref, wpn_ref, wpm_ref, wo_ref, o_ref):
    x = x_ref[...]
    d = x.shape[1]
    h = _rms(x, pre_ref[...]).astype(BF16)
    gm = jax.nn.sigmoid(_dot(h, wgm_ref[...]))
    merged = gm[:, :d] * _dot(yn_ref[...], wpn_ref[...]) + gm[:, d:] * _dot(ym_ref[...], wpm_ref[...])
    y = _dot(merged.astype(BF16), wo_ref[...])
    o_ref[...] = x + _rms(y, post_ref[...])


def _merge(x1, y_nsa, y_mla, pre_g, post_g, w_gm, w_pn, w_pm, w_out, tm):
    n, d = x1.shape
    tok = lambda w: pl.BlockSpec((tm, w), lambda i: (i, 0))
    return pl.pallas_call(
        _merge_kernel,
        grid=(n // tm,),
        in_specs=[tok(d), tok(y_nsa.shape[1]), tok(y_mla.shape[1]), _const_spec((1, d)), _const_spec((1, d)),
                  _const_spec(w_gm.shape), _const_spec(w_pn.shape), _const_spec(w_pm.shape),
                  _const_spec(w_out.shape)],
        out_specs=tok(d),
        out_shape=jax.ShapeDtypeStruct((n, d), F32),
        compiler_params=_params("parallel"),
        name="merge",
    )(x1, y_nsa, y_mla, pre_g.reshape(1, d), post_g.reshape(1, d),
      w_gm.astype(BF16), w_pn.astype(BF16), w_pm.astype(BF16), w_out.astype(BF16))


def kernel(x, ff1_pre_g, ff1_post_g, ff1_w_gate, ff1_w_up, ff1_w_down, mix_pre_g, mix_post_g, w_in, cmp_pos_k, cmp_w1_k, cmp_w2_k, cmp_pos_v, cmp_w1_v, cmp_w2_v, mla_q_norm_g, mla_w_uq, mla_kv_norm_g, mla_w_ukv, w_proj_nsa, w_proj_mla, w_out, ff2_pre_g, ff2_post_g, ff2_w_gate, ff2_w_up, ff2_w_down):
    b, s, d = x.shape
    assert s % (SEL_LEN * SEL_TOPK) == 0 and s // SEL_LEN <= LANES
    tl = _tiles(s)
    n = b * s

    x1 = _ffn(x.reshape(n, d), ff1_pre_g, ff1_post_g, ff1_w_gate, ff1_w_up, ff1_w_down, tl["tm"])

    w_b, w_q2, w_kv2 = _inproj_weights(w_in, mla_w_uq, mla_w_ukv)
    (qn, ksa, vs, vss, kw, vw, vws, cmp_in, gates, qm, km, vm) = _inproj(
        x1.reshape(b, s, d), mix_pre_g, w_b, mla_q_norm_g, w_q2, mla_kv_norm_g, w_kv2,
        _rope_tables(s), tl["tm"])

    kcv = _compress(cmp_in, cmp_pos_k, cmp_w1_k, cmp_w2_k, cmp_pos_v, cmp_w1_v, cmp_w2_v)
    o_cmp, selbias = _nsa_cmp(qn, kcv, _overlap_matrix(s), tl["tq_nsa"])
    y_nsa = _nsa_main(qn, selbias, o_cmp, gates, ksa, vs, vss, kw, vw, vws,
                      tl["tq_nsa"], tl["tk_sel"], tl["tk_win"])
    y_mla = _mla(qm, km, vm, tl["tq_mla"], tl["tk_mla"])

    o_gm = sum((NSA_HEADS * NSA_D, 6 * NSA_GROUPS * NSA_D, 3 * NSA_HEADS, MLA_Q_RANK, MLA_KV_RANK, MLA_ROPE))
    x2 = _merge(x1, y_nsa.reshape(n, -1), y_mla.reshape(n, -1), mix_pre_g, mix_post_g,
                w_in[:, o_gm:], w_proj_nsa, w_proj_mla, w_out, tl["tm"])
    x3 = _ffn(x2, ff2_pre_g, ff2_post_g, ff2_w_gate, ff2_w_up, ff2_w_down, tl["tm"])
    return x3.reshape(b, s, d)
```

```python
import functools

import numpy as np
import jax
import jax.numpy as jnp
from jax import lax
from jax.experimental import pallas as pl
from jax.experimental.pallas import tpu as pltpu

F32 = jnp.float32
BF16 = jnp.bfloat16

EPS = 1e-6
NEG = -1e30
FORCE_SCORE = 1e4
NSA_HEADS = 8
NSA_GROUPS = 2
NSA_HPG = NSA_HEADS // NSA_GROUPS
NSA_D = 64
CMP_LEN = 32
CMP_STRIDE = 16
CMP_HID = 256
SEL_LEN = 64
SEL_SHIFT = 6
SEL_TOPK = 16
WINDOW = 512
MLA_HEADS = 8
MLA_NOPE = 64
MLA_ROPE = 32
MLA_V = 64
MLA_Q_RANK = 256
MLA_KV_RANK = 128
ROPE_THETA = 10000.0
LANES = 128
VMEM_LIMIT = 56 * 1024 * 1024


def _tiles(seq):
    return dict(
        tm=min(512, seq),
        tq_cmp=min(256, seq),
        tq_nsa=min(128, seq),
        tk_sel=min(256, seq),
        tk_win=LANES,
        tq_mla=min(512, seq),
        tk_mla=min(256, seq),
    )


def _params(*sem):
    return pltpu.CompilerParams(dimension_semantics=sem, vmem_limit_bytes=VMEM_LIMIT)


def _const_spec(shape):
    nd = len(shape)
    return pl.BlockSpec(shape, lambda *_: (0,) * nd)


def _rms(x, g):
    return x * lax.rsqrt(jnp.mean(x * x, axis=-1, keepdims=True) + EPS) * g


def _dot(a, b):
    return jnp.dot(a, b, preferred_element_type=F32)


def _dot_nt(a, b):
    return lax.dot_general(a, b, (((1,), (1,)), ((), ())), preferred_element_type=F32)


def _alibi_slopes():
    return [float(2.0 ** (-8.0 * (i + 1) / NSA_HEADS)) for i in range(NSA_HEADS)]


def _ffn_kernel(x_ref, pre_ref, post_ref, wg_ref, wu_ref, wd_ref, o_ref, *, fc):
    x = x_ref[...]
    h = _rms(x, pre_ref[...]).astype(BF16)
    acc = jnp.zeros(x.shape, F32)
    for c in range(wg_ref.shape[1] // fc):
        gate = _dot(h, wg_ref[:, c * fc:(c + 1) * fc])
        up = _dot(h, wu_ref[:, c * fc:(c + 1) * fc])
        act = (gate * jax.nn.sigmoid(gate) * up).astype(BF16)
        acc = acc + _dot(act, wd_ref[c * fc:(c + 1) * fc, :])
    o_ref[...] = x + 0.5 * _rms(acc, post_ref[...])


def _ffn(x2d, pre_g, post_g, w_gate, w_up, w_down, tm):
    n, d = x2d.shape
    f = w_gate.shape[1]
    fc = f // 2 if (f // 2) % LANES == 0 else f
    return pl.pallas_call(
        functools.partial(_ffn_kernel, fc=fc),
        grid=(n // tm,),
        in_specs=[pl.BlockSpec((tm, d), lambda i: (i, 0)),
                  _const_spec((1, d)), _const_spec((1, d)),
                  _const_spec((d, f)), _const_spec((d, f)), _const_spec((f, d))],
        out_specs=pl.BlockSpec((tm, d), lambda i: (i, 0)),
        out_shape=jax.ShapeDtypeStruct((n, d), F32),
        compiler_params=_params("parallel"),
        name="ffn",
    )(x2d, pre_g.reshape(1, d), post_g.reshape(1, d),
      w_gate.astype(BF16), w_up.astype(BF16), w_down.astype(BF16))


_T_KSEL = 0
_T_KWIN = _T_KSEL + 2 * LANES
_T_CMP = _T_KWIN + 2 * LANES
_T_CQ = _T_CMP + 2 * LANES
_T_CKV = _T_CQ + MLA_Q_RANK
_T_KPE = _T_CKV + MLA_KV_RANK
_T_END = _T_KPE + 2 * LANES
_F_QN = 0
_F_VSEL = _F_QN + NSA_HEADS * LANES
_F_VWIN = _F_VSEL + LANES
_F_GATE = _F_VWIN + LANES
_F_END = _F_GATE + 32


def _inproj_weights(w_in, w_uq, w_ukv):
    d = w_in.shape[0]
    o_q = 0
    o_kv = o_q + NSA_HEADS * NSA_D
    o_g = o_kv + 6 * NSA_GROUPS * NSA_D
    o_cq = o_g + 3 * NSA_HEADS
    o_ckv = o_cq + MLA_Q_RANK
    o_kpe = o_ckv + MLA_KV_RANK
    zeros = lambda n: jnp.zeros((d, n), w_in.dtype)
    kv = lambda j, g: w_in[:, o_kv + (j * NSA_GROUPS + g) * NSA_D:o_kv + (j * NSA_GROUPS + g + 1) * NSA_D]
    half = MLA_ROPE // 2
    kp1 = w_in[:, o_kpe:o_kpe + half]
    kp2 = w_in[:, o_kpe + half:o_kpe + MLA_ROPE]
    tail = LANES - MLA_NOPE - MLA_ROPE
    cols = [kv(2, 0), zeros(NSA_D), kv(2, 1), zeros(NSA_D),
            kv(4, 0), zeros(NSA_D), kv(4, 1), zeros(NSA_D),
            kv(0, 0), kv(0, 1), kv(1, 0), kv(1, 1),
            w_in[:, o_cq:o_cq + MLA_Q_RANK], w_in[:, o_ckv:o_ckv + MLA_KV_RANK],
            zeros(MLA_NOPE), kp1, kp2, zeros(tail), zeros(MLA_NOPE), kp2, kp1, zeros(tail)]
    w_tok = jnp.concatenate(cols, axis=1).astype(BF16)
    assert w_tok.shape[1] == _T_END
    rows = []
    for h in range(NSA_HEADS):
        wq = w_in[:, o_q + h * NSA_D:o_q + (h + 1) * NSA_D]
        rows += [wq, zeros(NSA_D)] if h // NSA_HPG == 0 else [zeros(NSA_D), wq]
    rows += [kv(3, 0), kv(3, 1), kv(5, 0), kv(5, 1)]
    rows += [w_in[:, o_g:o_g + 3 * NSA_HEADS], zeros(_F_END - _F_GATE - 3 * NSA_HEADS)]
    w_feat = jnp.concatenate(rows, axis=1).T.astype(BF16)
    assert w_feat.shape[0] == _F_END

    dq = MLA_NOPE + MLA_ROPE
    zq = lambda n: jnp.zeros((w_uq.shape[0], n), w_uq.dtype)
    qa, qs = [], []
    for h in range(MLA_HEADS):
        nope = w_uq[:, h * dq:h * dq + MLA_NOPE]
        r1 = w_uq[:, h * dq + MLA_NOPE:h * dq + MLA_NOPE + half]
        r2 = w_uq[:, h * dq + MLA_NOPE + half:(h + 1) * dq]
        qa += [nope, r1, r2, zq(tail)]
        qs += [zq(MLA_NOPE), r2, r1, zq(tail)]
    w_q2t = jnp.concatenate(qa + qs, axis=1).T.astype(BF16)

    dkv = MLA_NOPE + MLA_V
    zk = lambda n: jnp.zeros((w_ukv.shape[0], n), w_ukv.dtype)
    ka, va = [], []
    for h in range(MLA_HEADS):
        ka += [w_ukv[:, h * dkv:h * dkv + MLA_NOPE], zk(LANES - MLA_NOPE)]
        va += [w_ukv[:, h * dkv + MLA_NOPE:(h + 1) * dkv]]
    w_k2 = jnp.concatenate(ka, axis=1).astype(BF16)
    w_v2t = jnp.concatenate(va, axis=1).T.astype(BF16)
    return w_tok, w_feat, w_q2t, w_k2, w_v2t


def _rope_tables(seq):
    half = MLA_ROPE // 2
    freqs = jnp.asarray(ROPE_THETA ** (-np.arange(half, dtype=np.float32) / half), F32)
    ang = jnp.arange(seq).astype(F32)[:, None] * freqs[None, :]
    cos, sin = jnp.cos(ang), jnp.sin(ang)
    pad = jnp.zeros((seq, LANES - MLA_NOPE - MLA_ROPE), F32)
    cq = jnp.concatenate([jnp.ones((seq, MLA_NOPE), F32), cos, cos, pad], axis=1)
    ck = jnp.concatenate([jnp.zeros((seq, MLA_NOPE), F32), cos, cos, pad], axis=1)
    sn = jnp.concatenate([jnp.zeros((seq, MLA_NOPE), F32), -sin, sin, pad], axis=1)
    return ck, sn, cq.T, sn.T


def _inproj_kernel(x_ref, g_ref, wt_ref, wf_ref, qg_ref, wq2_ref, kvg_ref, wk2_ref, wv2_ref,
                   ck_ref, sn_ref, cqt_ref, snt_ref,
                   qt_ref, ksa_ref, kwa_ref, vst_ref, vwt_ref, cmp_ref, gt_ref, qmt_ref, km_ref, vmt_ref,
                   *, tm):
    h = _rms(x_ref[...], g_ref[...]).astype(BF16)
    z = _dot(h, wt_ref[...])
    zt = _dot_nt(wf_ref[...], h)
    qt_ref[...] = (zt[_F_QN:_F_VSEL] * NSA_D ** -0.5).astype(BF16)
    vst_ref[...] = zt[_F_VSEL:_F_VWIN].astype(BF16)
    vwt_ref[...] = zt[_F_VWIN:_F_GATE].astype(BF16)
    gt_ref[...] = jax.nn.sigmoid(zt[_F_GATE:_F_GATE + 3 * NSA_HEADS])
    pos = pl.program_id(1) * tm + lax.broadcasted_iota(jnp.int32, (tm, LANES), 0)
    lane = lax.broadcasted_iota(jnp.int32, (tm, LANES), 1)
    onehot = (lane == lax.shift_right_logical(pos, SEL_SHIFT)).astype(BF16)
    in_blk = (pos & (SEL_LEN - 1)).astype(F32)
    in_tile = (pos & (LANES - 1)).astype(F32)
    for g in range(NSA_GROUPS):
        ksa_ref[g, :, :LANES] = onehot
        ksa_ref[g, :, LANES:] = jnp.where(
            lane == NSA_D, in_blk, z[:, _T_KSEL + g * LANES:_T_KSEL + (g + 1) * LANES]).astype(BF16)
        kwa_ref[g] = jnp.where(
            lane == NSA_D, in_tile, z[:, _T_KWIN + g * LANES:_T_KWIN + (g + 1) * LANES]).astype(BF16)
    for j in range(4):
        cmp_ref[j] = z[:, _T_CMP + j * NSA_D:_T_CMP + (j + 1) * NSA_D]
    cqn = _rms(z[:, _T_CQ:_T_CKV], qg_ref[...]).astype(BF16)
    q2t = _dot_nt(wq2_ref[...], cqn)
    cqt, snt = cqt_ref[...], snt_ref[...]
    nq = MLA_HEADS * LANES
    scale = (MLA_NOPE + MLA_ROPE) ** -0.5
    for hh in range(MLA_HEADS):
        qa = q2t[hh * LANES:(hh + 1) * LANES]
        qs = q2t[nq + hh * LANES:nq + (hh + 1) * LANES]
        qmt_ref[hh * LANES:(hh + 1) * LANES] = ((qa * cqt + qs * snt) * scale).astype(BF16)
    ckvn = _rms(z[:, _T_CKV:_T_KPE], kvg_ref[...]).astype(BF16)
    k2 = _dot(ckvn, wk2_ref[...])
    krot = z[:, _T_KPE:_T_KPE + LANES] * ck_ref[...] + z[:, _T_KPE + LANES:_T_END] * sn_ref[...]
    for hh in range(MLA_HEADS):
        km_ref[hh] = (k2[:, hh * LANES:(hh + 1) * LANES] + krot).astype(BF16)
    vmt_ref[...] = _dot_nt(wv2_ref[...], ckvn).astype(BF16)


def _inproj(x1, mix_pre_g, weights, q_norm_g, kv_norm_g, tables, tm):
    b, s, d = x1.shape
    w_tok, w_feat, w_q2t, w_k2, w_v2t = weights
    ck, sn, cqt, snt = tables
    tok = lambda w: pl.BlockSpec((None, tm, w), lambda bi, i: (bi, i, 0))
    feat = lambda r: pl.BlockSpec((None, r, tm), lambda bi, i: (bi, 0, i))
    heads = lambda n, w: pl.BlockSpec((None, n, tm, w), lambda bi, i: (bi, 0, i, 0))
    tab = pl.BlockSpec((tm, LANES), lambda bi, i: (i, 0))
    tabt = pl.BlockSpec((LANES, tm), lambda bi, i: (0, i))
    sds = jax.ShapeDtypeStruct
    return pl.pallas_call(
        functools.partial(_inproj_kernel, tm=tm),
        grid=(b, s // tm),
        in_specs=[tok(d), _const_spec((1, d)), _const_spec(w_tok.shape), _const_spec(w_feat.shape),
                  _const_spec((1, MLA_Q_RANK)), _const_spec(w_q2t.shape),
                  _const_spec((1, MLA_KV_RANK)), _const_spec(w_k2.shape), _const_spec(w_v2t.shape),
                  tab, tab, tabt, tabt],
        out_specs=[feat(NSA_HEADS * LANES), heads(NSA_GROUPS, 2 * LANES), heads(NSA_GROUPS, LANES),
                   feat(LANES), feat(LANES), heads(4, NSA_D), feat(3 * NSA_HEADS),
                   feat(MLA_HEADS * LANES), heads(MLA_HEADS, LANES), feat(MLA_HEADS * MLA_V)],
        out_shape=[sds((b, NSA_HEADS * LANES, s), BF16), sds((b, NSA_GROUPS, s, 2 * LANES), BF16),
                   sds((b, NSA_GROUPS, s, LANES), BF16), sds((b, LANES, s), BF16), sds((b, LANES, s), BF16),
                   sds((b, 4, s, NSA_D), F32), sds((b, 3 * NSA_HEADS, s), F32),
                   sds((b, MLA_HEADS * LANES, s), BF16), sds((b, MLA_HEADS, s, LANES), BF16),
                   sds((b, MLA_HEADS * MLA_V, s), BF16)],
        compiler_params=_params("parallel", "parallel"),
        name="inproj",
    )(x1, mix_pre_g.reshape(1, d), w_tok, w_feat, q_norm_g.reshape(1, -1), w_q2t,
      kv_norm_g.reshape(1, -1), w_k2, w_v2t, ck, sn, cqt, snt)


def _compress_kernel(c_ref, pos_ref, w1_ref, w2k_ref, w2vt_ref, kc_ref, vct_ref):
    nc = c_ref.shape[1]
    kc = []
    for j in range(4):
        kv, g = divmod(j, NSA_GROUPS)
        c = c_ref[j]
        a0 = _dot((c + pos_ref[kv, 0]).astype(BF16), w1_ref[kv, 0])
        a1 = _dot((c + pos_ref[kv, 1]).astype(BF16), w1_ref[kv, 1])
        pre = a0 + pltpu.roll(a1, nc - 1, 0)
        hid = jax.nn.gelu(pre).astype(BF16)
        if kv == 0:
            kc.append(_dot(hid, w2k_ref[...]))
        else:
            vct_ref[g * NSA_D:(g + 1) * NSA_D] = _dot_nt(w2vt_ref[...], hid).astype(BF16)
    kc_ref[...] = jnp.concatenate(kc, axis=1).astype(BF16)


def _compress(cmp_in, pos_k, w1_k, w2_k, pos_v, w1_v, w2_v):
    b, _, s, dk = cmp_in.shape
    nc = s // CMP_STRIDE
    kdim = CMP_STRIDE * dk
    c4 = cmp_in.reshape(b, 4, nc, kdim)
    pos = jnp.stack([pos_k.reshape(2, 1, kdim), pos_v.reshape(2, 1, kdim)])
    w1 = jnp.stack([w1_k.reshape(2, kdim, CMP_HID), w1_v.reshape(2, kdim, CMP_HID)]).astype(BF16)
    return pl.pallas_call(
        _compress_kernel,
        grid=(b,),
        in_specs=[pl.BlockSpec((None, 4, nc, kdim), lambda bi: (bi, 0, 0, 0)),
                  _const_spec(pos.shape), _const_spec(w1.shape),
                  _const_spec((CMP_HID, dk)), _const_spec((dk, CMP_HID))],
        out_specs=[pl.BlockSpec((None, nc, LANES), lambda bi: (bi, 0, 0)),
                   pl.BlockSpec((None, LANES, nc), lambda bi: (bi, 0, 0))],
        out_shape=[jax.ShapeDtypeStruct((b, nc, LANES), BF16), jax.ShapeDtypeStruct((b, LANES, nc), BF16)],
        compiler_params=_params("parallel"),
        name="compress",
    )(c4, pos, w1, w2_k.astype(BF16), w2_v.T.astype(BF16))


def _nsa_cmp_kernel(qt_ref, kc_ref, vct_ref, ovt_ref, oct_ref, sbt_ref, *, tq):
    nc = kc_ref.shape[0]
    t = pl.program_id(1) * tq + lax.broadcasted_iota(jnp.int32, (1, tq), 1)
    cmp_end = lax.broadcasted_iota(jnp.int32, (nc, 1), 0) * CMP_STRIDE + (CMP_LEN - 1)
    dist = t - cmp_end
    valid = dist >= 0
    validf = valid.astype(F32)
    distf = dist.astype(F32)
    slopes = _alibi_slopes()
    kc = kc_ref[...]
    blk = lax.broadcasted_iota(jnp.int32, (LANES, tq), 0)
    blkf = blk.astype(F32)
    cur = lax.shift_right_logical(t, SEL_SHIFT)
    forced = (blk == 0) | (blk == cur) | (blk == cur - 1)
    for g in range(NSA_GROUPS):
        psum = jnp.zeros((nc, tq), F32)
        for hh in range(NSA_HPG):
            h = g * NSA_HPG + hh
            s = _dot(kc, qt_ref[h * LANES:(h + 1) * LANES]) - slopes[h] * distf
            s = jnp.where(valid, s, NEG)
            e = jnp.exp(s - jnp.max(s, axis=0, keepdims=True)) * validf
            p = e * (1.0 / jnp.maximum(jnp.sum(e, axis=0, keepdims=True), 1e-30))
            psum = psum + p
            oct_ref[h * NSA_D:(h + 1) * NSA_D] = _dot(vct_ref[g * NSA_D:(g + 1) * NSA_D], p.astype(BF16))
        imp = _dot(ovt_ref[...], psum.astype(BF16))
        work = jnp.where(blk <= cur, jnp.where(forced, FORCE_SCORE, imp), NEG)
        chosen = jnp.zeros((LANES, tq), jnp.bool_)
        for _ in range(SEL_TOPK):
            top = jnp.max(work, axis=0, keepdims=True)
            idx = jnp.min(jnp.where(work == top, blkf, float(LANES)), axis=0, keepdims=True)
            hit = blkf == idx
            chosen = chosen | hit
            work = jnp.where(hit, -jnp.inf, work)
        sbt_ref[g] = jnp.where(chosen, 0.0, NEG).astype(BF16)


def _nsa_cmp(qt, kc, vct, overlap_t, tq):
    b, _, s = qt.shape
    nc = kc.shape[1]
    return pl.pallas_call(
        functools.partial(_nsa_cmp_kernel, tq=tq),
        grid=(b, s // tq),
        in_specs=[pl.BlockSpec((None, NSA_HEADS * LANES, tq), lambda bi, i: (bi, 0, i)),
                  pl.BlockSpec((None, nc, LANES), lambda bi, i: (bi, 0, 0)),
                  pl.BlockSpec((None, LANES, nc), lambda bi, i: (bi, 0, 0)),
                  _const_spec(overlap_t.shape)],
        out_specs=[pl.BlockSpec((None, NSA_HEADS * NSA_D, tq), lambda bi, i: (bi, 0, i)),
                   pl.BlockSpec((None, NSA_GROUPS, LANES, tq), lambda bi, i: (bi, 0, 0, i))],
        out_shape=[jax.ShapeDtypeStruct((b, NSA_HEADS * NSA_D, s), F32),
                   jax.ShapeDtypeStruct((b, NSA_GROUPS, LANES, s), BF16)],
        compiler_params=_params("parallel", "parallel"),
        name="nsa_cmp",
    )(qt, kc, vct, overlap_t)


def _overlap_matrix_t(seq):
    n_c = (seq - CMP_LEN) // CMP_STRIDE + 1
    n_sel = seq // SEL_LEN
    c0 = np.arange(n_c) * CMP_STRIDE
    s0 = np.arange(n_sel) * SEL_LEN
    ov = np.clip(np.minimum(c0[:, None] + CMP_LEN, s0[None, :] + SEL_LEN)
                 - np.maximum(c0[:, None], s0[None, :]), 0, None) / CMP_LEN
    full = np.zeros((LANES, seq // CMP_STRIDE), np.float32)
    full[:n_sel, :n_c] = ov.T
    return jnp.asarray(full, BF16)


def _flash_step(s, vt, m, l, acc):
    m_new = jnp.maximum(m, jnp.max(s, axis=0, keepdims=True))
    alpha = jnp.exp(m - m_new)
    p = jnp.exp(s - m_new)
    l_new = alpha * l + jnp.sum(p, axis=0, keepdims=True)
    return m_new, l_new, alpha * acc + _dot(vt, p.astype(BF16))


def _flash_init(dv, nq):
    return (jnp.full((1, nq), NEG, F32), jnp.zeros((1, nq), F32), jnp.zeros((dv, nq), F32))


def _flash_pipelined(n_tiles, scores, values, state, s_buf, p_buf):
    chains = range(len(state))
    nq = state[0][0].shape[1]
    s0 = scores(0)
    for c in chains:
        s_buf[0, c] = s0[c]
        p_buf[1, c] = jnp.zeros(p_buf.shape[2:], BF16)
    colmax = tuple(jnp.max(s0[c], axis=0, keepdims=True) for c in chains)
    ones = tuple(jnp.ones((1, nq), F32) for _ in chains)

    def body(i, carry):
        st, alphas, cmax = carry
        cur = i & 1
        nxt = 1 - cur
        vts = values(jnp.maximum(i - 1, 0))
        s_next = scores(i + 1)
        new_st, new_alpha, new_cmax = [], [], []
        for c in chains:
            m, l, acc = st[c]
            acc = alphas[c] * acc + _dot(vts[c], p_buf[nxt, c])
            m_new = jnp.maximum(m, cmax[c])
            alpha = jnp.exp(m - m_new)
            p = jnp.exp(s_buf[cur, c] - m_new)
            l = alpha * l + jnp.sum(p, axis=0, keepdims=True)
            p_buf[cur, c] = p.astype(BF16)
            s_buf[nxt, c] = s_next[c]
            new_st.append((m_new, l, acc))
            new_alpha.append(alpha)
            new_cmax.append(jnp.max(s_next[c], axis=0, keepdims=True))
        return tuple(new_st), tuple(new_alpha), tuple(new_cmax)

    st, alphas, _ = lax.fori_loop(0, n_tiles, body, (tuple(state), ones, colmax))
    vts = values(jnp.maximum(n_tiles - 1, 0))
    last = (n_tiles + 1) & 1
    return tuple((m, l, alphas[c] * acc + _dot(vts[c], p_buf[last, c]))
                 for c, (m, l, acc) in enumerate(st))


def _nsa_main_kernel(qt_ref, sbt_ref, oct_ref, gt_ref, ksa_ref, kwa_ref, vst_ref, vwt_ref, y_ref,
                     s_buf, p_buf, *, tq, tk_sel, tk_win):
    t0 = pl.program_id(1) * tq
    nq = NSA_HPG * tq
    slopes = _alibi_slopes()
    t_lane = jnp.concatenate([t0 + lax.broadcasted_iota(jnp.int32, (1, tq), 1)] * NSA_HPG, axis=1)
    row = lax.broadcasted_iota(jnp.int32, (NSA_D, nq), 0)
    blk_rel = (lax.broadcasted_iota(jnp.int32, (LANES, tq), 0)
               - lax.shift_right_logical(t0, SEL_SHIFT)).astype(F32)

    q_sel, q_win, slope_l = [], [], []
    for g in range(NSA_GROUPS):
        heads = range(g * NSA_HPG, (g + 1) * NSA_HPG)
        sl = jnp.concatenate([jnp.full((1, tq), slopes[h], F32) for h in heads], axis=1)
        qg = jnp.concatenate(
            [qt_ref[h * LANES + g * NSA_D:h * LANES + (g + 1) * NSA_D] for h in heads], axis=1)
        tail = jnp.where(row == 0, sl, 0.0).astype(BF16)
        sb = sbt_ref[g].astype(F32)
        bias = jnp.concatenate([sb + (slopes[h] * SEL_LEN) * blk_rel for h in heads], axis=1)
        q_sel.append(jnp.concatenate([bias.astype(BF16), qg, tail], axis=0))
        q_win.append(jnp.concatenate([qg, tail], axis=0))
        slope_l.append(sl)

    def sel_scores(kt, g, masked):
        k0 = pl.multiple_of(kt * tk_sel, tk_sel)
        s = _dot(ksa_ref[g, pl.ds(k0, tk_sel), :], q_sel[g])
        if masked:
            key = k0 + lax.broadcasted_iota(jnp.int32, (tk_sel, 1), 0)
            s = jnp.where(key <= t_lane, s, NEG)
        return s, sel_values(kt, g)

    def sel_values(kt, g):
        k0 = pl.multiple_of(kt * tk_sel, tk_sel)
        return vst_ref[g * NSA_D:(g + 1) * NSA_D, pl.ds(k0, tk_sel)]

    def win_scores(kt, g, masked):
        k0 = pl.multiple_of(kt * tk_win, tk_win)
        s = _dot(kwa_ref[g, pl.ds(k0, tk_win), :], q_win[g]) + slope_l[g] * (k0 - t0).astype(F32)
        if masked:
            rel = k0 + lax.broadcasted_iota(jnp.int32, (tk_win, 1), 0) - t_lane
            s = jnp.where((rel <= 0) & (rel > -WINDOW), s, NEG)
        return s, vwt_ref[g * NSA_D:(g + 1) * NSA_D, pl.ds(k0, tk_win)]

    def both_groups(scores, masked):
        def body(kt, carry):
            return tuple(_flash_step(*scores(kt, g, masked), *carry[g]) for g in range(NSA_GROUPS))
        return body

    init = tuple(_flash_init(NSA_D, nq) for _ in range(NSA_GROUPS))
    groups = range(NSA_GROUPS)
    sel = both_groups(sel_scores, True)(t0 // tk_sel, init)
    sel = _flash_pipelined((t0 + 1) // tk_sel,
                           lambda j: [sel_scores(j, g, False)[0] for g in groups],
                           lambda j: [sel_values(j, g) for g in groups], sel, s_buf, p_buf)
    diag = t0 // tk_win
    first = diag - WINDOW // tk_win
    lo = jnp.maximum(first, 0)
    n_lo = jnp.where(first >= 0, 1, 0)
    win = lax.fori_loop(lo, lo + n_lo, both_groups(win_scores, True), init)
    win = lax.fori_loop(lo + n_lo, diag, both_groups(win_scores, False), win)
    win = both_groups(win_scores, True)(diag, win)

    gates = gt_ref[...]
    outs = []
    for h in range(NSA_HEADS):
        g, hh = divmod(h, NSA_HPG)
        cols = slice(hh * tq, (hh + 1) * tq)
        o_s = sel[g][2][:, cols] * (1.0 / sel[g][1][:, cols])
        o_w = win[g][2][:, cols] * (1.0 / win[g][1][:, cols])
        outs.append(gates[3 * h:3 * h + 1] * oct_ref[h * NSA_D:(h + 1) * NSA_D]
                    + gates[3 * h + 1:3 * h + 2] * o_s + gates[3 * h + 2:3 * h + 3] * o_w)
    y_ref[...] = jnp.concatenate(outs, axis=0).T.astype(BF16)


def _nsa_main(qt, sbt, oct, gt, ksa, kwa, vst, vwt, tq, tk_sel, tk_win):
    b, _, s = qt.shape
    feat = lambda r: pl.BlockSpec((None, r, tq), lambda bi, i: (bi, 0, i))
    return pl.pallas_call(
        functools.partial(_nsa_main_kernel, tq=tq, tk_sel=tk_sel, tk_win=tk_win),
        grid=(b, s // tq),
        in_specs=[feat(NSA_HEADS * LANES),
                  pl.BlockSpec((None, NSA_GROUPS, LANES, tq), lambda bi, i: (bi, 0, 0, i)),
                  feat(NSA_HEADS * NSA_D), feat(3 * NSA_HEADS),
                  pl.BlockSpec((None, NSA_GROUPS, s, 2 * LANES), lambda bi, i: (bi, 0, 0, 0)),
                  pl.BlockSpec((None, NSA_GROUPS, s, LANES), lambda bi, i: (bi, 0, 0, 0)),
                  pl.BlockSpec((None, LANES, s), lambda bi, i: (bi, 0, 0)),
                  pl.BlockSpec((None, LANES, s), lambda bi, i: (bi, 0, 0))],
        out_specs=pl.BlockSpec((None, tq, NSA_HEADS * NSA_D), lambda bi, i: (bi, i, 0)),
        out_shape=jax.ShapeDtypeStruct((b, s, NSA_HEADS * NSA_D), BF16),
        scratch_shapes=[pltpu.VMEM((2, NSA_GROUPS, tk_sel, NSA_HPG * tq), F32),
                        pltpu.VMEM((2, NSA_GROUPS, tk_sel, NSA_HPG * tq), BF16)],
        compiler_params=_params("parallel", "arbitrary"),
        name="nsa_main",
    )(qt, sbt, oct, gt, ksa, kwa, vst, vwt)


def _mla_kernel(qt_ref, k_ref, vt_ref, y_ref, s_buf, p_buf, *, tq, tk):
    t0 = pl.program_id(2) * tq
    t_lane = t0 + lax.broadcasted_iota(jnp.int32, (1, tq), 1)
    heads = range(2)
    qts = [qt_ref[hh * LANES:(hh + 1) * LANES] for hh in heads]

    def scores(kt):
        k0 = pl.multiple_of(kt * tk, tk)
        return [_dot(k_ref[hh, pl.ds(k0, tk), :], qts[hh]) for hh in heads]

    def values(kt):
        k0 = pl.multiple_of(kt * tk, tk)
        return [vt_ref[hh * MLA_V:(hh + 1) * MLA_V, pl.ds(k0, tk)] for hh in heads]

    def diagonal(kt, carry):
        key = kt * tk + lax.broadcasted_iota(jnp.int32, (tk, 1), 0)
        s, vts = scores(kt), values(kt)
        return tuple(_flash_step(jnp.where(key <= t_lane, s[hh], NEG), vts[hh], *carry[hh]) for hh in heads)

    n_full = (t0 + 1) // tk
    n_tiles = (t0 + tq + tk - 1) // tk
    st = lax.fori_loop(n_full, n_tiles, diagonal, tuple(_flash_init(MLA_V, tq) for _ in heads))
    st = _flash_pipelined(n_full, scores, values, st, s_buf, p_buf)
    o = jnp.concatenate([st[hh][2] * (1.0 / st[hh][1]) for hh in heads], axis=0)
    y_ref[...] = o.T.astype(BF16)


def _mla(qmt, km, vmt, tq, tk):
    b, h, s, _ = km.shape
    k5 = km.reshape(b, h // 2, 2, s, LANES)
    return pl.pallas_call(
        functools.partial(_mla_kernel, tq=tq, tk=tk),
        grid=(b, h // 2, s // tq),
        in_specs=[pl.BlockSpec((None, 2 * LANES, tq), lambda bi, hp, i: (bi, hp, i)),
                  pl.BlockSpec((None, None, 2, s, LANES), lambda bi, hp, i: (bi, hp, 0, 0, 0)),
                  pl.BlockSpec((None, 2 * MLA_V, s), lambda bi, hp, i: (bi, hp, 0))],
        out_specs=pl.BlockSpec((None, tq, 2 * MLA_V), lambda bi, hp, i: (bi, i, hp)),
        out_shape=jax.ShapeDtypeStruct((b, s, h * MLA_V), BF16),
        scratch_shapes=[pltpu.VMEM((2, 2, tk, tq), F32), pltpu.VMEM((2, 2, tk, tq), BF16)],
        compiler_params=_params("parallel", "parallel", "arbitrary"),
        name="mla",
    )(qmt, k5, vmt)


def _merge_kernel(x_ref, yn_ref, ym_ref, pre_ref, post_ref, wgm_ref, wpn_ref, wpm_ref, wo_ref, o_ref):
    x = x_ref[...]
    d = x.shape[1]
    h = _rms(x, pre_ref[...]).astype(BF16)
    gm = jax.nn.sigmoid(_dot(h, wgm_ref[...]))
    merged = gm[:, :d] * _dot(yn_ref[...], wpn_ref[...]) + gm[:, d:] * _dot(ym_ref[...], wpm_ref[...])
    y = _dot(merged.astype(BF16), wo_ref[...])
    o_ref[...] = x + _rms(y, post_ref[...])


def _merge(x1, y_nsa, y_mla, pre_g, post_g, w_gm, w_pn, w_pm, w_out, tm):
    n, d = x1.shape
    tok = lambda w: pl.BlockSpec((tm, w), lambda i: (i, 0))
    return pl.pallas_call(
        _merge_kernel,
        grid=(n // tm,),
        in_specs=[tok(d), tok(y_nsa.shape[1]), tok(y_mla.shape[1]), _const_spec((1, d)), _const_spec((1, d)),
                  _const_spec(w_gm.shape), _const_spec(w_pn.shape), _const_spec(w_pm.shape),
                  _const_spec(w_out.shape)],
        out_specs=tok(d),
        out_shape=jax.ShapeDtypeStruct((n, d), F32),
        compiler_params=_params("parallel"),
        name="merge",
    )(x1, y_nsa, y_mla, pre_g.reshape(1, d), post_g.reshape(1, d),
      w_gm.astype(BF16), w_pn.astype(BF16), w_pm.astype(BF16), w_out.astype(BF16))


def kernel(x, ff1_pre_g, ff1_post_g, ff1_w_gate, ff1_w_up, ff1_w_down, mix_pre_g, mix_post_g, w_in, cmp_pos_k, cmp_w1_k, cmp_w2_k, cmp_pos_v, cmp_w1_v, cmp_w2_v, mla_q_norm_g, mla_w_uq, mla_kv_norm_g, mla_w_ukv, w_proj_nsa, w_proj_mla, w_out, ff2_pre_g, ff2_post_g, ff2_w_gate, ff2_w_up, ff2_w_down):
    b, s, d = x.shape
    assert s % (SEL_LEN * SEL_TOPK) == 0 and s // SEL_LEN <= LANES
    tl = _tiles(s)
    n = b * s

    x1 = _ffn(x.reshape(n, d), ff1_pre_g, ff1_post_g, ff1_w_gate, ff1_w_up, ff1_w_down, tl["tm"])

    (qt, ksa, kwa, vst, vwt, cmp_in, gt, qmt, km, vmt) = _inproj(
        x1.reshape(b, s, d), mix_pre_g, _inproj_weights(w_in, mla_w_uq, mla_w_ukv),
        mla_q_norm_g, mla_kv_norm_g, _rope_tables(s), tl["tm"])

    kc, vct = _compress(cmp_in, cmp_pos_k, cmp_w1_k, cmp_w2_k, cmp_pos_v, cmp_w1_v, cmp_w2_v)
    oct, sbt = _nsa_cmp(qt, kc, vct, _overlap_matrix_t(s), tl["tq_cmp"])
    y_nsa = _nsa_main(qt, sbt, oct, gt, ksa, kwa, vst, vwt, tl["tq_nsa"], tl["tk_sel"], tl["tk_win"])
    y_mla = _mla(qmt, km, vmt, tl["tq_mla"], tl["tk_mla"])

    o_gm = sum((NSA_HEADS * NSA_D, 6 * NSA_GROUPS * NSA_D, 3 * NSA_HEADS, MLA_Q_RANK, MLA_KV_RANK, MLA_ROPE))
    x2 = _merge(x1, y_nsa.reshape(n, -1), y_mla.reshape(n, -1), mix_pre_g, mix_post_g,
                w_in[:, o_gm:], w_proj_nsa, w_proj_mla, w_out, tl["tm"])
    x3 = _ffn(x2, ff2_pre_g, ff2_post_g, ff2_w_gate, ff2_w_up, ff2_w_down, tl["tm"])
    return x3.reshape(b, s, d)
```

```python
import functools

import numpy as np
import jax
import jax.numpy as jnp
from jax import lax
from jax.experimental import pallas as pl
from jax.experimental.pallas import tpu as pltpu

F32 = jnp.float32
BF16 = jnp.bfloat16

EPS = 1e-6
NEG = -1e30
FORCE_SCORE = 1e4
NSA_HEADS = 8
NSA_GROUPS = 2
NSA_HPG = NSA_HEADS // NSA_GROUPS
NSA_D = 64
CMP_LEN = 32
CMP_STRIDE = 16
CMP_HID = 256
SEL_LEN = 64
SEL_SHIFT = 6
SEL_TOPK = 16
WINDOW = 512
MLA_HEADS = 8
MLA_NOPE = 64
MLA_ROPE = 32
MLA_V = 64
MLA_Q_RANK = 256
MLA_KV_RANK = 128
ROPE_THETA = 10000.0
LANES = 128
VMEM_LIMIT = 56 * 1024 * 1024


def _tiles(seq):
    return dict(
        tm=min(512, seq),
        tq_cmp=min(256, seq),
        tq_nsa=min(128, seq),
        tk_sel=min(256, seq),
        tk_win=LANES,
        tq_mla=min(512, seq),
    )


def _params(*sem):
    return pltpu.CompilerParams(dimension_semantics=sem, vmem_limit_bytes=VMEM_LIMIT)


def _const_spec(shape):
    nd = len(shape)
    return pl.BlockSpec(shape, lambda *_: (0,) * nd)


def _rms(x, g):
    return x * lax.rsqrt(jnp.mean(x * x, axis=-1, keepdims=True) + EPS) * g


def _dot(a, b):
    return jnp.dot(a, b, preferred_element_type=F32)


def _dot_nt(a, b):
    return lax.dot_general(a, b, (((1,), (1,)), ((), ())), preferred_element_type=F32)


def _alibi_slopes():
    return [float(2.0 ** (-8.0 * (i + 1) / NSA_HEADS)) for i in range(NSA_HEADS)]


def _ffn_kernel(x_ref, pre_ref, post_ref, wg_ref, wu_ref, wd_ref, o_ref, *, fc):
    x = x_ref[...]
    h = _rms(x, pre_ref[...]).astype(BF16)
    acc = jnp.zeros(x.shape, F32)
    for c in range(wg_ref.shape[1] // fc):
        gate = _dot(h, wg_ref[:, c * fc:(c + 1) * fc])
        up = _dot(h, wu_ref[:, c * fc:(c + 1) * fc])
        act = (gate * jax.nn.sigmoid(gate) * up).astype(BF16)
        acc = acc + _dot(act, wd_ref[c * fc:(c + 1) * fc, :])
    o_ref[...] = x + 0.5 * _rms(acc, post_ref[...])


def _ffn(x2d, pre_g, post_g, w_gate, w_up, w_down, tm):
    n, d = x2d.shape
    f = w_gate.shape[1]
    fc = f // 2 if (f // 2) % LANES == 0 else f
    return pl.pallas_call(
        functools.partial(_ffn_kernel, fc=fc),
        grid=(n // tm,),
        in_specs=[pl.BlockSpec((tm, d), lambda i: (i, 0)),
                  _const_spec((1, d)), _const_spec((1, d)),
                  _const_spec((d, f)), _const_spec((d, f)), _const_spec((f, d))],
        out_specs=pl.BlockSpec((tm, d), lambda i: (i, 0)),
        out_shape=jax.ShapeDtypeStruct((n, d), F32),
        compiler_params=_params("parallel"),
        name="ffn",
    )(x2d, pre_g.reshape(1, d), post_g.reshape(1, d),
      w_gate.astype(BF16), w_up.astype(BF16), w_down.astype(BF16))


_T_KSEL = 0
_T_KWIN = _T_KSEL + 2 * LANES
_T_CMP = _T_KWIN + 2 * LANES
_T_CQ = _T_CMP + 2 * LANES
_T_CKV = _T_CQ + MLA_Q_RANK
_T_KPE = _T_CKV + MLA_KV_RANK
_T_END = _T_KPE + 2 * LANES
_F_QN = 0
_F_VSEL = _F_QN + NSA_HEADS * LANES
_F_VWIN = _F_VSEL + LANES
_F_GATE = _F_VWIN + LANES
_F_END = _F_GATE + 32


def _inproj_weights(w_in, w_uq, w_ukv):
    d = w_in.shape[0]
    o_q = 0
    o_kv = o_q + NSA_HEADS * NSA_D
    o_g = o_kv + 6 * NSA_GROUPS * NSA_D
    o_cq = o_g + 3 * NSA_HEADS
    o_ckv = o_cq + MLA_Q_RANK
    o_kpe = o_ckv + MLA_KV_RANK
    zeros = lambda n: jnp.zeros((d, n), w_in.dtype)
    kv = lambda j, g: w_in[:, o_kv + (j * NSA_GROUPS + g) * NSA_D:o_kv + (j * NSA_GROUPS + g + 1) * NSA_D]
    half = MLA_ROPE // 2
    kp1 = w_in[:, o_kpe:o_kpe + half]
    kp2 = w_in[:, o_kpe + half:o_kpe + MLA_ROPE]
    tail = LANES - MLA_NOPE - MLA_ROPE
    cols = [kv(2, 0), zeros(NSA_D), kv(2, 1), zeros(NSA_D),
            kv(4, 0), zeros(NSA_D), kv(4, 1), zeros(NSA_D),
            kv(0, 0), kv(0, 1), kv(1, 0), kv(1, 1),
            w_in[:, o_cq:o_cq + MLA_Q_RANK], w_in[:, o_ckv:o_ckv + MLA_KV_RANK],
            zeros(MLA_NOPE), kp1, kp2, zeros(tail), zeros(MLA_NOPE), kp2, kp1, zeros(tail)]
    w_tok = jnp.concatenate(cols, axis=1).astype(BF16)
    assert w_tok.shape[1] == _T_END
    rows = []
    for h in range(NSA_HEADS):
        wq = w_in[:, o_q + h * NSA_D:o_q + (h + 1) * NSA_D]
        rows += [wq, zeros(NSA_D)] if h // NSA_HPG == 0 else [zeros(NSA_D), wq]
    rows += [kv(3, 0), kv(3, 1), kv(5, 0), kv(5, 1)]
    rows += [w_in[:, o_g:o_g + 3 * NSA_HEADS], zeros(_F_END - _F_GATE - 3 * NSA_HEADS)]
    w_feat = jnp.concatenate(rows, axis=1).T.astype(BF16)
    assert w_feat.shape[0] == _F_END

    dq = MLA_NOPE + MLA_ROPE
    zq = lambda n: jnp.zeros((w_uq.shape[0], n), w_uq.dtype)
    qa, qs = [], []
    for h in range(MLA_HEADS):
        nope = w_uq[:, h * dq:h * dq + MLA_NOPE]
        r1 = w_uq[:, h * dq + MLA_NOPE:h * dq + MLA_NOPE + half]
        r2 = w_uq[:, h * dq + MLA_NOPE + half:(h + 1) * dq]
        qa += [nope, r1, r2, zq(tail)]
        qs += [zq(MLA_NOPE), r2, r1, zq(tail)]
    w_q2t = jnp.concatenate(qa + qs, axis=1).T.astype(BF16)

    dkv = MLA_NOPE + MLA_V
    zk = lambda n: jnp.zeros((w_ukv.shape[0], n), w_ukv.dtype)
    ka, va = [], []
    for h in range(MLA_HEADS):
        ka += [w_ukv[:, h * dkv:h * dkv + MLA_NOPE], zk(LANES - MLA_NOPE)]
        va += [w_ukv[:, h * dkv + MLA_NOPE:(h + 1) * dkv]]
    w_k2 = jnp.concatenate(ka, axis=1).astype(BF16)
    w_v2t = jnp.concatenate(va, axis=1).T.astype(BF16)
    return w_tok, w_feat, w_q2t, w_k2, w_v2t


def _rope_tables(seq):
    half = MLA_ROPE // 2
    freqs = jnp.asarray(ROPE_THETA ** (-np.arange(half, dtype=np.float32) / half), F32)
    ang = jnp.arange(seq).astype(F32)[:, None] * freqs[None, :]
    cos, sin = jnp.cos(ang), jnp.sin(ang)
    pad = jnp.zeros((seq, LANES - MLA_NOPE - MLA_ROPE), F32)
    cq = jnp.concatenate([jnp.ones((seq, MLA_NOPE), F32), cos, cos, pad], axis=1)
    ck = jnp.concatenate([jnp.zeros((seq, MLA_NOPE), F32), cos, cos, pad], axis=1)
    sn = jnp.concatenate([jnp.zeros((seq, MLA_NOPE), F32), -sin, sin, pad], axis=1)
    return ck, sn, cq.T, sn.T


def _inproj_kernel(x_ref, g_ref, wt_ref, wf_ref, qg_ref, wq2_ref, kvg_ref, wk2_ref, wv2_ref,
                   ck_ref, sn_ref, cqt_ref, snt_ref,
                   qt_ref, ksa_ref, kwa_ref, vst_ref, vwt_ref, cmp_ref, gt_ref, qmt_ref, km_ref, vmt_ref,
                   *, tm):
    h = _rms(x_ref[...], g_ref[...]).astype(BF16)
    z = _dot(h, wt_ref[...])
    zt = _dot_nt(wf_ref[...], h)
    qt_ref[...] = (zt[_F_QN:_F_VSEL] * NSA_D ** -0.5).astype(BF16)
    vst_ref[...] = zt[_F_VSEL:_F_VWIN].astype(BF16)
    vwt_ref[...] = zt[_F_VWIN:_F_GATE].astype(BF16)
    gt_ref[...] = jax.nn.sigmoid(zt[_F_GATE:_F_GATE + 3 * NSA_HEADS])
    pos = pl.program_id(1) * tm + lax.broadcasted_iota(jnp.int32, (tm, LANES), 0)
    lane = lax.broadcasted_iota(jnp.int32, (tm, LANES), 1)
    onehot = (lane == lax.shift_right_logical(pos, SEL_SHIFT)).astype(BF16)
    in_blk = (pos & (SEL_LEN - 1)).astype(F32)
    in_tile = (pos & (LANES - 1)).astype(F32)
    for g in range(NSA_GROUPS):
        ksa_ref[g, :, :LANES] = onehot
        ksa_ref[g, :, LANES:] = jnp.where(
            lane == NSA_D, in_blk, z[:, _T_KSEL + g * LANES:_T_KSEL + (g + 1) * LANES]).astype(BF16)
        kwa_ref[g] = jnp.where(
            lane == NSA_D, in_tile, z[:, _T_KWIN + g * LANES:_T_KWIN + (g + 1) * LANES]).astype(BF16)
    for j in range(4):
        cmp_ref[j] = z[:, _T_CMP + j * NSA_D:_T_CMP + (j + 1) * NSA_D]
    cqn = _rms(z[:, _T_CQ:_T_CKV], qg_ref[...]).astype(BF16)
    q2t = _dot_nt(wq2_ref[...], cqn)
    cqt, snt = cqt_ref[...], snt_ref[...]
    nq = MLA_HEADS * LANES
    scale = (MLA_NOPE + MLA_ROPE) ** -0.5
    for hh in range(MLA_HEADS):
        qa = q2t[hh * LANES:(hh + 1) * LANES]
        qs = q2t[nq + hh * LANES:nq + (hh + 1) * LANES]
        qmt_ref[hh * LANES:(hh + 1) * LANES] = ((qa * cqt + qs * snt) * scale).astype(BF16)
    ckvn = _rms(z[:, _T_CKV:_T_KPE], kvg_ref[...]).astype(BF16)
    k2 = _dot(ckvn, wk2_ref[...])
    krot = z[:, _T_KPE:_T_KPE + LANES] * ck_ref[...] + z[:, _T_KPE + LANES:_T_END] * sn_ref[...]
    for hh in range(MLA_HEADS):
        km_ref[hh] = (k2[:, hh * LANES:(hh + 1) * LANES] + krot).astype(BF16)
    vmt_ref[...] = _dot_nt(wv2_ref[...], ckvn).astype(BF16)


def _inproj(x1, mix_pre_g, weights, q_norm_g, kv_norm_g, tables, tm):
    b, s, d = x1.shape
    w_tok, w_feat, w_q2t, w_k2, w_v2t = weights
    ck, sn, cqt, snt = tables
    tok = lambda w: pl.BlockSpec((None, tm, w), lambda bi, i: (bi, i, 0))
    feat = lambda r: pl.BlockSpec((None, r, tm), lambda bi, i: (bi, 0, i))
    heads = lambda n, w: pl.BlockSpec((None, n, tm, w), lambda bi, i: (bi, 0, i, 0))
    tab = pl.BlockSpec((tm, LANES), lambda bi, i: (i, 0))
    tabt = pl.BlockSpec((LANES, tm), lambda bi, i: (0, i))
    sds = jax.ShapeDtypeStruct
    return pl.pallas_call(
        functools.partial(_inproj_kernel, tm=tm),
        grid=(b, s // tm),
        in_specs=[tok(d), _const_spec((1, d)), _const_spec(w_tok.shape), _const_spec(w_feat.shape),
                  _const_spec((1, MLA_Q_RANK)), _const_spec(w_q2t.shape),
                  _const_spec((1, MLA_KV_RANK)), _const_spec(w_k2.shape), _const_spec(w_v2t.shape),
                  tab, tab, tabt, tabt],
        out_specs=[feat(NSA_HEADS * LANES), heads(NSA_GROUPS, 2 * LANES), heads(NSA_GROUPS, LANES),
                   feat(LANES), feat(LANES), heads(4, NSA_D), feat(3 * NSA_HEADS),
                   feat(MLA_HEADS * LANES), heads(MLA_HEADS, LANES), feat(MLA_HEADS * MLA_V)],
        out_shape=[sds((b, NSA_HEADS * LANES, s), BF16), sds((b, NSA_GROUPS, s, 2 * LANES), BF16),
                   sds((b, NSA_GROUPS, s, LANES), BF16), sds((b, LANES, s), BF16), sds((b, LANES, s), BF16),
                   sds((b, 4, s, NSA_D), F32), sds((b, 3 * NSA_HEADS, s), F32),
                   sds((b, MLA_HEADS * LANES, s), BF16), sds((b, MLA_HEADS, s, LANES), BF16),
                   sds((b, MLA_HEADS * MLA_V, s), BF16)],
        compiler_params=_params("parallel", "parallel"),
        name="inproj",
    )(x1, mix_pre_g.reshape(1, d), w_tok, w_feat, q_norm_g.reshape(1, -1), w_q2t,
      kv_norm_g.reshape(1, -1), w_k2, w_v2t, ck, sn, cqt, snt)


def _compress_kernel(c_ref, pos_ref, w1_ref, w2k_ref, w2vt_ref, kc_ref, vct_ref):
    nc = c_ref.shape[1]
    kc = []
    for j in range(4):
        kv, g = divmod(j, NSA_GROUPS)
        c = c_ref[j]
        a0 = _dot((c + pos_ref[kv, 0]).astype(BF16), w1_ref[kv, 0])
        a1 = _dot((c + pos_ref[kv, 1]).astype(BF16), w1_ref[kv, 1])
        pre = a0 + pltpu.roll(a1, nc - 1, 0)
        hid = jax.nn.gelu(pre).astype(BF16)
        if kv == 0:
            kc.append(_dot(hid, w2k_ref[...]))
        else:
            vct_ref[g * NSA_D:(g + 1) * NSA_D] = _dot_nt(w2vt_ref[...], hid).astype(BF16)
    kc_ref[...] = jnp.concatenate(kc, axis=1).astype(BF16)


def _compress(cmp_in, pos_k, w1_k, w2_k, pos_v, w1_v, w2_v):
    b, _, s, dk = cmp_in.shape
    nc = s // CMP_STRIDE
    kdim = CMP_STRIDE * dk
    c4 = cmp_in.reshape(b, 4, nc, kdim)
    pos = jnp.stack([pos_k.reshape(2, 1, kdim), pos_v.reshape(2, 1, kdim)])
    w1 = jnp.stack([w1_k.reshape(2, kdim, CMP_HID), w1_v.reshape(2, kdim, CMP_HID)]).astype(BF16)
    return pl.pallas_call(
        _compress_kernel,
        grid=(b,),
        in_specs=[pl.BlockSpec((None, 4, nc, kdim), lambda bi: (bi, 0, 0, 0)),
                  _const_spec(pos.shape), _const_spec(w1.shape),
                  _const_spec((CMP_HID, dk)), _const_spec((dk, CMP_HID))],
        out_specs=[pl.BlockSpec((None, nc, LANES), lambda bi: (bi, 0, 0)),
                   pl.BlockSpec((None, LANES, nc), lambda bi: (bi, 0, 0))],
        out_shape=[jax.ShapeDtypeStruct((b, nc, LANES), BF16), jax.ShapeDtypeStruct((b, LANES, nc), BF16)],
        compiler_params=_params("parallel"),
        name="compress",
    )(c4, pos, w1, w2_k.astype(BF16), w2_v.T.astype(BF16))


def _nsa_cmp_kernel(qt_ref, kc_ref, vct_ref, ovt_ref, pool_ref, oct_ref, sbt_ref, act_ref, *, tq, tq_main):
    nc = kc_ref.shape[0]
    t = pl.program_id(1) * tq + lax.broadcasted_iota(jnp.int32, (1, tq), 1)
    cmp_end = lax.broadcasted_iota(jnp.int32, (nc, 1), 0) * CMP_STRIDE + (CMP_LEN - 1)
    dist = t - cmp_end
    valid = dist >= 0
    validf = valid.astype(F32)
    distf = dist.astype(F32)
    slopes = _alibi_slopes()
    kc = kc_ref[...]
    blk = lax.broadcasted_iota(jnp.int32, (LANES, tq), 0)
    blkf = blk.astype(F32)
    cur = lax.shift_right_logical(t, SEL_SHIFT)
    forced = (blk == 0) | (blk == cur) | (blk == cur - 1)
    for g in range(NSA_GROUPS):
        psum = jnp.zeros((nc, tq), F32)
        for hh in range(NSA_HPG):
            h = g * NSA_HPG + hh
            s = _dot(kc, qt_ref[h * LANES:(h + 1) * LANES]) - slopes[h] * distf
            s = jnp.where(valid, s, NEG)
            e = jnp.exp(s - jnp.max(s, axis=0, keepdims=True)) * validf
            p = e * (1.0 / jnp.maximum(jnp.sum(e, axis=0, keepdims=True), 1e-30))
            psum = psum + p
            oct_ref[h * NSA_D:(h + 1) * NSA_D] = _dot(vct_ref[g * NSA_D:(g + 1) * NSA_D], p.astype(BF16))
        imp = _dot(ovt_ref[...], psum.astype(BF16))
        work = jnp.where(blk <= cur, jnp.where(forced, FORCE_SCORE, imp), NEG)
        chosen = jnp.zeros((LANES, tq), jnp.bool_)
        for _ in range(SEL_TOPK):
            top = jnp.max(work, axis=0, keepdims=True)
            idx = jnp.min(jnp.where(work == top, blkf, float(LANES)), axis=0, keepdims=True)
            hit = blkf == idx
            chosen = chosen | hit
            work = jnp.where(hit, -jnp.inf, work)
        sbt_ref[g] = jnp.where(chosen, 0.0, NEG).astype(BF16)
        used = _dot(pool_ref[...], chosen.astype(BF16))
        for j in range(tq // tq_main):
            any_q = jnp.max(used[:, j * tq_main:(j + 1) * tq_main], axis=1, keepdims=True)
            act_ref[g, j] = jnp.broadcast_to(any_q, act_ref.shape[2:])


def _nsa_cmp(qt, kc, vct, overlap_t, pool, tq, tq_main):
    b, _, s = qt.shape
    nc = kc.shape[1]
    nkt = pool.shape[0]
    return pl.pallas_call(
        functools.partial(_nsa_cmp_kernel, tq=tq, tq_main=tq_main),
        grid=(b, s // tq),
        in_specs=[pl.BlockSpec((None, NSA_HEADS * LANES, tq), lambda bi, i: (bi, 0, i)),
                  pl.BlockSpec((None, nc, LANES), lambda bi, i: (bi, 0, 0)),
                  pl.BlockSpec((None, LANES, nc), lambda bi, i: (bi, 0, 0)),
                  _const_spec(overlap_t.shape), _const_spec(pool.shape)],
        out_specs=[pl.BlockSpec((None, NSA_HEADS * NSA_D, tq), lambda bi, i: (bi, 0, i)),
                   pl.BlockSpec((None, NSA_GROUPS, LANES, tq), lambda bi, i: (bi, 0, 0, i)),
                   pl.BlockSpec((None, NSA_GROUPS, tq // tq_main, nkt, LANES), lambda bi, i: (bi, 0, i, 0, 0))],
        out_shape=[jax.ShapeDtypeStruct((b, NSA_HEADS * NSA_D, s), F32),
                   jax.ShapeDtypeStruct((b, NSA_GROUPS, LANES, s), BF16),
                   jax.ShapeDtypeStruct((b, NSA_GROUPS, s // tq_main, nkt, LANES), F32)],
        compiler_params=_params("parallel", "parallel"),
        name="nsa_cmp",
    )(qt, kc, vct, overlap_t, pool)


def _tile_pool_matrix(seq, tk_sel):
    per_tile = tk_sel // SEL_LEN
    nkt = -(-(seq // tk_sel) // 16) * 16
    pool = np.zeros((nkt, LANES), np.float32)
    for c in range(seq // SEL_LEN):
        pool[c // per_tile, c] = 1.0
    return jnp.asarray(pool, BF16)


def _overlap_matrix_t(seq):
    n_c = (seq - CMP_LEN) // CMP_STRIDE + 1
    n_sel = seq // SEL_LEN
    c0 = np.arange(n_c) * CMP_STRIDE
    s0 = np.arange(n_sel) * SEL_LEN
    ov = np.clip(np.minimum(c0[:, None] + CMP_LEN, s0[None, :] + SEL_LEN)
                 - np.maximum(c0[:, None], s0[None, :]), 0, None) / CMP_LEN
    full = np.zeros((LANES, seq // CMP_STRIDE), np.float32)
    full[:n_sel, :n_c] = ov.T
    return jnp.asarray(full, BF16)


def _flash_step(s, vt, m, l, acc):
    m_new = jnp.maximum(m, jnp.max(s, axis=0, keepdims=True))
    alpha = jnp.exp(m - m_new)
    p = jnp.exp(s - m_new)
    l_new = alpha * l + jnp.sum(p, axis=0, keepdims=True)
    return m_new, l_new, alpha * acc + _dot(vt, p.astype(BF16))


def _flash_init(dv, nq):
    return (jnp.full((1, nq), NEG, F32), jnp.zeros((1, nq), F32), jnp.zeros((dv, nq), F32))


def _flash_pipelined(n_tiles, scores, values, state, s_buf, p_buf):
    chains = range(len(state))
    nq = state[0][0].shape[1]
    s0 = scores(0)
    for c in chains:
        s_buf[0, c] = s0[c]
        p_buf[1, c] = jnp.zeros(p_buf.shape[2:], BF16)
    colmax = tuple(jnp.max(s0[c], axis=0, keepdims=True) for c in chains)
    ones = tuple(jnp.ones((1, nq), F32) for _ in chains)

    def body(i, carry):
        st, alphas, cmax = carry
        cur = i & 1
        nxt = 1 - cur
        vts = values(jnp.maximum(i - 1, 0))
        s_next = scores(i + 1)
        new_st, new_alpha, new_cmax = [], [], []
        for c in chains:
            m, l, acc = st[c]
            acc = alphas[c] * acc + _dot(vts[c], p_buf[nxt, c])
            m_new = jnp.maximum(m, cmax[c])
            alpha = jnp.exp(m - m_new)
            p = jnp.exp(s_buf[cur, c] - m_new)
            l = alpha * l + jnp.sum(p, axis=0, keepdims=True)
            p_buf[cur, c] = p.astype(BF16)
            s_buf[nxt, c] = s_next[c]
            new_st.append((m_new, l, acc))
            new_alpha.append(alpha)
            new_cmax.append(jnp.max(s_next[c], axis=0, keepdims=True))
        return tuple(new_st), tuple(new_alpha), tuple(new_cmax)

    st, alphas, _ = lax.fori_loop(0, n_tiles, body, (tuple(state), ones, colmax))
    vts = values(jnp.maximum(n_tiles - 1, 0))
    last = (n_tiles + 1) & 1
    return tuple((m, l, alphas[c] * acc + _dot(vts[c], p_buf[last, c]))
                 for c, (m, l, acc) in enumerate(st))


def _nsa_main_kernel(act_ref, qt_ref, sbt_ref, oct_ref, gt_ref, ksa_ref, kwa_ref, vst_ref, vwt_ref, y_ref,
                     s_buf, p_buf, tiles_ref, *, tq, tk_sel, tk_win, act_tiles):
    t0 = pl.program_id(1) * tq
    nq = NSA_HPG * tq
    slopes = _alibi_slopes()
    t_lane = jnp.concatenate([t0 + lax.broadcasted_iota(jnp.int32, (1, tq), 1)] * NSA_HPG, axis=1)
    row = lax.broadcasted_iota(jnp.int32, (NSA_D, nq), 0)
    blk_rel = (lax.broadcasted_iota(jnp.int32, (LANES, tq), 0)
               - lax.shift_right_logical(t0, SEL_SHIFT)).astype(F32)

    q_sel, q_win, slope_l = [], [], []
    for g in range(NSA_GROUPS):
        heads = range(g * NSA_HPG, (g + 1) * NSA_HPG)
        sl = jnp.concatenate([jnp.full((1, tq), slopes[h], F32) for h in heads], axis=1)
        qg = jnp.concatenate(
            [qt_ref[h * LANES + g * NSA_D:h * LANES + (g + 1) * NSA_D] for h in heads], axis=1)
        tail = jnp.where(row == 0, sl, 0.0).astype(BF16)
        sb = sbt_ref[g].astype(F32)
        bias = jnp.concatenate([sb + (slopes[h] * SEL_LEN) * blk_rel for h in heads], axis=1)
        q_sel.append(jnp.concatenate([bias.astype(BF16), qg, tail], axis=0))
        q_win.append(jnp.concatenate([qg, tail], axis=0))
        slope_l.append(sl)

    groups = range(NSA_GROUPS)

    def sel_scores(kt, g):
        k0 = pl.multiple_of(kt * tk_sel, tk_sel)
        return _dot(ksa_ref[g, pl.ds(k0, tk_sel), :], q_sel[g])

    def sel_values(kt, g):
        k0 = pl.multiple_of(kt * tk_sel, tk_sel)
        return vst_ref[g * NSA_D:(g + 1) * NSA_D, pl.ds(k0, tk_sel)]

    n_full = (t0 + 1) // tk_sel
    n_kt = tiles_ref.shape[1]
    counts, spare = [], []
    for g in groups:
        base = ((pl.program_id(0) * NSA_GROUPS + g) * pl.num_programs(1) + pl.program_id(1)) * act_tiles
        for k in range(n_kt):
            tiles_ref[g, k] = 0

        def scan(kt, carry, g=g, base=base):
            cnt, unused = carry
            used = act_ref[base + kt]
            tiles_ref[g, cnt] = kt
            return cnt + used, jnp.where(used == 0, kt, unused)

        cnt, unused = lax.fori_loop(0, n_full, scan, (jnp.int32(0), jnp.int32(0)))
        counts.append(cnt)
        spare.append(unused)

    def tile_of(k, g):
        return jnp.where(k < counts[g], tiles_ref[g, jnp.minimum(k, n_kt - 1)], spare[g])

    diag0 = pl.multiple_of((t0 // tk_sel) * tk_sel, tk_sel)
    causal = diag0 + lax.broadcasted_iota(jnp.int32, (tk_sel, 1), 0) <= t_lane
    sel = tuple(_flash_step(jnp.where(causal, sel_scores(t0 // tk_sel, g), NEG), sel_values(t0 // tk_sel, g),
                            *_flash_init(NSA_D, nq)) for g in groups)
    sel = _flash_pipelined(jnp.maximum(counts[0], counts[1]),
                           lambda k: [sel_scores(tile_of(k, g), g) for g in groups],
                           lambda k: [sel_values(tile_of(k, g), g) for g in groups], sel, s_buf, p_buf)

    start = pl.multiple_of(jnp.maximum(t0 - WINDOW, 0), tk_win)
    n_win = WINDOW + tq
    r = lax.broadcasted_iota(jnp.int32, (n_win, 1), 0)
    rel = start + r - t_lane
    in_window = (rel <= 0) & (rel > -WINDOW)
    chunk_off = (start - t0 + (r & ~(tk_win - 1))).astype(F32)
    win = []
    for g in groups:
        s = _dot(kwa_ref[g, pl.ds(start, n_win), :], q_win[g]) + slope_l[g] * chunk_off
        s = jnp.where(in_window, s, NEG)
        p = jnp.exp(s - jnp.max(s, axis=0, keepdims=True))
        win.append((None, jnp.sum(p, axis=0, keepdims=True),
                    _dot(vwt_ref[g * NSA_D:(g + 1) * NSA_D, pl.ds(start, n_win)], p.astype(BF16))))

    gates = gt_ref[...]
    outs = []
    for h in range(NSA_HEADS):
        g, hh = divmod(h, NSA_HPG)
        cols = slice(hh * tq, (hh + 1) * tq)
        o_s = sel[g][2][:, cols] * (1.0 / sel[g][1][:, cols])
        o_w = win[g][2][:, cols] * (1.0 / win[g][1][:, cols])
        outs.append(gates[3 * h:3 * h + 1] * oct_ref[h * NSA_D:(h + 1) * NSA_D]
                    + gates[3 * h + 1:3 * h + 2] * o_s + gates[3 * h + 2:3 * h + 3] * o_w)
    y_ref[...] = jnp.concatenate(outs, axis=0).T.astype(BF16)


def _nsa_main(act, qt, sbt, oct, gt, ksa, kwa, vst, vwt, tq, tk_sel, tk_win):
    b, _, s = qt.shape
    act_tiles = act.shape[0] // (b * NSA_GROUPS * (s // tq))
    feat = lambda r: pl.BlockSpec((None, r, tq), lambda bi, i, _: (bi, 0, i))
    grid_spec = pltpu.PrefetchScalarGridSpec(
        num_scalar_prefetch=1,
        grid=(b, s // tq),
        in_specs=[feat(NSA_HEADS * LANES),
                  pl.BlockSpec((None, NSA_GROUPS, LANES, tq), lambda bi, i, _: (bi, 0, 0, i)),
                  feat(NSA_HEADS * NSA_D), feat(3 * NSA_HEADS),
                  pl.BlockSpec((None, NSA_GROUPS, s, 2 * LANES), lambda bi, i, _: (bi, 0, 0, 0)),
                  pl.BlockSpec((None, NSA_GROUPS, s, LANES), lambda bi, i, _: (bi, 0, 0, 0)),
                  pl.BlockSpec((None, LANES, s), lambda bi, i, _: (bi, 0, 0)),
                  pl.BlockSpec((None, LANES, s), lambda bi, i, _: (bi, 0, 0))],
        out_specs=pl.BlockSpec((None, tq, NSA_HEADS * NSA_D), lambda bi, i, _: (bi, i, 0)),
        scratch_shapes=[pltpu.VMEM((2, NSA_GROUPS, tk_sel, NSA_HPG * tq), F32),
                        pltpu.VMEM((2, NSA_GROUPS, tk_sel, NSA_HPG * tq), BF16),
                        pltpu.SMEM((NSA_GROUPS, act_tiles), jnp.int32)])
    return pl.pallas_call(
        functools.partial(_nsa_main_kernel, tq=tq, tk_sel=tk_sel, tk_win=tk_win, act_tiles=act_tiles),
        grid_spec=grid_spec,
        out_shape=jax.ShapeDtypeStruct((b, s, NSA_HEADS * NSA_D), BF16),
        compiler_params=_params("parallel", "arbitrary"),
        name="nsa_main",
    )(act, qt, sbt, oct, gt, ksa, kwa, vst, vwt)


def _mla_kernel(qt_ref, k_ref, vt_ref, y_ref, s_buf, p_buf, *, tq, tk):
    t0 = pl.program_id(2) * tq
    t_lane = t0 + lax.broadcasted_iota(jnp.int32, (1, tq), 1)
    heads = range(2)
    qts = [qt_ref[hh * LANES:(hh + 1) * LANES] for hh in heads]

    def scores(kt):
        k0 = pl.multiple_of(kt * tk, tk)
        return [_dot(k_ref[hh, pl.ds(k0, tk), :], qts[hh]) for hh in heads]

    def values(kt):
        k0 = pl.multiple_of(kt * tk, tk)
        return [vt_ref[hh * MLA_V:(hh + 1) * MLA_V, pl.ds(k0, tk)] for hh in heads]

    assert tk == tq
    diag = pl.program_id(2)
    causal = t0 + lax.broadcasted_iota(jnp.int32, (tk, 1), 0) <= t_lane
    st = []
    for s, vt in zip(scores(diag), values(diag)):
        s = jnp.where(causal, s, NEG)
        m = jnp.max(s, axis=0, keepdims=True)
        p = jnp.exp(s - m)
        st.append((m, jnp.sum(p, axis=0, keepdims=True), _dot(vt, p.astype(BF16))))
    st = _flash_pipelined(diag, scores, values, st, s_buf, p_buf)
    o = jnp.concatenate([st[hh][2] * (1.0 / st[hh][1]) for hh in heads], axis=0)
    y_ref[...] = o.T.astype(BF16)


def _mla(qmt, km, vmt, tq, tk):
    b, h, s, _ = km.shape
    k5 = km.reshape(b, h // 2, 2, s, LANES)
    return pl.pallas_call(
        functools.partial(_mla_kernel, tq=tq, tk=tk),
        grid=(b, h // 2, s // tq),
        in_specs=[pl.BlockSpec((None, 2 * LANES, tq), lambda bi, hp, i: (bi, hp, i)),
                  pl.BlockSpec((None, None, 2, s, LANES), lambda bi, hp, i: (bi, hp, 0, 0, 0)),
                  pl.BlockSpec((None, 2 * MLA_V, s), lambda bi, hp, i: (bi, hp, 0))],
        out_specs=pl.BlockSpec((None, tq, 2 * MLA_V), lambda bi, hp, i: (bi, i, hp)),
        out_shape=jax.ShapeDtypeStruct((b, s, h * MLA_V), BF16),
        scratch_shapes=[pltpu.VMEM((2, 2, tk, tq), F32), pltpu.VMEM((2, 2, tk, tq), BF16)],
        compiler_params=_params("parallel", "parallel", "arbitrary"),
        name="mla",
    )(qmt, k5, vmt)


def _merge_kernel(x_ref, yn_ref, ym_ref, pre_ref, post_ref, wgm_ref, wpn_ref, wpm_ref, wo_ref, o_ref):
    x = x_ref[...]
    d = x.shape[1]
    h = _rms(x, pre_ref[...]).astype(BF16)
    gm = jax.nn.sigmoid(_dot(h, wgm_ref[...]))
    merged = gm[:, :d] * _dot(yn_ref[...], wpn_ref[...]) + gm[:, d:] * _dot(ym_ref[...], wpm_ref[...])
    y = _dot(merged.astype(BF16), wo_ref[...])
    o_ref[...] = x + _rms(y, post_ref[...])


def _merge(x1, y_nsa, y_mla, pre_g, post_g, w_gm, w_pn, w_pm, w_out, tm):
    n, d = x1.shape
    tok = lambda w: pl.BlockSpec((tm, w), lambda i: (i, 0))
    return pl.pallas_call(
        _merge_kernel,
        grid=(n // tm,),
        in_specs=[tok(d), tok(y_nsa.shape[1]), tok(y_mla.shape[1]), _const_spec((1, d)), _const_spec((1, d)),
                  _const_spec(w_gm.shape), _const_spec(w_pn.shape), _const_spec(w_pm.shape),
                  _const_spec(w_out.shape)],
        out_specs=tok(d),
        out_shape=jax.ShapeDtypeStruct((n, d), F32),
        compiler_params=_params("parallel"),
        name="merge",
    )(x1, y_nsa, y_mla, pre_g.reshape(1, d), post_g.reshape(1, d),
      w_gm.astype(BF16), w_pn.astype(BF16), w_pm.astype(BF16), w_out.astype(BF16))


def kernel(x, ff1_pre_g, ff1_post_g, ff1_w_gate, ff1_w_up, ff1_w_down, mix_pre_g, mix_post_g, w_in, cmp_pos_k, cmp_w1_k, cmp_w2_k, cmp_pos_v, cmp_w1_v, cmp_w2_v, mla_q_norm_g, mla_w_uq, mla_kv_norm_g, mla_w_ukv, w_proj_nsa, w_proj_mla, w_out, ff2_pre_g, ff2_post_g, ff2_w_gate, ff2_w_up, ff2_w_down):
    b, s, d = x.shape
    assert s % (SEL_LEN * SEL_TOPK) == 0 and s // SEL_LEN <= LANES
    tl = _tiles(s)
    n = b * s

    x1 = _ffn(x.reshape(n, d), ff1_pre_g, ff1_post_g, ff1_w_gate, ff1_w_up, ff1_w_down, tl["tm"])

    (qt, ksa, kwa, vst, vwt, cmp_in, gt, qmt, km, vmt) = _inproj(
        x1.reshape(b, s, d), mix_pre_g, _inproj_weights(w_in, mla_w_uq, mla_w_ukv),
        mla_q_norm_g, mla_kv_norm_g, _rope_tables(s), tl["tm"])

    kc, vct = _compress(cmp_in, cmp_pos_k, cmp_w1_k, cmp_w2_k, cmp_pos_v, cmp_w1_v, cmp_w2_v)
    oct, sbt, act = _nsa_cmp(qt, kc, vct, _overlap_matrix_t(s), _tile_pool_matrix(s, tl["tk_sel"]),
                             tl["tq_cmp"], tl["tq_nsa"])
    act = (act[..., 0] > 0).astype(jnp.int32).reshape(-1)
    y_nsa = _nsa_main(act, qt, sbt, oct, gt, ksa, kwa, vst, vwt, tl["tq_nsa"], tl["tk_sel"], tl["tk_win"])
    y_mla = _mla(qmt, km, vmt, tl["tq_mla"], tl["tq_mla"])

    o_gm = sum((NSA_HEADS * NSA_D, 6 * NSA_GROUPS * NSA_D, 3 * NSA_HEADS, MLA_Q_RANK, MLA_KV_RANK, MLA_ROPE))
    x2 = _merge(x1, y_nsa.reshape(n, -1), y_mla.reshape(n, -1), mix_pre_g, mix_post_g,
                w_in[:, o_gm:], w_proj_nsa, w_proj_mla, w_out, tl["tm"])
    x3 = _ffn(x2, ff2_pre_g, ff2_post_g, ff2_w_gate, ff2_w_up, ff2_w_down, tl["tm"])
    return x3.reshape(b, s, d)
```

```python
import functools

import numpy as np
import jax
import jax.numpy as jnp
from jax import lax
from jax.experimental import pallas as pl
from jax.experimental.pallas import tpu as pltpu

F32 = jnp.float32
BF16 = jnp.bfloat16

EPS = 1e-6
NEG = -1e30
FORCE_SCORE = 1e4
NSA_HEADS = 8
NSA_GROUPS = 2
NSA_HPG = NSA_HEADS // NSA_GROUPS
NSA_D = 64
CMP_LEN = 32
CMP_STRIDE = 16
CMP_HID = 256
END_SHIFT = 4
SEL_LEN = 64
SEL_SHIFT = 6
SEL_TOPK = 16
WINDOW = 512
MLA_HEADS = 8
MLA_NOPE = 64
MLA_ROPE = 32
MLA_V = 64
MLA_Q_RANK = 256
MLA_KV_RANK = 128
ROPE_THETA = 10000.0
LANES = 128
POS_SHIFT = 7
LOG2E = 1.4426950408889634
VMEM_LIMIT = 56 * 1024 * 1024


def _tiles(seq):
    return dict(
        tm=min(512, seq),
        tq_cmp=min(1024, seq),
        tq_nsa=min(256, seq),
        tk_sel=min(256, seq),
        tk_win=LANES,
        tq_mla=min(512, seq),
    )


def _params(*sem):
    return pltpu.CompilerParams(dimension_semantics=sem, vmem_limit_bytes=VMEM_LIMIT)


def _const_spec(shape):
    nd = len(shape)
    return pl.BlockSpec(shape, lambda *_: (0,) * nd)


def _rms(x, g):
    return x * lax.rsqrt(jnp.mean(x * x, axis=-1, keepdims=True) + EPS) * g


def _dot(a, b):
    return jnp.dot(a, b, preferred_element_type=F32)


def _dot_nt(a, b):
    return lax.dot_general(a, b, (((1,), (1,)), ((), ())), preferred_element_type=F32)


def _alibi_slopes():
    return [float(2.0 ** (-8.0 * (i + 1) / NSA_HEADS)) for i in range(NSA_HEADS)]


def _ffn_kernel(x_ref, pre_ref, post_ref, wg_ref, wu_ref, wd_ref, o_ref, *, fc):
    x = x_ref[...]
    h = _rms(x, pre_ref[...]).astype(BF16)
    acc = jnp.zeros(x.shape, F32)
    for c in range(wg_ref.shape[1] // fc):
        gate = _dot(h, wg_ref[:, c * fc:(c + 1) * fc])
        up = _dot(h, wu_ref[:, c * fc:(c + 1) * fc])
        act = (gate * jax.nn.sigmoid(gate) * up).astype(BF16)
        acc = acc + _dot(act, wd_ref[c * fc:(c + 1) * fc, :])
    o_ref[...] = x + 0.5 * _rms(acc, post_ref[...])


def _ffn(x2d, pre_g, post_g, w_gate, w_up, w_down, tm):
    n, d = x2d.shape
    f = w_gate.shape[1]
    fc = f // 2 if (f // 2) % LANES == 0 else f
    return pl.pallas_call(
        functools.partial(_ffn_kernel, fc=fc),
        grid=(n // tm,),
        in_specs=[pl.BlockSpec((tm, d), lambda i: (i, 0)),
                  _const_spec((1, d)), _const_spec((1, d)),
                  _const_spec((d, f)), _const_spec((d, f)), _const_spec((f, d))],
        out_specs=pl.BlockSpec((tm, d), lambda i: (i, 0)),
        out_shape=jax.ShapeDtypeStruct((n, d), F32),
        compiler_params=_params("parallel"),
        name="ffn",
    )(x2d, pre_g.reshape(1, d), post_g.reshape(1, d),
      w_gate.astype(BF16), w_up.astype(BF16), w_down.astype(BF16))


_T_KSEL = 0
_T_KWIN = _T_KSEL + 2 * LANES
_T_CMP = _T_KWIN + 2 * LANES
_T_CQ = _T_CMP + 2 * LANES
_T_CKV = _T_CQ + MLA_Q_RANK
_T_KPE = _T_CKV + MLA_KV_RANK
_T_END = _T_KPE + 2 * LANES
_F_QN = 0
_F_VSEL = _F_QN + NSA_HEADS * LANES
_F_VWIN = _F_VSEL + LANES
_F_GATE = _F_VWIN + LANES
_F_END = _F_GATE + 32


def _inproj_weights(w_in, w_uq, w_ukv):
    d = w_in.shape[0]
    o_q = 0
    o_kv = o_q + NSA_HEADS * NSA_D
    o_g = o_kv + 6 * NSA_GROUPS * NSA_D
    o_cq = o_g + 3 * NSA_HEADS
    o_ckv = o_cq + MLA_Q_RANK
    o_kpe = o_ckv + MLA_KV_RANK
    zeros = lambda n: jnp.zeros((d, n), w_in.dtype)
    kv = lambda j, g: w_in[:, o_kv + (j * NSA_GROUPS + g) * NSA_D:o_kv + (j * NSA_GROUPS + g + 1) * NSA_D]
    half = MLA_ROPE // 2
    kp1 = w_in[:, o_kpe:o_kpe + half]
    kp2 = w_in[:, o_kpe + half:o_kpe + MLA_ROPE]
    tail = LANES - MLA_NOPE - MLA_ROPE
    cols = [kv(2, 0), zeros(NSA_D), kv(2, 1), zeros(NSA_D),
            kv(4, 0), zeros(NSA_D), kv(4, 1), zeros(NSA_D),
            kv(0, 0), kv(0, 1), kv(1, 0), kv(1, 1),
            w_in[:, o_cq:o_cq + MLA_Q_RANK], w_in[:, o_ckv:o_ckv + MLA_KV_RANK],
            zeros(MLA_NOPE), kp1, kp2, zeros(tail), zeros(MLA_NOPE), kp2, kp1, zeros(tail)]
    w_tok = jnp.concatenate(cols, axis=1).astype(BF16)
    assert w_tok.shape[1] == _T_END
    rows = []
    for h in range(NSA_HEADS):
        wq = w_in[:, o_q + h * NSA_D:o_q + (h + 1) * NSA_D]
        rows += [wq, zeros(NSA_D)] if h // NSA_HPG == 0 else [zeros(NSA_D), wq]
    rows += [kv(3, 0), kv(3, 1), kv(5, 0), kv(5, 1)]
    rows += [w_in[:, o_g:o_g + 3 * NSA_HEADS], zeros(_F_END - _F_GATE - 3 * NSA_HEADS)]
    w_feat = jnp.concatenate(rows, axis=1).T.astype(BF16)
    assert w_feat.shape[0] == _F_END

    dq = MLA_NOPE + MLA_ROPE
    zq = lambda n: jnp.zeros((w_uq.shape[0], n), w_uq.dtype)
    qa, qs = [], []
    for h in range(MLA_HEADS):
        nope = w_uq[:, h * dq:h * dq + MLA_NOPE]
        r1 = w_uq[:, h * dq + MLA_NOPE:h * dq + MLA_NOPE + half]
        r2 = w_uq[:, h * dq + MLA_NOPE + half:(h + 1) * dq]
        qa += [nope, r1, r2, zq(tail)]
        qs += [zq(MLA_NOPE), r2, r1, zq(tail)]
    w_q2t = jnp.concatenate(qa + qs, axis=1).T.astype(BF16)

    dkv = MLA_NOPE + MLA_V
    zk = lambda n: jnp.zeros((w_ukv.shape[0], n), w_ukv.dtype)
    ka, va = [], []
    for h in range(MLA_HEADS):
        ka += [w_ukv[:, h * dkv:h * dkv + MLA_NOPE], zk(LANES - MLA_NOPE)]
        va += [w_ukv[:, h * dkv + MLA_NOPE:(h + 1) * dkv]]
    w_k2 = jnp.concatenate(ka, axis=1).astype(BF16)
    w_v2t = jnp.concatenate(va, axis=1).T.astype(BF16)
    return w_tok, w_feat, w_q2t, w_k2, w_v2t


def _rope_tables(seq):
    half = MLA_ROPE // 2
    freqs = jnp.asarray(ROPE_THETA ** (-np.arange(half, dtype=np.float32) / half), F32)
    ang = jnp.arange(seq).astype(F32)[:, None] * freqs[None, :]
    cos, sin = jnp.cos(ang), jnp.sin(ang)
    pad = jnp.zeros((seq, LANES - MLA_NOPE - MLA_ROPE), F32)
    cq = jnp.concatenate([jnp.ones((seq, MLA_NOPE), F32), cos, cos, pad], axis=1)
    ck = jnp.concatenate([jnp.zeros((seq, MLA_NOPE), F32), cos, cos, pad], axis=1)
    sn = jnp.concatenate([jnp.zeros((seq, MLA_NOPE), F32), -sin, sin, pad], axis=1)
    return ck, sn, cq.T, sn.T


def _inproj_kernel(x_ref, g_ref, wt_ref, wf_ref, qg_ref, wq2_ref, kvg_ref, wk2_ref, wv2_ref,
                   ck_ref, sn_ref, cqt_ref, snt_ref,
                   qt_ref, ksa_ref, kwa_ref, vst_ref, vwt_ref, cmp_ref, gt_ref, qmt_ref, km_ref, vmt_ref,
                   *, tm):
    h = _rms(x_ref[...], g_ref[...]).astype(BF16)
    z = _dot(h, wt_ref[...])
    zt = _dot_nt(wf_ref[...], h)
    qt_ref[...] = (zt[_F_QN:_F_VSEL] * NSA_D ** -0.5).astype(BF16)
    vst_ref[...] = zt[_F_VSEL:_F_VWIN].astype(BF16)
    vwt_ref[...] = zt[_F_VWIN:_F_GATE].astype(BF16)
    gt_ref[...] = jax.nn.sigmoid(zt[_F_GATE:_F_GATE + 3 * NSA_HEADS])
    pos = pl.program_id(1) * tm + lax.broadcasted_iota(jnp.int32, (tm, LANES), 0)
    lane = lax.broadcasted_iota(jnp.int32, (tm, LANES), 1)
    onehot = (lane == lax.shift_right_logical(pos, SEL_SHIFT)).astype(BF16)
    in_blk = (pos & (SEL_LEN - 1)).astype(F32)
    pos_lo_hi = jnp.where(lane == NSA_D, pos & (LANES - 1), lax.shift_right_logical(pos, POS_SHIFT)).astype(F32)
    for g in range(NSA_GROUPS):
        ksa_ref[g, :, :LANES] = onehot
        ksa_ref[g, :, LANES:] = jnp.where(
            lane == NSA_D, in_blk, z[:, _T_KSEL + g * LANES:_T_KSEL + (g + 1) * LANES]).astype(BF16)
        kwa_ref[g] = jnp.where(
            (lane == NSA_D) | (lane == NSA_D + 1), pos_lo_hi,
            z[:, _T_KWIN + g * LANES:_T_KWIN + (g + 1) * LANES]).astype(BF16)
    for j in range(4):
        cmp_ref[j] = z[:, _T_CMP + j * NSA_D:_T_CMP + (j + 1) * NSA_D]
    cqn = _rms(z[:, _T_CQ:_T_CKV], qg_ref[...]).astype(BF16)
    q2t = _dot_nt(wq2_ref[...], cqn)
    cqt, snt = cqt_ref[...], snt_ref[...]
    nq = MLA_HEADS * LANES
    scale = (MLA_NOPE + MLA_ROPE) ** -0.5 * LOG2E
    for hh in range(MLA_HEADS):
        qa = q2t[hh * LANES:(hh + 1) * LANES]
        qs = q2t[nq + hh * LANES:nq + (hh + 1) * LANES]
        qmt_ref[hh * LANES:(hh + 1) * LANES] = ((qa * cqt + qs * snt) * scale).astype(BF16)
    ckvn = _rms(z[:, _T_CKV:_T_KPE], kvg_ref[...]).astype(BF16)
    k2 = _dot(ckvn, wk2_ref[...])
    krot = z[:, _T_KPE:_T_KPE + LANES] * ck_ref[...] + z[:, _T_KPE + LANES:_T_END] * sn_ref[...]
    for hh in range(MLA_HEADS):
        km_ref[hh] = (k2[:, hh * LANES:(hh + 1) * LANES] + krot).astype(BF16)
    vmt_ref[...] = _dot_nt(wv2_ref[...], ckvn).astype(BF16)


def _inproj(x1, mix_pre_g, weights, q_norm_g, kv_norm_g, tables, tm):
    b, s, d = x1.shape
    w_tok, w_feat, w_q2t, w_k2, w_v2t = weights
    ck, sn, cqt, snt = tables
    tok = lambda w: pl.BlockSpec((None, tm, w), lambda bi, i: (bi, i, 0))
    feat = lambda r: pl.BlockSpec((None, r, tm), lambda bi, i: (bi, 0, i))
    heads = lambda n, w: pl.BlockSpec((None, n, tm, w), lambda bi, i: (bi, 0, i, 0))
    tab = pl.BlockSpec((tm, LANES), lambda bi, i: (i, 0))
    tabt = pl.BlockSpec((LANES, tm), lambda bi, i: (0, i))
    sds = jax.ShapeDtypeStruct
    return pl.pallas_call(
        functools.partial(_inproj_kernel, tm=tm),
        grid=(b, s // tm),
        in_specs=[tok(d), _const_spec((1, d)), _const_spec(w_tok.shape), _const_spec(w_feat.shape),
                  _const_spec((1, MLA_Q_RANK)), _const_spec(w_q2t.shape),
                  _const_spec((1, MLA_KV_RANK)), _const_spec(w_k2.shape), _const_spec(w_v2t.shape),
                  tab, tab, tabt, tabt],
        out_specs=[feat(NSA_HEADS * LANES), heads(NSA_GROUPS, 2 * LANES), heads(NSA_GROUPS, LANES),
                   feat(LANES), feat(LANES), heads(4, NSA_D), feat(3 * NSA_HEADS),
                   feat(MLA_HEADS * LANES), heads(MLA_HEADS, LANES), feat(MLA_HEADS * MLA_V)],
        out_shape=[sds((b, NSA_HEADS * LANES, s), BF16), sds((b, NSA_GROUPS, s, 2 * LANES), BF16),
                   sds((b, NSA_GROUPS, s, LANES), BF16), sds((b, LANES, s), BF16), sds((b, LANES, s), BF16),
                   sds((b, 4, s, NSA_D), F32), sds((b, 3 * NSA_HEADS, s), F32),
                   sds((b, MLA_HEADS * LANES, s), BF16), sds((b, MLA_HEADS, s, LANES), BF16),
                   sds((b, MLA_HEADS * MLA_V, s), BF16)],
        compiler_params=_params("parallel", "parallel"),
        name="inproj",
    )(x1, mix_pre_g.reshape(1, d), w_tok, w_feat, q_norm_g.reshape(1, -1), w_q2t,
      kv_norm_g.reshape(1, -1), w_k2, w_v2t, ck, sn, cqt, snt)


def _compress_kernel(c_ref, pos_ref, w1_ref, w2k_ref, w2vt_ref, kc_ref, vct_ref):
    nc = c_ref.shape[1]
    lane = lax.broadcasted_iota(jnp.int32, (nc, NSA_D), 1)
    blk = lax.broadcasted_iota(jnp.int32, (nc, NSA_D), 0)
    end_cols = jnp.where(lane == 0, lax.shift_right_logical(blk, END_SHIFT),
                         jnp.where(lane == 1, blk & ((1 << END_SHIFT) - 1), 0)).astype(F32)
    for j in range(4):
        kv, g = divmod(j, NSA_GROUPS)
        c = c_ref[j]
        a0 = _dot((c + pos_ref[kv, 0]).astype(BF16), w1_ref[kv, 0])
        a1 = _dot((c + pos_ref[kv, 1]).astype(BF16), w1_ref[kv, 1])
        pre = a0 + pltpu.roll(a1, nc - 1, 0)
        hid = jax.nn.gelu(pre).astype(BF16)
        if kv == 0:
            kc_ref[g] = jnp.concatenate([_dot(hid, w2k_ref[...]), end_cols], axis=1).astype(BF16)
        else:
            vct_ref[g * NSA_D:(g + 1) * NSA_D] = _dot_nt(w2vt_ref[...], hid).astype(BF16)


def _compress(cmp_in, pos_k, w1_k, w2_k, pos_v, w1_v, w2_v):
    b, _, s, dk = cmp_in.shape
    nc = s // CMP_STRIDE
    kdim = CMP_STRIDE * dk
    c4 = cmp_in.reshape(b, 4, nc, kdim)
    pos = jnp.stack([pos_k.reshape(2, 1, kdim), pos_v.reshape(2, 1, kdim)])
    w1 = jnp.stack([w1_k.reshape(2, kdim, CMP_HID), w1_v.reshape(2, kdim, CMP_HID)]).astype(BF16)
    return pl.pallas_call(
        _compress_kernel,
        grid=(b,),
        in_specs=[pl.BlockSpec((None, 4, nc, kdim), lambda bi: (bi, 0, 0, 0)),
                  _const_spec(pos.shape), _const_spec(w1.shape),
                  _const_spec((CMP_HID, dk)), _const_spec((dk, CMP_HID))],
        out_specs=[pl.BlockSpec((None, NSA_GROUPS, nc, LANES), lambda bi: (bi, 0, 0, 0)),
                   pl.BlockSpec((None, LANES, nc), lambda bi: (bi, 0, 0))],
        out_shape=[jax.ShapeDtypeStruct((b, NSA_GROUPS, nc, LANES), BF16),
                   jax.ShapeDtypeStruct((b, LANES, nc), BF16)],
        compiler_params=_params("parallel"),
        name="compress",
    )(c4, pos, w1, w2_k.astype(BF16), w2_v.T.astype(BF16))


def _nsa_cmp_kernel(qt_ref, kc_ref, vct_ref, ovt_ref, pool_ref, oct_ref, sbt_ref, act_ref, *, tq, tq_main):
    nc = kc_ref.shape[1]
    t = pl.program_id(1) * tq + lax.broadcasted_iota(jnp.int32, (1, tq), 1)
    cmp_end = lax.broadcasted_iota(jnp.int32, (nc, 1), 0) * CMP_STRIDE + (CMP_LEN - 1)
    valid = cmp_end <= t
    any_valid = t >= CMP_LEN - 1
    slopes = _alibi_slopes()
    row = lax.broadcasted_iota(jnp.int32, (NSA_D, tq), 0)
    blk = lax.broadcasted_iota(jnp.int32, (LANES, tq), 0)
    blkf = blk.astype(F32)
    cur = lax.shift_right_logical(t, SEL_SHIFT)
    forced = (blk == 0) | (blk == cur) | (blk == cur - 1)
    for g in range(NSA_GROUPS):
        vo = jnp.concatenate([vct_ref[g * NSA_D:(g + 1) * NSA_D], ovt_ref[...]], axis=0)
        imp = jnp.zeros((LANES, tq), F32)
        for hh in range(NSA_HPG):
            h = g * NSA_HPG + hh
            hi_w = slopes[h] * (CMP_STRIDE << END_SHIFT)
            tail = jnp.where(row == 0, hi_w, jnp.where(row == 1, slopes[h] * CMP_STRIDE, 0.0)).astype(BF16)
            q = jnp.concatenate([qt_ref[h * LANES + g * NSA_D:h * LANES + (g + 1) * NSA_D], tail], axis=0)
            s = jnp.where(valid, _dot(kc_ref[g], q), NEG)
            e = jnp.exp(s - jnp.max(s, axis=0, keepdims=True))
            norm = jnp.where(any_valid, 1.0 / jnp.maximum(jnp.sum(e, axis=0, keepdims=True), 1e-30), 0.0)
            r = _dot(vo, e.astype(BF16)) * norm
            oct_ref[h * NSA_D:(h + 1) * NSA_D] = r[:NSA_D]
            imp = imp + r[NSA_D:]
        work = jnp.where(blk <= cur, jnp.where(forced, FORCE_SCORE, imp), NEG)
        chosen = jnp.zeros((LANES, tq), jnp.bool_)
        for _ in range(SEL_TOPK):
            top = jnp.max(work, axis=0, keepdims=True)
            idx = jnp.min(jnp.where(work == top, blkf, float(LANES)), axis=0, keepdims=True)
            hit = blkf == idx
            chosen = chosen | hit
            work = jnp.where(hit, -jnp.inf, work)
        sbt_ref[g] = jnp.where(chosen, 0.0, NEG).astype(BF16)
        used = _dot(pool_ref[...], chosen.astype(BF16))
        for j in range(tq // tq_main):
            any_q = jnp.max(used[:, j * tq_main:(j + 1) * tq_main], axis=1, keepdims=True)
            act_ref[g, j] = jnp.broadcast_to(any_q, act_ref.shape[2:])


def _nsa_cmp(qt, kc, vct, overlap_t, pool, tq, tq_main):
    b, _, s = qt.shape
    nc = kc.shape[2]
    nkt = pool.shape[0]
    return pl.pallas_call(
        functools.partial(_nsa_cmp_kernel, tq=tq, tq_main=tq_main),
        grid=(b, s // tq),
        in_specs=[pl.BlockSpec((None, NSA_HEADS * LANES, tq), lambda bi, i: (bi, 0, i)),
                  pl.BlockSpec((None, NSA_GROUPS, nc, LANES), lambda bi, i: (bi, 0, 0, 0)),
                  pl.BlockSpec((None, LANES, nc), lambda bi, i: (bi, 0, 0)),
                  _const_spec(overlap_t.shape), _const_spec(pool.shape)],
        out_specs=[pl.BlockSpec((None, NSA_HEADS * NSA_D, tq), lambda bi, i: (bi, 0, i)),
                   pl.BlockSpec((None, NSA_GROUPS, LANES, tq), lambda bi, i: (bi, 0, 0, i)),
                   pl.BlockSpec((None, NSA_GROUPS, tq // tq_main, nkt, LANES), lambda bi, i: (bi, 0, i, 0, 0))],
        out_shape=[jax.ShapeDtypeStruct((b, NSA_HEADS * NSA_D, s), F32),
                   jax.ShapeDtypeStruct((b, NSA_GROUPS, LANES, s), BF16),
                   jax.ShapeDtypeStruct((b, NSA_GROUPS, s // tq_main, nkt, LANES), F32)],
        compiler_params=_params("parallel", "parallel"),
        name="nsa_cmp",
    )(qt, kc, vct, overlap_t, pool)


def _tile_pool_matrix(seq, tk_sel):
    per_tile = tk_sel // SEL_LEN
    nkt = -(-(seq // tk_sel) // 16) * 16
    pool = np.zeros((nkt, LANES), np.float32)
    for c in range(seq // SEL_LEN):
        pool[c // per_tile, c] = 1.0
    return jnp.asarray(pool, BF16)


def _overlap_matrix_t(seq):
    n_c = (seq - CMP_LEN) // CMP_STRIDE + 1
    n_sel = seq // SEL_LEN
    c0 = np.arange(n_c) * CMP_STRIDE
    s0 = np.arange(n_sel) * SEL_LEN
    ov = np.clip(np.minimum(c0[:, None] + CMP_LEN, s0[None, :] + SEL_LEN)
                 - np.maximum(c0[:, None], s0[None, :]), 0, None) / CMP_LEN
    full = np.zeros((LANES, seq // CMP_STRIDE), np.float32)
    full[:n_sel, :n_c] = ov.T
    return jnp.asarray(full, BF16)


def _flash_step(s, vt, m, l, acc):
    m_new = jnp.maximum(m, jnp.max(s, axis=0, keepdims=True))
    alpha = jnp.exp(m - m_new)
    p = jnp.exp(s - m_new)
    l_new = alpha * l + jnp.sum(p, axis=0, keepdims=True)
    return m_new, l_new, alpha * acc + _dot(vt, p.astype(BF16))


def _flash_init(dv, nq):
    return (jnp.full((1, nq), NEG, F32), jnp.zeros((1, nq), F32), jnp.zeros((dv, nq), F32))


def _flash_pipelined(n_tiles, scores, values, state, s_buf, p_buf, exp=jnp.exp):
    chains = range(len(state))
    nq = state[0][0].shape[1]
    s0 = scores(0)
    for c in chains:
        s_buf[0, c] = s0[c]
        p_buf[1, c] = jnp.zeros(p_buf.shape[2:], BF16)
    colmax = tuple(jnp.max(s0[c], axis=0, keepdims=True) for c in chains)
    ones = tuple(jnp.ones((1, nq), F32) for _ in chains)

    def step(i, carry, cur):
        st, alphas, cmax = carry
        nxt = 1 - cur
        vts = values(jnp.maximum(i - 1, 0))
        s_next = scores(i + 1)
        new_st, new_alpha, new_cmax = [], [], []
        for c in chains:
            m, l, acc = st[c]
            acc = alphas[c] * acc + _dot(vts[c], p_buf[nxt, c])
            m_new = jnp.maximum(m, cmax[c])
            alpha = exp(m - m_new)
            p = exp(s_buf[cur, c] - m_new)
            l = alpha * l + jnp.sum(p, axis=0, keepdims=True)
            p_buf[cur, c] = p.astype(BF16)
            s_buf[nxt, c] = s_next[c]
            new_st.append((m_new, l, acc))
            new_alpha.append(alpha)
            new_cmax.append(jnp.max(s_next[c], axis=0, keepdims=True))
        return tuple(new_st), tuple(new_alpha), tuple(new_cmax)

    def steps(first, count, carry):
        for u in range(count):
            carry = step(first + u, carry, u % 2)
        return carry

    n_pairs = n_tiles // 2
    carry = lax.fori_loop(0, n_pairs, lambda j, c: steps(2 * j, 2, c), (tuple(state), ones, colmax))
    st, alphas, _ = lax.fori_loop(2 * n_pairs, n_tiles, lambda i, c: steps(i, 1, c), carry)
    vts = values(jnp.maximum(n_tiles - 1, 0))
    last = (n_tiles + 1) & 1
    return tuple((m, l, alphas[c] * acc + _dot(vts[c], p_buf[last, c]))
                 for c, (m, l, acc) in enumerate(st))


def _nsa_main_kernel(act_ref, qt_ref, sbt_ref, oct_ref, gt_ref, ksa_ref, kwa_ref, vst_ref, vwt_ref, y_ref,
                     s_buf, p_buf, tiles_ref, *, tq, tk_sel, tk_win, act_tiles):
    t0 = pl.program_id(1) * tq
    nq = NSA_HPG * tq
    slopes = _alibi_slopes()
    t_lane = jnp.concatenate([t0 + lax.broadcasted_iota(jnp.int32, (1, tq), 1)] * NSA_HPG, axis=1)
    row = lax.broadcasted_iota(jnp.int32, (NSA_D, nq), 0)
    blk_rel = (lax.broadcasted_iota(jnp.int32, (LANES, tq), 0)
               - lax.shift_right_logical(t0, SEL_SHIFT)).astype(F32)

    q_sel, q_win, slope_l = [], [], []
    for g in range(NSA_GROUPS):
        heads = range(g * NSA_HPG, (g + 1) * NSA_HPG)
        sl = jnp.concatenate([jnp.full((1, tq), slopes[h], F32) for h in heads], axis=1)
        qg = jnp.concatenate(
            [qt_ref[h * LANES + g * NSA_D:h * LANES + (g + 1) * NSA_D] for h in heads], axis=1)
        tail = jnp.where(row == 0, sl, 0.0).astype(BF16)
        tail_win = jnp.where(row == 0, sl, jnp.where(row == 1, sl * LANES, 0.0)).astype(BF16)
        sb = sbt_ref[g].astype(F32)
        bias = jnp.concatenate([sb + (slopes[h] * SEL_LEN) * blk_rel for h in heads], axis=1)
        q_sel.append(jnp.concatenate([bias.astype(BF16), qg, tail], axis=0))
        q_win.append(jnp.concatenate([qg, tail_win], axis=0))

    groups = range(NSA_GROUPS)

    def sel_scores(kt, g):
        k0 = pl.multiple_of(kt * tk_sel, tk_sel)
        return _dot(ksa_ref[g, pl.ds(k0, tk_sel), :], q_sel[g])

    def sel_values(kt, g):
        k0 = pl.multiple_of(kt * tk_sel, tk_sel)
        return vst_ref[g * NSA_D:(g + 1) * NSA_D, pl.ds(k0, tk_sel)]

    n_full = (t0 + 1) // tk_sel
    n_kt = tiles_ref.shape[1]
    counts, spare = [], []
    for g in groups:
        base = ((pl.program_id(0) * NSA_GROUPS + g) * pl.num_programs(1) + pl.program_id(1)) * act_tiles
        for k in range(n_kt):
            tiles_ref[g, k] = 0

        def scan(kt, carry, g=g, base=base):
            cnt, unused = carry
            used = act_ref[base + kt]
            tiles_ref[g, cnt] = kt
            return cnt + used, jnp.where(used == 0, kt, unused)

        cnt, unused = lax.fori_loop(0, n_full, scan, (jnp.int32(0), jnp.int32(0)))
        counts.append(cnt)
        spare.append(unused)

    def tile_of(k, g):
        return jnp.where(k < counts[g], tiles_ref[g, jnp.minimum(k, n_kt - 1)], spare[g])

    diag0 = pl.multiple_of((t0 // tk_sel) * tk_sel, tk_sel)
    causal = diag0 + lax.broadcasted_iota(jnp.int32, (tk_sel, 1), 0) <= t_lane
    sel = tuple(_flash_step(jnp.where(causal, sel_scores(t0 // tk_sel, g), NEG), sel_values(t0 // tk_sel, g),
                            *_flash_init(NSA_D, nq)) for g in groups)
    sel = _flash_pipelined(jnp.maximum(counts[0], counts[1]),
                           lambda k: [sel_scores(tile_of(k, g), g) for g in groups],
                           lambda k: [sel_values(tile_of(k, g), g) for g in groups], sel, s_buf, p_buf)

    start = pl.multiple_of(jnp.maximum(t0 - WINDOW, 0), tk_win)
    n_win = WINDOW + tq
    key = start + lax.broadcasted_iota(jnp.int32, (n_win, 1), 0)
    edge = lambda k: (k <= t_lane) & (k > t_lane - WINDOW)
    in_window = jnp.concatenate([edge(key[:tq]), key[tq:n_win - tq] <= t_lane, edge(key[n_win - tq:])], axis=0)
    win = []
    for g in groups:
        s = _dot(kwa_ref[g, pl.ds(start, n_win), :], q_win[g])
        s = jnp.where(in_window, s, NEG)
        p = jnp.exp(s - jnp.max(s, axis=0, keepdims=True))
        win.append((None, jnp.sum(p, axis=0, keepdims=True),
                    _dot(vwt_ref[g * NSA_D:(g + 1) * NSA_D, pl.ds(start, n_win)], p.astype(BF16))))

    gates = gt_ref[...]
    outs = []
    for h in range(NSA_HEADS):
        g, hh = divmod(h, NSA_HPG)
        cols = slice(hh * tq, (hh + 1) * tq)
        o_s = sel[g][2][:, cols] * (1.0 / sel[g][1][:, cols])
        o_w = win[g][2][:, cols] * (1.0 / win[g][1][:, cols])
        outs.append(gates[3 * h:3 * h + 1] * oct_ref[h * NSA_D:(h + 1) * NSA_D]
                    + gates[3 * h + 1:3 * h + 2] * o_s + gates[3 * h + 2:3 * h + 3] * o_w)
    y_ref[...] = jnp.concatenate(outs, axis=0).T.astype(BF16)


def _nsa_main(act, qt, sbt, oct, gt, ksa, kwa, vst, vwt, tq, tk_sel, tk_win):
    b, _, s = qt.shape
    act_tiles = act.shape[0] // (b * NSA_GROUPS * (s // tq))
    feat = lambda r: pl.BlockSpec((None, r, tq), lambda bi, i, _: (bi, 0, i))
    grid_spec = pltpu.PrefetchScalarGridSpec(
        num_scalar_prefetch=1,
        grid=(b, s // tq),
        in_specs=[feat(NSA_HEADS * LANES),
                  pl.BlockSpec((None, NSA_GROUPS, LANES, tq), lambda bi, i, _: (bi, 0, 0, i)),
                  feat(NSA_HEADS * NSA_D), feat(3 * NSA_HEADS),
                  pl.BlockSpec((None, NSA_GROUPS, s, 2 * LANES), lambda bi, i, _: (bi, 0, 0, 0)),
                  pl.BlockSpec((None, NSA_GROUPS, s, LANES), lambda bi, i, _: (bi, 0, 0, 0)),
                  pl.BlockSpec((None, LANES, s), lambda bi, i, _: (bi, 0, 0)),
                  pl.BlockSpec((None, LANES, s), lambda bi, i, _: (bi, 0, 0))],
        out_specs=pl.BlockSpec((None, tq, NSA_HEADS * NSA_D), lambda bi, i, _: (bi, i, 0)),
        scratch_shapes=[pltpu.VMEM((2, NSA_GROUPS, tk_sel, NSA_HPG * tq), F32),
                        pltpu.VMEM((2, NSA_GROUPS, tk_sel, NSA_HPG * tq), BF16),
                        pltpu.SMEM((NSA_GROUPS, act_tiles), jnp.int32)])
    return pl.pallas_call(
        functools.partial(_nsa_main_kernel, tq=tq, tk_sel=tk_sel, tk_win=tk_win, act_tiles=act_tiles),
        grid_spec=grid_spec,
        out_shape=jax.ShapeDtypeStruct((b, s, NSA_HEADS * NSA_D), BF16),
        compiler_params=_params("parallel", "arbitrary"),
        name="nsa_main",
    )(act, qt, sbt, oct, gt, ksa, kwa, vst, vwt)


def _mla_kernel(qt_ref, k_ref, vt_ref, y_ref, s_buf, p_buf, *, tq, tk):
    t0 = pl.program_id(2) * tq
    t_lane = t0 + lax.broadcasted_iota(jnp.int32, (1, tq), 1)
    heads = range(2)
    qts = [qt_ref[hh * LANES:(hh + 1) * LANES] for hh in heads]

    def scores(kt):
        k0 = pl.multiple_of(kt * tk, tk)
        return [_dot(k_ref[hh, pl.ds(k0, tk), :], qts[hh]) for hh in heads]

    def values(kt):
        k0 = pl.multiple_of(kt * tk, tk)
        return [vt_ref[hh * MLA_V:(hh + 1) * MLA_V, pl.ds(k0, tk)] for hh in heads]

    assert tk == tq
    diag = pl.program_id(2)
    causal = t0 + lax.broadcasted_iota(jnp.int32, (tk, 1), 0) <= t_lane
    st = []
    for s, vt in zip(scores(diag), values(diag)):
        s = jnp.where(causal, s, NEG)
        m = jnp.max(s, axis=0, keepdims=True)
        p = jnp.exp2(s - m)
        st.append((m, jnp.sum(p, axis=0, keepdims=True), _dot(vt, p.astype(BF16))))
    st = _flash_pipelined(diag, scores, values, st, s_buf, p_buf, exp=jnp.exp2)
    o = jnp.concatenate([st[hh][2] * (1.0 / st[hh][1]) for hh in heads], axis=0)
    y_ref[...] = o.T.astype(BF16)


def _mla(qmt, km, vmt, tq, tk):
    b, h, s, _ = km.shape
    k5 = km.reshape(b, h // 2, 2, s, LANES)
    return pl.pallas_call(
        functools.partial(_mla_kernel, tq=tq, tk=tk),
        grid=(b, h // 2, s // tq),
        in_specs=[pl.BlockSpec((None, 2 * LANES, tq), lambda bi, hp, i: (bi, hp, i)),
                  pl.BlockSpec((None, None, 2, s, LANES), lambda bi, hp, i: (bi, hp, 0, 0, 0)),
                  pl.BlockSpec((None, 2 * MLA_V, s), lambda bi, hp, i: (bi, hp, 0))],
        out_specs=pl.BlockSpec((None, tq, 2 * MLA_V), lambda bi, hp, i: (bi, i, hp)),
        out_shape=jax.ShapeDtypeStruct((b, s, h * MLA_V), BF16),
        scratch_shapes=[pltpu.VMEM((2, 2, tk, tq), F32), pltpu.VMEM((2, 2, tk, tq), BF16)],
        compiler_params=_params("parallel", "parallel", "arbitrary"),
        name="mla",
    )(qmt, k5, vmt)


def _merge_kernel(x_ref, yn_ref, ym_ref, pre_ref, post_ref, wgm_ref, wpn_ref, wpm_ref, wo_ref, o_ref):
    x = x_ref[...]
    d = x.shape[1]
    h = _rms(x, pre_ref[...]).astype(BF16)
    gm = jax.nn.sigmoid(_dot(h, wgm_ref[...]))
    merged = gm[:, :d] * _dot(yn_ref[...], wpn_ref[...]) + gm[:, d:] * _dot(ym_ref[...], wpm_ref[...])
    y = _dot(merged.astype(BF16), wo_ref[...])
    o_ref[...] = x + _rms(y, post_ref[...])


def _merge(x1, y_nsa, y_mla, pre_g, post_g, w_gm, w_pn, w_pm, w_out, tm):
    n, d = x1.shape
    tok = lambda w: pl.BlockSpec((tm, w), lambda i: (i, 0))
    return pl.pallas_call(
        _merge_kernel,
        grid=(n // tm,),
        in_specs=[tok(d), tok(y_nsa.shape[1]), tok(y_mla.shape[1]), _const_spec((1, d)), _const_spec((1, d)),
                  _const_spec(w_gm.shape), _const_spec(w_pn.shape), _const_spec(w_pm.shape),
                  _const_spec(w_out.shape)],
        out_specs=tok(d),
        out_shape=jax.ShapeDtypeStruct((n, d), F32),
        compiler_params=_params("parallel"),
        name="merge",
    )(x1, y_nsa, y_mla, pre_g.reshape(1, d), post_g.reshape(1, d),
      w_gm.astype(BF16), w_pn.astype(BF16), w_pm.astype(BF16), w_out.astype(BF16))


def kernel(x, ff1_pre_g, ff1_post_g, ff1_w_gate, ff1_w_up, ff1_w_down, mix_pre_g, mix_post_g, w_in, cmp_pos_k, cmp_w1_k, cmp_w2_k, cmp_pos_v, cmp_w1_v, cmp_w2_v, mla_q_norm_g, mla_w_uq, mla_kv_norm_g, mla_w_ukv, w_proj_nsa, w_proj_mla, w_out, ff2_pre_g, ff2_post_g, ff2_w_gate, ff2_w_up, ff2_w_down):
    b, s, d = x.shape
    assert s % (SEL_LEN * SEL_TOPK) == 0 and s // SEL_LEN <= LANES
    tl = _tiles(s)
    n = b * s

    x1 = _ffn(x.reshape(n, d), ff1_pre_g, ff1_post_g, ff1_w_gate, ff1_w_up, ff1_w_down, tl["tm"])

    (qt, ksa, kwa, vst, vwt, cmp_in, gt, qmt, km, vmt) = _inproj(
        x1.reshape(b, s, d), mix_pre_g, _inproj_weights(w_in, mla_w_uq, mla_w_ukv),
        mla_q_norm_g, mla_kv_norm_g, _rope_tables(s), tl["tm"])

    kc, vct = _compress(cmp_in, cmp_pos_k, cmp_w1_k, cmp_w2_k, cmp_pos_v, cmp_w1_v, cmp_w2_v)
    oct, sbt, act = _nsa_cmp(qt, kc, vct, _overlap_matrix_t(s), _tile_pool_matrix(s, tl["tk_sel"]),
                             tl["tq_cmp"], tl["tq_nsa"])
    act = (act[..., 0] > 0).astype(jnp.int32).reshape(-1)
    y_nsa = _nsa_main(act, qt, sbt, oct, gt, ksa, kwa, vst, vwt, tl["tq_nsa"], tl["tk_sel"], tl["tk_win"])
    y_mla = _mla(qmt, km, vmt, tl["tq_mla"], tl["tq_mla"])

    o_gm = sum((NSA_HEADS * NSA_D, 6 * NSA_GROUPS * NSA_D, 3 * NSA_HEADS, MLA_Q_RANK, MLA_KV_RANK, MLA_ROPE))
    x2 = _merge(x1, y_nsa.reshape(n, -1), y_mla.reshape(n, -1), mix_pre_g, mix_post_g,
                w_in[:, o_gm:], w_proj_nsa, w_proj_mla, w_out, tl["tm"])
    x3 = _ffn(x2, ff2_pre_g, ff2_post_g, ff2_w_gate, ff2_w_up, ff2_w_down, tl["tm"])
    return x3.reshape(b, s, d)
```

```python
import functools

import numpy as np
import jax
import jax.numpy as jnp
from jax import lax
from jax.experimental import pallas as pl
from jax.experimental.pallas import tpu as pltpu

F32 = jnp.float32
BF16 = jnp.bfloat16

EPS = 1e-6
NEG = -1e30
FORCE_SCORE = 1e4
NSA_HEADS = 8
NSA_GROUPS = 2
NSA_HPG = NSA_HEADS // NSA_GROUPS
NSA_D = 64
CMP_LEN = 32
CMP_STRIDE = 16
CMP_HID = 256
END_SHIFT = 4
SEL_LEN = 64
SEL_SHIFT = 6
SEL_TOPK = 16
WINDOW = 512
MLA_HEADS = 8
MLA_NOPE = 64
MLA_ROPE = 32
MLA_V = 64
MLA_Q_RANK = 256
MLA_KV_RANK = 128
ROPE_THETA = 10000.0
LANES = 128
POS_SHIFT = 7
LOG2E = 1.4426950408889634
VMEM_LIMIT = 56 * 1024 * 1024
FFN_CHUNK = 256


def _tiles(seq):
    return dict(
        tm=min(1024, seq),
        tm_ffn=min(1024, seq),
        tq_cmp=min(1024, seq),
        tq_nsa=min(256, seq),
        tk_sel=min(256, seq),
        tk_win=LANES,
        tq_mla=min(512, seq),
    )


def _params(*sem):
    return pltpu.CompilerParams(dimension_semantics=sem, vmem_limit_bytes=VMEM_LIMIT)


def _const_spec(shape):
    nd = len(shape)
    return pl.BlockSpec(shape, lambda *_: (0,) * nd, pipeline_mode=pl.Buffered(1))


def _rms(x, g):
    return x * lax.rsqrt(jnp.mean(x * x, axis=-1, keepdims=True) + EPS) * g


def _dot(a, b):
    return jnp.dot(a, b, preferred_element_type=F32)


def _dot_nt(a, b):
    return lax.dot_general(a, b, (((1,), (1,)), ((), ())), preferred_element_type=F32)


def _alibi_slopes():
    return [float(2.0 ** (-8.0 * (i + 1) / NSA_HEADS)) for i in range(NSA_HEADS)]


def _ffn_kernel(x_ref, pre_ref, post_ref, wg_ref, wu_ref, wd_ref, o_ref, *, fc):
    x = x_ref[...]
    h = _rms(x, pre_ref[...]).astype(BF16)
    acc = jnp.zeros(x.shape, F32)
    for c in range(wg_ref.shape[1] // fc):
        gate = _dot(h, wg_ref[:, c * fc:(c + 1) * fc])
        up = _dot(h, wu_ref[:, c * fc:(c + 1) * fc])
        act = (gate * jax.nn.sigmoid(gate) * up).astype(BF16)
        acc = acc + _dot(act, wd_ref[c * fc:(c + 1) * fc, :])
    o_ref[...] = x + 0.5 * _rms(acc, post_ref[...])


def _ffn(x2d, pre_g, post_g, w_gate, w_up, w_down, tm):
    n, d = x2d.shape
    f = w_gate.shape[1]
    fc = FFN_CHUNK if f % FFN_CHUNK == 0 else f
    return pl.pallas_call(
        functools.partial(_ffn_kernel, fc=fc),
        grid=(n // tm,),
        in_specs=[pl.BlockSpec((tm, d), lambda i: (i, 0)),
                  _const_spec((1, d)), _const_spec((1, d)),
                  _const_spec((d, f)), _const_spec((d, f)), _const_spec((f, d))],
        out_specs=pl.BlockSpec((tm, d), lambda i: (i, 0)),
        out_shape=jax.ShapeDtypeStruct((n, d), F32),
        compiler_params=_params("parallel"),
        name="ffn",
    )(x2d, pre_g.reshape(1, d), post_g.reshape(1, d),
      w_gate.astype(BF16), w_up.astype(BF16), w_down.astype(BF16))


_T_KSEL = 0
_T_KWIN = _T_KSEL + 2 * LANES
_T_CMP = _T_KWIN + 2 * LANES
_T_CQ = _T_CMP + 2 * LANES
_T_CKV = _T_CQ + MLA_Q_RANK
_T_KPE = _T_CKV + MLA_KV_RANK
_T_END = _T_KPE + 2 * LANES
_F_QN = 0
_F_VSEL = _F_QN + NSA_HEADS * LANES
_F_VWIN = _F_VSEL + LANES
_F_GATE = _F_VWIN + LANES
_F_END = _F_GATE + 32


def _inproj_weights(w_in, w_uq, w_ukv):
    d = w_in.shape[0]
    o_q = 0
    o_kv = o_q + NSA_HEADS * NSA_D
    o_g = o_kv + 6 * NSA_GROUPS * NSA_D
    o_cq = o_g + 3 * NSA_HEADS
    o_ckv = o_cq + MLA_Q_RANK
    o_kpe = o_ckv + MLA_KV_RANK
    zeros = lambda n: jnp.zeros((d, n), w_in.dtype)
    kv = lambda j, g: w_in[:, o_kv + (j * NSA_GROUPS + g) * NSA_D:o_kv + (j * NSA_GROUPS + g + 1) * NSA_D]
    half = MLA_ROPE // 2
    kp1 = w_in[:, o_kpe:o_kpe + half]
    kp2 = w_in[:, o_kpe + half:o_kpe + MLA_ROPE]
    tail = LANES - MLA_NOPE - MLA_ROPE
    cols = [kv(2, 0), zeros(NSA_D), kv(2, 1), zeros(NSA_D),
            kv(4, 0), zeros(NSA_D), kv(4, 1), zeros(NSA_D),
            kv(0, 0), kv(0, 1), kv(1, 0), kv(1, 1),
            w_in[:, o_cq:o_cq + MLA_Q_RANK], w_in[:, o_ckv:o_ckv + MLA_KV_RANK],
            zeros(MLA_NOPE), kp1, kp2, zeros(tail), zeros(MLA_NOPE), kp2, kp1, zeros(tail)]
    w_tok = jnp.concatenate(cols, axis=1).astype(BF16)
    assert w_tok.shape[1] == _T_END
    rows = []
    for h in range(NSA_HEADS):
        wq = w_in[:, o_q + h * NSA_D:o_q + (h + 1) * NSA_D]
        rows += [wq, zeros(NSA_D)] if h // NSA_HPG == 0 else [zeros(NSA_D), wq]
    rows += [kv(3, 0), kv(3, 1), kv(5, 0), kv(5, 1)]
    rows += [w_in[:, o_g:o_g + 3 * NSA_HEADS], zeros(_F_END - _F_GATE - 3 * NSA_HEADS)]
    w_feat = jnp.concatenate(rows, axis=1).T.astype(BF16)
    assert w_feat.shape[0] == _F_END

    dq = MLA_NOPE + MLA_ROPE
    zq = lambda n: jnp.zeros((w_uq.shape[0], n), w_uq.dtype)
    qa, qs = [], []
    for h in range(MLA_HEADS):
        nope = w_uq[:, h * dq:h * dq + MLA_NOPE]
        r1 = w_uq[:, h * dq + MLA_NOPE:h * dq + MLA_NOPE + half]
        r2 = w_uq[:, h * dq + MLA_NOPE + half:(h + 1) * dq]
        qa += [nope, r1, r2, zq(tail)]
        qs += [zq(MLA_NOPE), r2, r1, zq(tail)]
    w_q2t = jnp.concatenate(qa + qs, axis=1).T.astype(BF16)

    dkv = MLA_NOPE + MLA_V
    zk = lambda n: jnp.zeros((w_ukv.shape[0], n), w_ukv.dtype)
    ka, va = [], []
    for h in range(MLA_HEADS):
        ka += [w_ukv[:, h * dkv:h * dkv + MLA_NOPE], zk(LANES - MLA_NOPE)]
        va += [w_ukv[:, h * dkv + MLA_NOPE:(h + 1) * dkv]]
    w_k2 = jnp.concatenate(ka, axis=1).astype(BF16)
    w_v2t = jnp.concatenate(va, axis=1).T.astype(BF16)
    return w_tok, w_feat, w_q2t, w_k2, w_v2t


def _rope_tables(seq):
    half = MLA_ROPE // 2
    freqs = jnp.asarray(ROPE_THETA ** (-np.arange(half, dtype=np.float32) / half), F32)
    ang = jnp.arange(seq).astype(F32)[:, None] * freqs[None, :]
    cos, sin = jnp.cos(ang), jnp.sin(ang)
    pad = jnp.zeros((seq, LANES - MLA_NOPE - MLA_ROPE), F32)
    cq = jnp.concatenate([jnp.ones((seq, MLA_NOPE), F32), cos, cos, pad], axis=1)
    ck = jnp.concatenate([jnp.zeros((seq, MLA_NOPE), F32), cos, cos, pad], axis=1)
    sn = jnp.concatenate([jnp.zeros((seq, MLA_NOPE), F32), -sin, sin, pad], axis=1)
    return ck, sn, cq.T, sn.T


def _inproj_kernel(x_ref, g_ref, wt_ref, wf_ref, qg_ref, wq2_ref, kvg_ref, wk2_ref, wv2_ref,
                   ck_ref, sn_ref, cqt_ref, snt_ref,
                   qt_ref, ksa_ref, kwa_ref, vst_ref, vwt_ref, cmp_ref, gt_ref, qmt_ref, km_ref, vmt_ref,
                   *, tm):
    h = _rms(x_ref[...], g_ref[...]).astype(BF16)
    z = _dot(h, wt_ref[...])
    zt = _dot_nt(wf_ref[...], h)
    qt_ref[...] = (zt[_F_QN:_F_VSEL] * NSA_D ** -0.5).astype(BF16)
    vst_ref[...] = zt[_F_VSEL:_F_VWIN].astype(BF16)
    vwt_ref[...] = zt[_F_VWIN:_F_GATE].astype(BF16)
    gt_ref[...] = jax.nn.sigmoid(zt[_F_GATE:_F_GATE + 3 * NSA_HEADS])
    pos = pl.program_id(1) * tm + lax.broadcasted_iota(jnp.int32, (tm, LANES), 0)
    lane = lax.broadcasted_iota(jnp.int32, (tm, LANES), 1)
    onehot = (lane == lax.shift_right_logical(pos, SEL_SHIFT)).astype(BF16)
    in_blk = (pos & (SEL_LEN - 1)).astype(F32)
    pos_lo_hi = jnp.where(lane == NSA_D, pos & (LANES - 1), lax.shift_right_logical(pos, POS_SHIFT)).astype(F32)
    for g in range(NSA_GROUPS):
        ksa_ref[g, :, :LANES] = onehot
        ksa_ref[g, :, LANES:] = jnp.where(
            lane == NSA_D, in_blk, z[:, _T_KSEL + g * LANES:_T_KSEL + (g + 1) * LANES]).astype(BF16)
        kwa_ref[g] = jnp.where(
            (lane == NSA_D) | (lane == NSA_D + 1), pos_lo_hi,
            z[:, _T_KWIN + g * LANES:_T_KWIN + (g + 1) * LANES]).astype(BF16)
    for j in range(4):
        cmp_ref[j] = z[:, _T_CMP + j * NSA_D:_T_CMP + (j + 1) * NSA_D]
    cqn = _rms(z[:, _T_CQ:_T_CKV], qg_ref[...]).astype(BF16)
    q2t = _dot_nt(wq2_ref[...], cqn)
    cqt, snt = cqt_ref[...], snt_ref[...]
    nq = MLA_HEADS * LANES
    scale = (MLA_NOPE + MLA_ROPE) ** -0.5 * LOG2E
    for hh in range(MLA_HEADS):
        qa = q2t[hh * LANES:(hh + 1) * LANES]
        qs = q2t[nq + hh * LANES:nq + (hh + 1) * LANES]
        qmt_ref[hh * LANES:(hh + 1) * LANES] = ((qa * cqt + qs * snt) * scale).astype(BF16)
    ckvn = _rms(z[:, _T_CKV:_T_KPE], kvg_ref[...]).astype(BF16)
    k2 = _dot(ckvn, wk2_ref[...])
    krot = z[:, _T_KPE:_T_KPE + LANES] * ck_ref[...] + z[:, _T_KPE + LANES:_T_END] * sn_ref[...]
    for hh in range(MLA_HEADS):
        km_ref[hh] = (k2[:, hh * LANES:(hh + 1) * LANES] + krot).astype(BF16)
    vmt_ref[...] = _dot_nt(wv2_ref[...], ckvn).astype(BF16)


def _inproj(x1, mix_pre_g, weights, q_norm_g, kv_norm_g, tables, tm):
    b, s, d = x1.shape
    w_tok, w_feat, w_q2t, w_k2, w_v2t = weights
    ck, sn, cqt, snt = tables
    tok = lambda w: pl.BlockSpec((None, tm, w), lambda bi, i: (bi, i, 0))
    feat = lambda r: pl.BlockSpec((None, r, tm), lambda bi, i: (bi, 0, i))
    heads = lambda n, w: pl.BlockSpec((None, n, tm, w), lambda bi, i: (bi, 0, i, 0))
    tab = pl.BlockSpec((tm, LANES), lambda bi, i: (i, 0))
    tabt = pl.BlockSpec((LANES, tm), lambda bi, i: (0, i))
    sds = jax.ShapeDtypeStruct
    return pl.pallas_call(
        functools.partial(_inproj_kernel, tm=tm),
        grid=(b, s // tm),
        in_specs=[tok(d), _const_spec((1, d)), _const_spec(w_tok.shape), _const_spec(w_feat.shape),
                  _const_spec((1, MLA_Q_RANK)), _const_spec(w_q2t.shape),
                  _const_spec((1, MLA_KV_RANK)), _const_spec(w_k2.shape), _const_spec(w_v2t.shape),
                  tab, tab, tabt, tabt],
        out_specs=[feat(NSA_HEADS * LANES), heads(NSA_GROUPS, 2 * LANES), heads(NSA_GROUPS, LANES),
                   feat(LANES), feat(LANES), heads(4, NSA_D), feat(3 * NSA_HEADS),
                   feat(MLA_HEADS * LANES), heads(MLA_HEADS, LANES), feat(MLA_HEADS * MLA_V)],
        out_shape=[sds((b, NSA_HEADS * LANES, s), BF16), sds((b, NSA_GROUPS, s, 2 * LANES), BF16),
                   sds((b, NSA_GROUPS, s, LANES), BF16), sds((b, LANES, s), BF16), sds((b, LANES, s), BF16),
                   sds((b, 4, s, NSA_D), F32), sds((b, 3 * NSA_HEADS, s), F32),
                   sds((b, MLA_HEADS * LANES, s), BF16), sds((b, MLA_HEADS, s, LANES), BF16),
                   sds((b, MLA_HEADS * MLA_V, s), BF16)],
        compiler_params=_params("parallel", "parallel"),
        name="inproj",
    )(x1, mix_pre_g.reshape(1, d), w_tok, w_feat, q_norm_g.reshape(1, -1), w_q2t,
      kv_norm_g.reshape(1, -1), w_k2, w_v2t, ck, sn, cqt, snt)


def _compress_kernel(c_ref, pos_ref, w1_ref, w2k_ref, w2vt_ref, kc_ref, vct_ref):
    nc = c_ref.shape[1]
    lane = lax.broadcasted_iota(jnp.int32, (nc, NSA_D), 1)
    blk = lax.broadcasted_iota(jnp.int32, (nc, NSA_D), 0)
    end_cols = jnp.where(lane == 0, lax.shift_right_logical(blk, END_SHIFT),
                         jnp.where(lane == 1, blk & ((1 << END_SHIFT) - 1), 0)).astype(F32)
    for j in range(4):
        kv, g = divmod(j, NSA_GROUPS)
        c = c_ref[j]
        a0 = _dot((c + pos_ref[kv, 0]).astype(BF16), w1_ref[kv, 0])
        a1 = _dot((c + pos_ref[kv, 1]).astype(BF16), w1_ref[kv, 1])
        pre = a0 + pltpu.roll(a1, nc - 1, 0)
        hid = jax.nn.gelu(pre).astype(BF16)
        if kv == 0:
            kc_ref[g] = jnp.concatenate([_dot(hid, w2k_ref[...]), end_cols], axis=1).astype(BF16)
        else:
            vct_ref[g * NSA_D:(g + 1) * NSA_D] = _dot_nt(w2vt_ref[...], hid).astype(BF16)


def _compress(cmp_in, pos_k, w1_k, w2_k, pos_v, w1_v, w2_v):
    b, _, s, dk = cmp_in.shape
    nc = s // CMP_STRIDE
    kdim = CMP_STRIDE * dk
    c4 = cmp_in.reshape(b, 4, nc, kdim)
    pos = jnp.stack([pos_k.reshape(2, 1, kdim), pos_v.reshape(2, 1, kdim)])
    w1 = jnp.stack([w1_k.reshape(2, kdim, CMP_HID), w1_v.reshape(2, kdim, CMP_HID)]).astype(BF16)
    return pl.pallas_call(
        _compress_kernel,
        grid=(b,),
        in_specs=[pl.BlockSpec((None, 4, nc, kdim), lambda bi: (bi, 0, 0, 0)),
                  _const_spec(pos.shape), _const_spec(w1.shape),
                  _const_spec((CMP_HID, dk)), _const_spec((dk, CMP_HID))],
        out_specs=[pl.BlockSpec((None, NSA_GROUPS, nc, LANES), lambda bi: (bi, 0, 0, 0)),
                   pl.BlockSpec((None, LANES, nc), lambda bi: (bi, 0, 0))],
        out_shape=[jax.ShapeDtypeStruct((b, NSA_GROUPS, nc, LANES), BF16),
                   jax.ShapeDtypeStruct((b, LANES, nc), BF16)],
        compiler_params=_params("parallel"),
        name="compress",
    )(c4, pos, w1, w2_k.astype(BF16), w2_v.T.astype(BF16))


def _nsa_cmp_kernel(qt_ref, kc_ref, vct_ref, ovt_ref, pool_ref, oct_ref, sbt_ref, act_ref, *, tq, tq_main):
    nc = kc_ref.shape[1]
    t = pl.program_id(1) * tq + lax.broadcasted_iota(jnp.int32, (1, tq), 1)
    cmp_end = lax.broadcasted_iota(jnp.int32, (nc, 1), 0) * CMP_STRIDE + (CMP_LEN - 1)
    valid = cmp_end <= t
    any_valid = t >= CMP_LEN - 1
    slopes = _alibi_slopes()
    row = lax.broadcasted_iota(jnp.int32, (NSA_D, tq), 0)
    blk = lax.broadcasted_iota(jnp.int32, (LANES, tq), 0)
    blkf = blk.astype(F32)
    cur = lax.shift_right_logical(t, SEL_SHIFT)
    forced = (blk == 0) | (blk == cur) | (blk == cur - 1)
    for g in range(NSA_GROUPS):
        vo = jnp.concatenate([vct_ref[g * NSA_D:(g + 1) * NSA_D], ovt_ref[...]], axis=0)
        imp = jnp.zeros((LANES, tq), F32)
        for hh in range(NSA_HPG):
            h = g * NSA_HPG + hh
            hi_w = slopes[h] * (CMP_STRIDE << END_SHIFT)
            tail = jnp.where(row == 0, hi_w, jnp.where(row == 1, slopes[h] * CMP_STRIDE, 0.0)).astype(BF16)
            q = jnp.concatenate([qt_ref[h * LANES + g * NSA_D:h * LANES + (g + 1) * NSA_D], tail], axis=0)
            s = jnp.where(valid, _dot(kc_ref[g], q), NEG)
            e = jnp.exp(s - jnp.max(s, axis=0, keepdims=True))
            norm = jnp.where(any_valid, 1.0 / jnp.maximum(jnp.sum(e, axis=0, keepdims=True), 1e-30), 0.0)
            r = _dot(vo, e.astype(BF16)) * norm
            oct_ref[h * NSA_D:(h + 1) * NSA_D] = r[:NSA_D]
            imp = imp + r[NSA_D:]
        work = jnp.where(blk <= cur, jnp.where(forced, FORCE_SCORE, imp), NEG)
        chosen = jnp.zeros((LANES, tq), jnp.bool_)
        for _ in range(SEL_TOPK):
            top = jnp.max(work, axis=0, keepdims=True)
            idx = jnp.min(jnp.where(work == top, blkf, float(LANES)), axis=0, keepdims=True)
            hit = blkf == idx
            chosen = chosen | hit
            work = jnp.where(hit, -jnp.inf, work)
        sbt_ref[g] = jnp.where(chosen, 0.0, NEG).astype(BF16)
        used = _dot(pool_ref[...], chosen.astype(BF16))
        for j in range(tq // tq_main):
            any_q = jnp.max(used[:, j * tq_main:(j + 1) * tq_main], axis=1, keepdims=True)
            act_ref[g, j] = jnp.broadcast_to(any_q, act_ref.shape[2:])


def _nsa_cmp(qt, kc, vct, overlap_t, pool, tq, tq_main):
    b, _, s = qt.shape
    nc = kc.shape[2]
    nkt = pool.shape[0]
    return pl.pallas_call(
        functools.partial(_nsa_cmp_kernel, tq=tq, tq_main=tq_main),
        grid=(b, s // tq),
        in_specs=[pl.BlockSpec((None, NSA_HEADS * LANES, tq), lambda bi, i: (bi, 0, i)),
                  pl.BlockSpec((None, NSA_GROUPS, nc, LANES), lambda bi, i: (bi, 0, 0, 0)),
                  pl.BlockSpec((None, LANES, nc), lambda bi, i: (bi, 0, 0)),
                  _const_spec(overlap_t.shape), _const_spec(pool.shape)],
        out_specs=[pl.BlockSpec((None, NSA_HEADS * NSA_D, tq), lambda bi, i: (bi, 0, i)),
                   pl.BlockSpec((None, NSA_GROUPS, LANES, tq), lambda bi, i: (bi, 0, 0, i)),
                   pl.BlockSpec((None, NSA_GROUPS, tq // tq_main, nkt, LANES), lambda bi, i: (bi, 0, i, 0, 0))],
        out_shape=[jax.ShapeDtypeStruct((b, NSA_HEADS * NSA_D, s), F32),
                   jax.ShapeDtypeStruct((b, NSA_GROUPS, LANES, s), BF16),
                   jax.ShapeDtypeStruct((b, NSA_GROUPS, s // tq_main, nkt, LANES), F32)],
        compiler_params=_params("parallel", "parallel"),
        name="nsa_cmp",
    )(qt, kc, vct, overlap_t, pool)


def _tile_pool_matrix(seq, tk_sel):
    per_tile = tk_sel // SEL_LEN
    nkt = -(-(seq // tk_sel) // 16) * 16
    pool = np.zeros((nkt, LANES), np.float32)
    for c in range(seq // SEL_LEN):
        pool[c // per_tile, c] = 1.0
    return jnp.asarray(pool, BF16)


def _overlap_matrix_t(seq):
    n_c = (seq - CMP_LEN) // CMP_STRIDE + 1
    n_sel = seq // SEL_LEN
    c0 = np.arange(n_c) * CMP_STRIDE
    s0 = np.arange(n_sel) * SEL_LEN
    ov = np.clip(np.minimum(c0[:, None] + CMP_LEN, s0[None, :] + SEL_LEN)
                 - np.maximum(c0[:, None], s0[None, :]), 0, None) / CMP_LEN
    full = np.zeros((LANES, seq // CMP_STRIDE), np.float32)
    full[:n_sel, :n_c] = ov.T
    return jnp.asarray(full, BF16)


def _flash_init(dv, nq):
    return (jnp.full((1, nq), NEG, F32), jnp.zeros((1, nq), F32), jnp.zeros((dv, nq), F32))


def _flash_pipelined(n_tiles, first_scores, scores, values, dv, s_buf, p_buf, exp=jnp.exp):
    chains = range(len(first_scores))
    nq = first_scores[0].shape[1]
    state = tuple(_flash_init(dv, nq) for _ in chains)
    s0 = first_scores
    for c in chains:
        s_buf[0, c] = s0[c]
        p_buf[1, c] = jnp.zeros(p_buf.shape[2:], BF16)
    colmax = tuple(jnp.max(s0[c], axis=0, keepdims=True) for c in chains)
    ones = tuple(jnp.ones((1, nq), F32) for _ in chains)

    def step(i, carry, cur):
        st, alphas, cmax = carry
        nxt = 1 - cur
        vts = values(jnp.maximum(i - 1, 0))
        s_next = scores(i + 1)
        new_st, new_alpha, new_cmax = [], [], []
        for c in chains:
            m, l, acc = st[c]
            acc = alphas[c] * acc + _dot(vts[c], p_buf[nxt, c])
            m_new = jnp.maximum(m, cmax[c])
            alpha = exp(m - m_new)
            p = exp(s_buf[cur, c] - m_new)
            l = alpha * l + jnp.sum(p, axis=0, keepdims=True)
            p_buf[cur, c] = p.astype(BF16)
            s_buf[nxt, c] = s_next[c]
            new_st.append((m_new, l, acc))
            new_alpha.append(alpha)
            new_cmax.append(jnp.max(s_next[c], axis=0, keepdims=True))
        return tuple(new_st), tuple(new_alpha), tuple(new_cmax)

    def steps(first, count, carry):
        for u in range(count):
            carry = step(first + u, carry, u % 2)
        return carry

    n_pairs = n_tiles // 2
    carry = lax.fori_loop(0, n_pairs, lambda j, c: steps(2 * j, 2, c), (tuple(state), ones, colmax))
    st, alphas, _ = lax.fori_loop(2 * n_pairs, n_tiles, lambda i, c: steps(i, 1, c), carry)
    vts = values(jnp.maximum(n_tiles - 1, 0))
    last = (n_tiles + 1) & 1
    return tuple((m, l, alphas[c] * acc + _dot(vts[c], p_buf[last, c]))
                 for c, (m, l, acc) in enumerate(st))


def _nsa_main_kernel(act_ref, qt_ref, sbt_ref, oct_ref, gt_ref, ksa_ref, kwa_ref, vst_ref, vwt_ref, y_ref,
                     s_buf, p_buf, tiles_ref, *, tq, tk_sel, tk_win, act_tiles):
    t0 = pl.program_id(1) * tq
    nq = NSA_HPG * tq
    slopes = _alibi_slopes()
    t_lane = jnp.concatenate([t0 + lax.broadcasted_iota(jnp.int32, (1, tq), 1)] * NSA_HPG, axis=1)
    row = lax.broadcasted_iota(jnp.int32, (NSA_D, nq), 0)
    blk_rel = (lax.broadcasted_iota(jnp.int32, (LANES, tq), 0)
               - lax.shift_right_logical(t0, SEL_SHIFT)).astype(F32)

    q_sel, q_win, slope_l = [], [], []
    for g in range(NSA_GROUPS):
        heads = range(g * NSA_HPG, (g + 1) * NSA_HPG)
        sl = jnp.concatenate([jnp.full((1, tq), slopes[h], F32) for h in heads], axis=1)
        qg = jnp.concatenate(
            [qt_ref[h * LANES + g * NSA_D:h * LANES + (g + 1) * NSA_D] for h in heads], axis=1)
        tail = jnp.where(row == 0, sl, 0.0).astype(BF16)
        tail_win = jnp.where(row == 0, sl, jnp.where(row == 1, sl * LANES, 0.0)).astype(BF16)
        sb = sbt_ref[g].astype(F32)
        bias = jnp.concatenate([sb + (slopes[h] * SEL_LEN) * blk_rel for h in heads], axis=1)
        q_sel.append(jnp.concatenate([bias.astype(BF16), qg, tail], axis=0))
        q_win.append(jnp.concatenate([qg, tail_win], axis=0))

    groups = range(NSA_GROUPS)

    def sel_scores(kt, g):
        k0 = pl.multiple_of(kt * tk_sel, tk_sel)
        return _dot(ksa_ref[g, pl.ds(k0, tk_sel), :], q_sel[g])

    def sel_values(kt, g):
        k0 = pl.multiple_of(kt * tk_sel, tk_sel)
        return vst_ref[g * NSA_D:(g + 1) * NSA_D, pl.ds(k0, tk_sel)]

    n_full = (t0 + 1) // tk_sel
    n_kt = tiles_ref.shape[1]
    counts, spare = [], []
    for g in groups:
        base = ((pl.program_id(0) * NSA_GROUPS + g) * pl.num_programs(1) + pl.program_id(1)) * act_tiles
        for k in range(n_kt):
            tiles_ref[g, k] = 0

        def scan(kt, carry, g=g, base=base):
            cnt, unused = carry
            used = act_ref[base + kt]
            tiles_ref[g, cnt] = kt
            return cnt + used, jnp.where(used == 0, kt, unused)

        cnt, unused = lax.fori_loop(0, n_full, scan, (jnp.int32(0), jnp.int32(0)))
        counts.append(cnt)
        spare.append(unused)

    def tile_of(k, g):
        return jnp.where(k < counts[g], tiles_ref[g, jnp.minimum(k, n_kt - 1)], spare[g])

    diag = t0 // tk_sel
    causal = diag * tk_sel + lax.broadcasted_iota(jnp.int32, (tk_sel, 1), 0) <= t_lane
    at = lambda k, g: jnp.where(k == 0, diag, tile_of(jnp.maximum(k - 1, 0), g))
    sel = _flash_pipelined(jnp.maximum(counts[0], counts[1]) + 1,
                           [jnp.where(causal, sel_scores(diag, g), NEG) for g in groups],
                           lambda k: [sel_scores(tile_of(k - 1, g), g) for g in groups],
                           lambda k: [sel_values(at(k, g), g) for g in groups], NSA_D, s_buf, p_buf)

    start = pl.multiple_of(jnp.maximum(t0 - WINDOW, 0), tk_win)
    n_win = WINDOW + tq
    key = start + lax.broadcasted_iota(jnp.int32, (n_win, 1), 0)
    edge = lambda k: (k <= t_lane) & (k > t_lane - WINDOW)
    in_window = jnp.concatenate([edge(key[:tq]), key[tq:n_win - tq] <= t_lane, edge(key[n_win - tq:])], axis=0)
    win = []
    for g in groups:
        s = _dot(kwa_ref[g, pl.ds(start, n_win), :], q_win[g])
        s = jnp.where(in_window, s, NEG)
        p = jnp.exp(s - jnp.max(s, axis=0, keepdims=True))
        win.append((None, jnp.sum(p, axis=0, keepdims=True),
                    _dot(vwt_ref[g * NSA_D:(g + 1) * NSA_D, pl.ds(start, n_win)], p.astype(BF16))))

    gates = gt_ref[...]
    outs = []
    for h in range(NSA_HEADS):
        g, hh = divmod(h, NSA_HPG)
        cols = slice(hh * tq, (hh + 1) * tq)
        o_s = sel[g][2][:, cols] * (1.0 / sel[g][1][:, cols])
        o_w = win[g][2][:, cols] * (1.0 / win[g][1][:, cols])
        outs.append(gates[3 * h:3 * h + 1] * oct_ref[h * NSA_D:(h + 1) * NSA_D]
                    + gates[3 * h + 1:3 * h + 2] * o_s + gates[3 * h + 2:3 * h + 3] * o_w)
    y_ref[...] = jnp.concatenate(outs, axis=0).T.astype(BF16)


def _nsa_main(act, qt, sbt, oct, gt, ksa, kwa, vst, vwt, tq, tk_sel, tk_win):
    b, _, s = qt.shape
    act_tiles = act.shape[0] // (b * NSA_GROUPS * (s // tq))
    feat = lambda r: pl.BlockSpec((None, r, tq), lambda bi, i, _: (bi, 0, i))
    grid_spec = pltpu.PrefetchScalarGridSpec(
        num_scalar_prefetch=1,
        grid=(b, s // tq),
        in_specs=[feat(NSA_HEADS * LANES),
                  pl.BlockSpec((None, NSA_GROUPS, LANES, tq), lambda bi, i, _: (bi, 0, 0, i)),
                  feat(NSA_HEADS * NSA_D), feat(3 * NSA_HEADS),
                  pl.BlockSpec((None, NSA_GROUPS, s, 2 * LANES), lambda bi, i, _: (bi, 0, 0, 0)),
                  pl.BlockSpec((None, NSA_GROUPS, s, LANES), lambda bi, i, _: (bi, 0, 0, 0)),
                  pl.BlockSpec((None, LANES, s), lambda bi, i, _: (bi, 0, 0)),
                  pl.BlockSpec((None, LANES, s), lambda bi, i, _: (bi, 0, 0))],
        out_specs=pl.BlockSpec((None, tq, NSA_HEADS * NSA_D), lambda bi, i, _: (bi, i, 0)),
        scratch_shapes=[pltpu.VMEM((2, NSA_GROUPS, tk_sel, NSA_HPG * tq), F32),
                        pltpu.VMEM((2, NSA_GROUPS, tk_sel, NSA_HPG * tq), BF16),
                        pltpu.SMEM((NSA_GROUPS, act_tiles), jnp.int32)])
    return pl.pallas_call(
        functools.partial(_nsa_main_kernel, tq=tq, tk_sel=tk_sel, tk_win=tk_win, act_tiles=act_tiles),
        grid_spec=grid_spec,
        out_shape=jax.ShapeDtypeStruct((b, s, NSA_HEADS * NSA_D), BF16),
        compiler_params=_params("parallel", "arbitrary"),
        name="nsa_main",
    )(act, qt, sbt, oct, gt, ksa, kwa, vst, vwt)


def _mla_kernel(qt_ref, k_ref, vt_ref, y_ref, s_buf, p_buf, *, tq, tk):
    t0 = pl.program_id(2) * tq
    t_lane = t0 + lax.broadcasted_iota(jnp.int32, (1, tq), 1)
    heads = range(2)
    qts = [qt_ref[hh * LANES:(hh + 1) * LANES] for hh in heads]

    def scores(kt):
        k0 = pl.multiple_of(kt * tk, tk)
        return [_dot(k_ref[hh, pl.ds(k0, tk), :], qts[hh]) for hh in heads]

    def values(kt):
        k0 = pl.multiple_of(kt * tk, tk)
        return [vt_ref[hh * MLA_V:(hh + 1) * MLA_V, pl.ds(k0, tk)] for hh in heads]

    assert tk == tq
    diag = pl.program_id(2)
    causal = t0 + lax.broadcasted_iota(jnp.int32, (tk, 1), 0) <= t_lane
    st = _flash_pipelined(diag + 1, [jnp.where(causal, s, NEG) for s in scores(diag)],
                          lambda k: scores(k - 1),
                          lambda k: values(jnp.where(k == 0, diag, jnp.maximum(k - 1, 0))),
                          MLA_V, s_buf, p_buf, exp=jnp.exp2)
    o = jnp.concatenate([st[hh][2] * (1.0 / st[hh][1]) for hh in heads], axis=0)
    y_ref[...] = o.T.astype(BF16)


def _mla(qmt, km, vmt, tq, tk):
    b, h, s, _ = km.shape
    k5 = km.reshape(b, h // 2, 2, s, LANES)
    return pl.pallas_call(
        functools.partial(_mla_kernel, tq=tq, tk=tk),
        grid=(b, h // 2, s // tq),
        in_specs=[pl.BlockSpec((None, 2 * LANES, tq), lambda bi, hp, i: (bi, hp, i)),
                  pl.BlockSpec((None, None, 2, s, LANES), lambda bi, hp, i: (bi, hp, 0, 0, 0)),
                  pl.BlockSpec((None, 2 * MLA_V, s), lambda bi, hp, i: (bi, hp, 0))],
        out_specs=pl.BlockSpec((None, tq, 2 * MLA_V), lambda bi, hp, i: (bi, i, hp)),
        out_shape=jax.ShapeDtypeStruct((b, s, h * MLA_V), BF16),
        scratch_shapes=[pltpu.VMEM((2, 2, tk, tq), F32), pltpu.VMEM((2, 2, tk, tq), BF16)],
        compiler_params=_params("parallel", "parallel", "arbitrary"),
        name="mla",
    )(qmt, k5, vmt)


def _merge_kernel(x_ref, yn_ref, ym_ref, pre_ref, post_ref, wgm_ref, wpn_ref, wpm_ref, wo_ref, o_ref):
    x = x_ref[...]
    d = x.shape[1]
    h = _rms(x, pre_ref[...]).astype(BF16)
    gm = jax.nn.sigmoid(_dot(h, wgm_ref[...]))
    merged = gm[:, :d] * _dot(yn_ref[...], wpn_ref[...]) + gm[:, d:] * _dot(ym_ref[...], wpm_ref[...])
    y = _dot(merged.astype(BF16), wo_ref[...])
    o_ref[...] = x + _rms(y, post_ref[...])


def _merge(x1, y_nsa, y_mla, pre_g, post_g, w_gm, w_pn, w_pm, w_out, tm):
    n, d = x1.shape
    tok = lambda w: pl.BlockSpec((tm, w), lambda i: (i, 0))
    return pl.pallas_call(
        _merge_kernel,
        grid=(n // tm,),
        in_specs=[tok(d), tok(y_nsa.shape[1]), tok(y_mla.shape[1]), _const_spec((1, d)), _const_spec((1, d)),
                  _const_spec(w_gm.shape), _const_spec(w_pn.shape), _const_spec(w_pm.shape),
                  _const_spec(w_out.shape)],
        out_specs=tok(d),
        out_shape=jax.ShapeDtypeStruct((n, d), F32),
        compiler_params=_params("parallel"),
        name="merge",
    )(x1, y_nsa, y_mla, pre_g.reshape(1, d), post_g.reshape(1, d),
      w_gm.astype(BF16), w_pn.astype(BF16), w_pm.astype(BF16), w_out.astype(BF16))


def kernel(x, ff1_pre_g, ff1_post_g, ff1_w_gate, ff1_w_up, ff1_w_down, mix_pre_g, mix_post_g, w_in, cmp_pos_k, cmp_w1_k, cmp_w2_k, cmp_pos_v, cmp_w1_v, cmp_w2_v, mla_q_norm_g, mla_w_uq, mla_kv_norm_g, mla_w_ukv, w_proj_nsa, w_proj_mla, w_out, ff2_pre_g, ff2_post_g, ff2_w_gate, ff2_w_up, ff2_w_down):
    b, s, d = x.shape
    assert s % (SEL_LEN * SEL_TOPK) == 0 and s // SEL_LEN <= LANES
    tl = _tiles(s)
    n = b * s

    x1 = _ffn(x.reshape(n, d), ff1_pre_g, ff1_post_g, ff1_w_gate, ff1_w_up, ff1_w_down, tl["tm_ffn"])

    (qt, ksa, kwa, vst, vwt, cmp_in, gt, qmt, km, vmt) = _inproj(
        x1.reshape(b, s, d), mix_pre_g, _inproj_weights(w_in, mla_w_uq, mla_w_ukv),
        mla_q_norm_g, mla_kv_norm_g, _rope_tables(s), tl["tm"])

    kc, vct = _compress(cmp_in, cmp_pos_k, cmp_w1_k, cmp_w2_k, cmp_pos_v, cmp_w1_v, cmp_w2_v)
    oct, sbt, act = _nsa_cmp(qt, kc, vct, _overlap_matrix_t(s), _tile_pool_matrix(s, tl["tk_sel"]),
                             tl["tq_cmp"], tl["tq_nsa"])
    act = (act[..., 0] > 0).astype(jnp.int32).reshape(-1)
    y_nsa = _nsa_main(act, qt, sbt, oct, gt, ksa, kwa, vst, vwt, tl["tq_nsa"], tl["tk_sel"], tl["tk_win"])
    y_mla = _mla(qmt, km, vmt, tl["tq_mla"], tl["tq_mla"])

    o_gm = sum((NSA_HEADS * NSA_D, 6 * NSA_GROUPS * NSA_D, 3 * NSA_HEADS, MLA_Q_RANK, MLA_KV_RANK, MLA_ROPE))
    x2 = _merge(x1, y_nsa.reshape(n, -1), y_mla.reshape(n, -1), mix_pre_g, mix_post_g,
                w_in[:, o_gm:], w_proj_nsa, w_proj_mla, w_out, tl["tm"])
    x3 = _ffn(x2, ff2_pre_g, ff2_post_g, ff2_w_gate, ff2_w_up, ff2_w_down, tl["tm_ffn"])
    return x3.reshape(b, s, d)
```

```python
import functools

import numpy as np
import jax
import jax.numpy as jnp
from jax import lax
from jax.experimental import pallas as pl
from jax.experimental.pallas import tpu as pltpu

F32 = jnp.float32
BF16 = jnp.bfloat16

EPS = 1e-6
NEG = -1e30
FORCE_SCORE = 1e4
NSA_HEADS = 8
NSA_GROUPS = 2
NSA_HPG = NSA_HEADS // NSA_GROUPS
NSA_D = 64
CMP_LEN = 32
CMP_STRIDE = 16
CMP_HID = 256
END_SHIFT = 4
SEL_LEN = 64
SEL_SHIFT = 6
SEL_TOPK = 16
WINDOW = 512
MLA_HEADS = 8
MLA_NOPE = 64
MLA_ROPE = 32
MLA_V = 64
MLA_Q_RANK = 256
MLA_KV_RANK = 128
ROPE_THETA = 10000.0
LANES = 128
POS_SHIFT = 7
LOG2E = 1.4426950408889634
VMEM_LIMIT = 56 * 1024 * 1024
FFN_CHUNK = 256
V_EXTRA = 16


def _tiles(seq):
    return dict(
        tm=min(1024, seq),
        tm_ffn=min(1024, seq),
        tq_cmp=min(1024, seq),
        tq_nsa=min(256, seq),
        tk_sel=min(256, seq),
        tk_win=LANES,
        tq_mla=min(512, seq),
    )


def _params(*sem):
    return pltpu.CompilerParams(dimension_semantics=sem, vmem_limit_bytes=VMEM_LIMIT)


def _const_spec(shape):
    nd = len(shape)
    return pl.BlockSpec(shape, lambda *_: (0,) * nd, pipeline_mode=pl.Buffered(1))


def _rms(x, g):
    return x * lax.rsqrt(jnp.mean(x * x, axis=-1, keepdims=True) + EPS) * g


def _dot(a, b):
    return jnp.dot(a, b, preferred_element_type=F32)


def _dot_nt(a, b):
    return lax.dot_general(a, b, (((1,), (1,)), ((), ())), preferred_element_type=F32)


def _alibi_slopes():
    return [float(2.0 ** (-8.0 * (i + 1) / NSA_HEADS)) for i in range(NSA_HEADS)]


def _ffn_kernel(x_ref, pre_ref, post_ref, wg_ref, wu_ref, wd_ref, o_ref, *, fc):
    x = x_ref[...]
    h = _rms(x, pre_ref[...]).astype(BF16)
    acc = jnp.zeros(x.shape, F32)
    for c in range(wg_ref.shape[1] // fc):
        gate = _dot(h, wg_ref[:, c * fc:(c + 1) * fc])
        up = _dot(h, wu_ref[:, c * fc:(c + 1) * fc])
        act = (gate * jax.nn.sigmoid(gate) * up).astype(BF16)
        acc = acc + _dot(act, wd_ref[c * fc:(c + 1) * fc, :])
    o_ref[...] = x + 0.5 * _rms(acc, post_ref[...])


def _ffn(x2d, pre_g, post_g, w_gate, w_up, w_down, tm):
    n, d = x2d.shape
    f = w_gate.shape[1]
    fc = FFN_CHUNK if f % FFN_CHUNK == 0 else f
    return pl.pallas_call(
        functools.partial(_ffn_kernel, fc=fc),
        grid=(n // tm,),
        in_specs=[pl.BlockSpec((tm, d), lambda i: (i, 0)),
                  _const_spec((1, d)), _const_spec((1, d)),
                  _const_spec((d, f)), _const_spec((d, f)), _const_spec((f, d))],
        out_specs=pl.BlockSpec((tm, d), lambda i: (i, 0)),
        out_shape=jax.ShapeDtypeStruct((n, d), F32),
        compiler_params=_params("parallel"),
        name="ffn",
    )(x2d, pre_g.reshape(1, d), post_g.reshape(1, d),
      w_gate.astype(BF16), w_up.astype(BF16), w_down.astype(BF16))


_T_KSEL = 0
_T_KWIN = _T_KSEL + 2 * LANES
_T_CMP = _T_KWIN + 2 * LANES
_T_CQ = _T_CMP + 2 * LANES
_T_CKV = _T_CQ + MLA_Q_RANK
_T_KPE = _T_CKV + MLA_KV_RANK
_T_END = _T_KPE + 2 * LANES
_F_QN = 0
_F_VSEL = _F_QN + NSA_HEADS * LANES
_F_VWIN = _F_VSEL + LANES
_F_GATE = _F_VWIN + LANES
_F_END = _F_GATE + 32


def _inproj_weights(w_in, w_uq, w_ukv):
    d = w_in.shape[0]
    o_q = 0
    o_kv = o_q + NSA_HEADS * NSA_D
    o_g = o_kv + 6 * NSA_GROUPS * NSA_D
    o_cq = o_g + 3 * NSA_HEADS
    o_ckv = o_cq + MLA_Q_RANK
    o_kpe = o_ckv + MLA_KV_RANK
    zeros = lambda n: jnp.zeros((d, n), w_in.dtype)
    kv = lambda j, g: w_in[:, o_kv + (j * NSA_GROUPS + g) * NSA_D:o_kv + (j * NSA_GROUPS + g + 1) * NSA_D]
    half = MLA_ROPE // 2
    kp1 = w_in[:, o_kpe:o_kpe + half]
    kp2 = w_in[:, o_kpe + half:o_kpe + MLA_ROPE]
    tail = LANES - MLA_NOPE - MLA_ROPE
    cols = [kv(2, 0), zeros(NSA_D), kv(2, 1), zeros(NSA_D),
            kv(4, 0), zeros(NSA_D), kv(4, 1), zeros(NSA_D),
            kv(0, 0), kv(0, 1), kv(1, 0), kv(1, 1),
            w_in[:, o_cq:o_cq + MLA_Q_RANK], w_in[:, o_ckv:o_ckv + MLA_KV_RANK],
            zeros(MLA_NOPE), kp1, kp2, zeros(tail), zeros(MLA_NOPE), kp2, kp1, zeros(tail)]
    w_tok = jnp.concatenate(cols, axis=1).astype(BF16)
    assert w_tok.shape[1] == _T_END
    rows = []
    for h in range(NSA_HEADS):
        wq = w_in[:, o_q + h * NSA_D:o_q + (h + 1) * NSA_D]
        rows += [wq, zeros(NSA_D)] if h // NSA_HPG == 0 else [zeros(NSA_D), wq]
    rows += [kv(3, 0), kv(3, 1), kv(5, 0), kv(5, 1)]
    rows += [w_in[:, o_g:o_g + 3 * NSA_HEADS], zeros(_F_END - _F_GATE - 3 * NSA_HEADS)]
    w_feat = jnp.concatenate(rows, axis=1).T.astype(BF16)
    assert w_feat.shape[0] == _F_END

    dq = MLA_NOPE + MLA_ROPE
    zq = lambda n: jnp.zeros((w_uq.shape[0], n), w_uq.dtype)
    qa, qs = [], []
    for h in range(MLA_HEADS):
        nope = w_uq[:, h * dq:h * dq + MLA_NOPE]
        r1 = w_uq[:, h * dq + MLA_NOPE:h * dq + MLA_NOPE + half]
        r2 = w_uq[:, h * dq + MLA_NOPE + half:(h + 1) * dq]
        qa += [nope, r1, r2, zq(tail)]
        qs += [zq(MLA_NOPE), r2, r1, zq(tail)]
    w_q2t = jnp.concatenate(qa + qs, axis=1).T.astype(BF16)

    dkv = MLA_NOPE + MLA_V
    zk = lambda n: jnp.zeros((w_ukv.shape[0], n), w_ukv.dtype)
    ka, va = [], []
    for h in range(MLA_HEADS):
        ka += [w_ukv[:, h * dkv:h * dkv + MLA_NOPE], zk(LANES - MLA_NOPE)]
        va += [w_ukv[:, h * dkv + MLA_NOPE:(h + 1) * dkv]]
    w_k2 = jnp.concatenate(ka, axis=1).astype(BF16)
    w_v2t = jnp.concatenate(va, axis=1).T.astype(BF16)
    return w_tok, w_feat, w_q2t, w_k2, w_v2t


def _rope_tables(seq):
    half = MLA_ROPE // 2
    freqs = jnp.asarray(ROPE_THETA ** (-np.arange(half, dtype=np.float32) / half), F32)
    ang = jnp.arange(seq).astype(F32)[:, None] * freqs[None, :]
    cos, sin = jnp.cos(ang), jnp.sin(ang)
    pad = jnp.zeros((seq, LANES - MLA_NOPE - MLA_ROPE), F32)
    cq = jnp.concatenate([jnp.ones((seq, MLA_NOPE), F32), cos, cos, pad], axis=1)
    ck = jnp.concatenate([jnp.zeros((seq, MLA_NOPE), F32), cos, cos, pad], axis=1)
    sn = jnp.concatenate([jnp.zeros((seq, MLA_NOPE), F32), -sin, sin, pad], axis=1)
    return ck, sn, cq.T, sn.T


def _inproj_kernel(x_ref, g_ref, wt_ref, wf_ref, qg_ref, wq2_ref, kvg_ref, wk2_ref, wv2_ref,
                   ck_ref, sn_ref, cqt_ref, snt_ref,
                   qt_ref, ksa_ref, kwa_ref, vst_ref, vwt_ref, cmp_ref, gt_ref, qmt_ref, km_ref, vmt_ref,
                   *, tm):
    h = _rms(x_ref[...], g_ref[...]).astype(BF16)
    z = _dot(h, wt_ref[...])
    zt = _dot_nt(wf_ref[...], h)
    qt_ref[...] = (zt[_F_QN:_F_VSEL] * NSA_D ** -0.5).astype(BF16)
    ones_rows = (lax.broadcasted_iota(jnp.int32, (V_EXTRA, tm), 0) == 0).astype(BF16)
    for g in range(NSA_GROUPS):
        for ref, base in ((vst_ref, _F_VSEL), (vwt_ref, _F_VWIN)):
            ref[g, :NSA_D] = zt[base + g * NSA_D:base + (g + 1) * NSA_D].astype(BF16)
            ref[g, NSA_D:] = ones_rows
    gt_ref[...] = jax.nn.sigmoid(zt[_F_GATE:_F_GATE + 3 * NSA_HEADS])
    pos = pl.program_id(1) * tm + lax.broadcasted_iota(jnp.int32, (tm, LANES), 0)
    lane = lax.broadcasted_iota(jnp.int32, (tm, LANES), 1)
    onehot = (lane == lax.shift_right_logical(pos, SEL_SHIFT)).astype(BF16)
    in_blk = (pos & (SEL_LEN - 1)).astype(F32)
    pos_lo_hi = jnp.where(lane == NSA_D, pos & (LANES - 1), lax.shift_right_logical(pos, POS_SHIFT)).astype(F32)
    for g in range(NSA_GROUPS):
        ksa_ref[g, :, :LANES] = onehot
        ksa_ref[g, :, LANES:] = jnp.where(
            lane == NSA_D, in_blk, z[:, _T_KSEL + g * LANES:_T_KSEL + (g + 1) * LANES]).astype(BF16)
        kwa_ref[g] = jnp.where(
            (lane == NSA_D) | (lane == NSA_D + 1), pos_lo_hi,
            z[:, _T_KWIN + g * LANES:_T_KWIN + (g + 1) * LANES]).astype(BF16)
    for j in range(4):
        cmp_ref[j] = z[:, _T_CMP + j * NSA_D:_T_CMP + (j + 1) * NSA_D]
    cqn = _rms(z[:, _T_CQ:_T_CKV], qg_ref[...]).astype(BF16)
    q2t = _dot_nt(wq2_ref[...], cqn)
    cqt, snt = cqt_ref[...], snt_ref[...]
    nq = MLA_HEADS * LANES
    scale = (MLA_NOPE + MLA_ROPE) ** -0.5 * LOG2E
    for hh in range(MLA_HEADS):
        qa = q2t[hh * LANES:(hh + 1) * LANES]
        qs = q2t[nq + hh * LANES:nq + (hh + 1) * LANES]
        qmt_ref[hh * LANES:(hh + 1) * LANES] = ((qa * cqt + qs * snt) * scale).astype(BF16)
    ckvn = _rms(z[:, _T_CKV:_T_KPE], kvg_ref[...]).astype(BF16)
    k2 = _dot(ckvn, wk2_ref[...])
    krot = z[:, _T_KPE:_T_KPE + LANES] * ck_ref[...] + z[:, _T_KPE + LANES:_T_END] * sn_ref[...]
    for hh in range(MLA_HEADS):
        km_ref[hh] = (k2[:, hh * LANES:(hh + 1) * LANES] + krot).astype(BF16)
    vt = _dot_nt(wv2_ref[...], ckvn)
    for hh in range(MLA_HEADS):
        vmt_ref[hh, :MLA_V] = vt[hh * MLA_V:(hh + 1) * MLA_V].astype(BF16)
        vmt_ref[hh, MLA_V:] = ones_rows


def _inproj(x1, mix_pre_g, weights, q_norm_g, kv_norm_g, tables, tm):
    b, s, d = x1.shape
    w_tok, w_feat, w_q2t, w_k2, w_v2t = weights
    ck, sn, cqt, snt = tables
    tok = lambda w: pl.BlockSpec((None, tm, w), lambda bi, i: (bi, i, 0))
    feat = lambda r: pl.BlockSpec((None, r, tm), lambda bi, i: (bi, 0, i))
    heads = lambda n, w: pl.BlockSpec((None, n, tm, w), lambda bi, i: (bi, 0, i, 0))
    slabs = lambda n, r: pl.BlockSpec((None, n, r, tm), lambda bi, i: (bi, 0, 0, i))
    tab = pl.BlockSpec((tm, LANES), lambda bi, i: (i, 0))
    tabt = pl.BlockSpec((LANES, tm), lambda bi, i: (0, i))
    sds = jax.ShapeDtypeStruct
    return pl.pallas_call(
        functools.partial(_inproj_kernel, tm=tm),
        grid=(b, s // tm),
        in_specs=[tok(d), _const_spec((1, d)), _const_spec(w_tok.shape), _const_spec(w_feat.shape),
                  _const_spec((1, MLA_Q_RANK)), _const_spec(w_q2t.shape),
                  _const_spec((1, MLA_KV_RANK)), _const_spec(w_k2.shape), _const_spec(w_v2t.shape),
                  tab, tab, tabt, tabt],
        out_specs=[feat(NSA_HEADS * LANES), heads(NSA_GROUPS, 2 * LANES), heads(NSA_GROUPS, LANES),
                   slabs(NSA_GROUPS, NSA_D + V_EXTRA), slabs(NSA_GROUPS, NSA_D + V_EXTRA),
                   heads(4, NSA_D), feat(3 * NSA_HEADS),
                   feat(MLA_HEADS * LANES), heads(MLA_HEADS, LANES), slabs(MLA_HEADS, MLA_V + V_EXTRA)],
        out_shape=[sds((b, NSA_HEADS * LANES, s), BF16), sds((b, NSA_GROUPS, s, 2 * LANES), BF16),
                   sds((b, NSA_GROUPS, s, LANES), BF16),
                   sds((b, NSA_GROUPS, NSA_D + V_EXTRA, s), BF16), sds((b, NSA_GROUPS, NSA_D + V_EXTRA, s), BF16),
                   sds((b, 4, s, NSA_D), F32), sds((b, 3 * NSA_HEADS, s), F32),
                   sds((b, MLA_HEADS * LANES, s), BF16), sds((b, MLA_HEADS, s, LANES), BF16),
                   sds((b, MLA_HEADS, MLA_V + V_EXTRA, s), BF16)],
        compiler_params=_params("parallel", "parallel"),
        name="inproj",
    )(x1, mix_pre_g.reshape(1, d), w_tok, w_feat, q_norm_g.reshape(1, -1), w_q2t,
      kv_norm_g.reshape(1, -1), w_k2, w_v2t, ck, sn, cqt, snt)


def _compress_kernel(c_ref, pos_ref, w1_ref, w2k_ref, w2vt_ref, kc_ref, vct_ref):
    nc = c_ref.shape[1]
    lane = lax.broadcasted_iota(jnp.int32, (nc, NSA_D), 1)
    blk = lax.broadcasted_iota(jnp.int32, (nc, NSA_D), 0)
    end_cols = jnp.where(lane == 0, lax.shift_right_logical(blk, END_SHIFT),
                         jnp.where(lane == 1, blk & ((1 << END_SHIFT) - 1), 0)).astype(F32)
    for j in range(4):
        kv, g = divmod(j, NSA_GROUPS)
        c = c_ref[j]
        a0 = _dot((c + pos_ref[kv, 0]).astype(BF16), w1_ref[kv, 0])
        a1 = _dot((c + pos_ref[kv, 1]).astype(BF16), w1_ref[kv, 1])
        pre = a0 + pltpu.roll(a1, nc - 1, 0)
        hid = jax.nn.gelu(pre).astype(BF16)
        if kv == 0:
            kc_ref[g] = jnp.concatenate([_dot(hid, w2k_ref[...]), end_cols], axis=1).astype(BF16)
        else:
            vct_ref[g * NSA_D:(g + 1) * NSA_D] = _dot_nt(w2vt_ref[...], hid).astype(BF16)


def _compress(cmp_in, pos_k, w1_k, w2_k, pos_v, w1_v, w2_v):
    b, _, s, dk = cmp_in.shape
    nc = s // CMP_STRIDE
    kdim = CMP_STRIDE * dk
    c4 = cmp_in.reshape(b, 4, nc, kdim)
    pos = jnp.stack([pos_k.reshape(2, 1, kdim), pos_v.reshape(2, 1, kdim)])
    w1 = jnp.stack([w1_k.reshape(2, kdim, CMP_HID), w1_v.reshape(2, kdim, CMP_HID)]).astype(BF16)
    return pl.pallas_call(
        _compress_kernel,
        grid=(b,),
        in_specs=[pl.BlockSpec((None, 4, nc, kdim), lambda bi: (bi, 0, 0, 0)),
                  _const_spec(pos.shape), _const_spec(w1.shape),
                  _const_spec((CMP_HID, dk)), _const_spec((dk, CMP_HID))],
        out_specs=[pl.BlockSpec((None, NSA_GROUPS, nc, LANES), lambda bi: (bi, 0, 0, 0)),
                   pl.BlockSpec((None, LANES, nc), lambda bi: (bi, 0, 0))],
        out_shape=[jax.ShapeDtypeStruct((b, NSA_GROUPS, nc, LANES), BF16),
                   jax.ShapeDtypeStruct((b, LANES, nc), BF16)],
        compiler_params=_params("parallel"),
        name="compress",
    )(c4, pos, w1, w2_k.astype(BF16), w2_v.T.astype(BF16))


def _nsa_cmp_kernel(qt_ref, kc_ref, vct_ref, ovt_ref, pool_ref, oct_ref, sbt_ref, act_ref, *, tq, tq_main):
    nc = kc_ref.shape[1]
    t = pl.program_id(1) * tq + lax.broadcasted_iota(jnp.int32, (1, tq), 1)
    cmp_end = lax.broadcasted_iota(jnp.int32, (nc, 1), 0) * CMP_STRIDE + (CMP_LEN - 1)
    valid = cmp_end <= t
    any_valid = t >= CMP_LEN - 1
    slopes = _alibi_slopes()
    row = lax.broadcasted_iota(jnp.int32, (NSA_D, tq), 0)
    blk = lax.broadcasted_iota(jnp.int32, (LANES, tq), 0)
    blkf = blk.astype(F32)
    cur = lax.shift_right_logical(t, SEL_SHIFT)
    forced = (blk == 0) | (blk == cur) | (blk == cur - 1)
    for g in range(NSA_GROUPS):
        vo = jnp.concatenate([vct_ref[g * NSA_D:(g + 1) * NSA_D], ovt_ref[...]], axis=0)
        imp = jnp.zeros((LANES, tq), F32)
        for hh in range(NSA_HPG):
            h = g * NSA_HPG + hh
            hi_w = slopes[h] * (CMP_STRIDE << END_SHIFT)
            tail = jnp.where(row == 0, hi_w, jnp.where(row == 1, slopes[h] * CMP_STRIDE, 0.0)).astype(BF16)
            q = jnp.concatenate([qt_ref[h * LANES + g * NSA_D:h * LANES + (g + 1) * NSA_D], tail], axis=0)
            s = jnp.where(valid, _dot(kc_ref[g], q), NEG)
            e = jnp.exp(s - jnp.max(s, axis=0, keepdims=True))
            norm = jnp.where(any_valid, 1.0 / jnp.maximum(jnp.sum(e, axis=0, keepdims=True), 1e-30), 0.0)
            r = _dot(vo, e.astype(BF16)) * norm
            oct_ref[h * NSA_D:(h + 1) * NSA_D] = r[:NSA_D]
            imp = imp + r[NSA_D:]
        work = jnp.where(blk <= cur, jnp.where(forced, FORCE_SCORE, imp), NEG)
        chosen = jnp.zeros((LANES, tq), jnp.bool_)
        for _ in range(SEL_TOPK):
            top = jnp.max(work, axis=0, keepdims=True)
            idx = jnp.min(jnp.where(work == top, blkf, float(LANES)), axis=0, keepdims=True)
            hit = blkf == idx
            chosen = chosen | hit
            work = jnp.where(hit, -jnp.inf, work)
        sbt_ref[g] = jnp.where(chosen, 0.0, NEG).astype(BF16)
        used = _dot(pool_ref[...], chosen.astype(BF16))
        for j in range(tq // tq_main):
            any_q = jnp.max(used[:, j * tq_main:(j + 1) * tq_main], axis=1, keepdims=True)
            act_ref[g, j] = jnp.broadcast_to(any_q, act_ref.shape[2:])


def _nsa_cmp(qt, kc, vct, overlap_t, pool, tq, tq_main):
    b, _, s = qt.shape
    nc = kc.shape[2]
    nkt = pool.shape[0]
    return pl.pallas_call(
        functools.partial(_nsa_cmp_kernel, tq=tq, tq_main=tq_main),
        grid=(b, s // tq),
        in_specs=[pl.BlockSpec((None, NSA_HEADS * LANES, tq), lambda bi, i: (bi, 0, i)),
                  pl.BlockSpec((None, NSA_GROUPS, nc, LANES), lambda bi, i: (bi, 0, 0, 0)),
                  pl.BlockSpec((None, LANES, nc), lambda bi, i: (bi, 0, 0)),
                  _const_spec(overlap_t.shape), _const_spec(pool.shape)],
        out_specs=[pl.BlockSpec((None, NSA_HEADS * NSA_D, tq), lambda bi, i: (bi, 0, i)),
                   pl.BlockSpec((None, NSA_GROUPS, LANES, tq), lambda bi, i: (bi, 0, 0, i)),
                   pl.BlockSpec((None, NSA_GROUPS, tq // tq_main, nkt, LANES), lambda bi, i: (bi, 0, i, 0, 0))],
        out_shape=[jax.ShapeDtypeStruct((b, NSA_HEADS * NSA_D, s), F32),
                   jax.ShapeDtypeStruct((b, NSA_GROUPS, LANES, s), BF16),
                   jax.ShapeDtypeStruct((b, NSA_GROUPS, s // tq_main, nkt, LANES), F32)],
        compiler_params=_params("parallel", "parallel"),
        name="nsa_cmp",
    )(qt, kc, vct, overlap_t, pool)


def _tile_pool_matrix(seq, tk_sel):
    per_tile = tk_sel // SEL_LEN
    nkt = -(-(seq // tk_sel) // 16) * 16
    pool = np.zeros((nkt, LANES), np.float32)
    for c in range(seq // SEL_LEN):
        pool[c // per_tile, c] = 1.0
    return jnp.asarray(pool, BF16)


def _overlap_matrix_t(seq):
    n_c = (seq - CMP_LEN) // CMP_STRIDE + 1
    n_sel = seq // SEL_LEN
    c0 = np.arange(n_c) * CMP_STRIDE
    s0 = np.arange(n_sel) * SEL_LEN
    ov = np.clip(np.minimum(c0[:, None] + CMP_LEN, s0[None, :] + SEL_LEN)
                 - np.maximum(c0[:, None], s0[None, :]), 0, None) / CMP_LEN
    full = np.zeros((LANES, seq // CMP_STRIDE), np.float32)
    full[:n_sel, :n_c] = ov.T
    return jnp.asarray(full, BF16)


def _flash_pipelined(n_tiles, first_scores, scores, values, dv, s_buf, p_buf, exp=jnp.exp):
    chains = range(len(first_scores))
    nq = first_scores[0].shape[1]
    state = tuple((jnp.full((1, nq), NEG, F32), jnp.zeros((dv + V_EXTRA, nq), F32)) for _ in chains)
    for c in chains:
        s_buf[0, c] = first_scores[c]
        p_buf[1, c] = jnp.zeros(p_buf.shape[2:], BF16)
    colmax = tuple(jnp.max(first_scores[c], axis=0, keepdims=True) for c in chains)
    ones = tuple(jnp.ones((1, nq), F32) for _ in chains)

    static = isinstance(n_tiles, int)

    def step(i, carry, cur):
        st, alphas, cmax = carry
        nxt = 1 - cur
        vts = values(max(i - 1, 0) if static else jnp.maximum(i - 1, 0))
        s_next = None if static and i + 1 == n_tiles else scores(i + 1)
        new_st, new_alpha, new_cmax = [], [], []
        for c in chains:
            m, acc = st[c]
            acc = alphas[c] * acc + _dot(vts[c], p_buf[nxt, c])
            m_new = jnp.maximum(m, cmax[c])
            p_buf[cur, c] = exp(s_buf[cur, c] - m_new).astype(BF16)
            new_st.append((m_new, acc))
            new_alpha.append(exp(m - m_new))
            if s_next is not None:
                s_buf[nxt, c] = s_next[c]
                new_cmax.append(jnp.max(s_next[c], axis=0, keepdims=True))
        return tuple(new_st), tuple(new_alpha), tuple(new_cmax)

    def steps(first, count, carry):
        for u in range(count):
            carry = step(first + u, carry, u % 2)
        return carry

    carry = (state, ones, colmax)
    if static:
        st, alphas, _ = steps(0, n_tiles, carry)
        vts = values(n_tiles - 1)
    else:
        n_pairs = n_tiles // 2
        carry = lax.fori_loop(0, n_pairs, lambda j, c: steps(2 * j, 2, c), carry)
        st, alphas, _ = lax.fori_loop(2 * n_pairs, n_tiles, lambda i, c: steps(i, 1, c), carry)
        vts = values(jnp.maximum(n_tiles - 1, 0))
    last = (n_tiles + 1) & 1
    outs = []
    for c, (_, acc) in enumerate(st):
        acc = alphas[c] * acc + _dot(vts[c], p_buf[last, c])
        outs.append(acc[:dv] * (1.0 / acc[dv:dv + 1]))
    return outs


def _nsa_main_kernel(act_ref, qt_ref, sbt_ref, oct_ref, gt_ref, ksa_ref, kwa_ref, vst_ref, vwt_ref, y_ref,
                     s_buf, p_buf, tiles_ref, *, tq, tk_sel, tk_win, act_tiles):
    assert tq == tk_sel
    t0 = pl.program_id(1) * tq
    nq = NSA_HPG * tq
    slopes = _alibi_slopes()
    t_lane = jnp.concatenate([t0 + lax.broadcasted_iota(jnp.int32, (1, tq), 1)] * NSA_HPG, axis=1)
    row = lax.broadcasted_iota(jnp.int32, (NSA_D, nq), 0)
    blk_rel = (lax.broadcasted_iota(jnp.int32, (LANES, tq), 0)
               - lax.shift_right_logical(t0, SEL_SHIFT)).astype(F32)

    q_sel, q_win, slope_l = [], [], []
    for g in range(NSA_GROUPS):
        heads = range(g * NSA_HPG, (g + 1) * NSA_HPG)
        sl = jnp.concatenate([jnp.full((1, tq), slopes[h], F32) for h in heads], axis=1)
        qg = jnp.concatenate(
            [qt_ref[h * LANES + g * NSA_D:h * LANES + (g + 1) * NSA_D] for h in heads], axis=1)
        tail = jnp.where(row == 0, sl, 0.0).astype(BF16)
        tail_win = jnp.where(row == 0, sl, jnp.where(row == 1, sl * LANES, 0.0)).astype(BF16)
        sb = sbt_ref[g].astype(F32)
        bias = jnp.concatenate([sb + (slopes[h] * SEL_LEN) * blk_rel for h in heads], axis=1)
        q_sel.append(jnp.concatenate([bias.astype(BF16), qg, tail], axis=0))
        q_win.append(jnp.concatenate([qg, tail_win], axis=0))

    groups = range(NSA_GROUPS)

    def sel_scores(kt, g):
        k0 = pl.multiple_of(kt * tk_sel, tk_sel)
        return _dot(ksa_ref[g, pl.ds(k0, tk_sel), :], q_sel[g])

    def sel_values(kt, g):
        k0 = pl.multiple_of(kt * tk_sel, tk_sel)
        return vst_ref[g, :, pl.ds(k0, tk_sel)]

    n_full = (t0 + 1) // tk_sel
    n_kt = tiles_ref.shape[1]
    counts, spare = [], []
    for g in groups:
        base = ((pl.program_id(0) * NSA_GROUPS + g) * pl.num_programs(1) + pl.program_id(1)) * act_tiles
        for k in range(n_kt):
            tiles_ref[g, k] = 0

        def scan(kt, carry, g=g, base=base):
            cnt, unused = carry
            used = act_ref[base + kt]
            tiles_ref[g, cnt] = kt
            return cnt + used, jnp.where(used == 0, kt, unused)

        cnt, unused = lax.fori_loop(0, n_full, scan, (jnp.int32(0), jnp.int32(0)))
        counts.append(cnt)
        spare.append(unused)

    def tile_of(k, g):
        return jnp.where(k < counts[g], tiles_ref[g, jnp.minimum(k, n_kt - 1)], spare[g])

    diag = t0 // tk_sel
    causal = diag * tk_sel + lax.broadcasted_iota(jnp.int32, (tk_sel, 1), 0) <= t_lane
    at = lambda k, g: jnp.where(k == 0, diag, tile_of(jnp.maximum(k - 1, 0), g))
    sel = _flash_pipelined(jnp.maximum(counts[0], counts[1]) + 1,
                           [jnp.where(causal, sel_scores(diag, g), NEG) for g in groups],
                           lambda k: [sel_scores(tile_of(k - 1, g), g) for g in groups],
                           lambda k: [sel_values(at(k, g), g) for g in groups], NSA_D, s_buf, p_buf)

    assert WINDOW % tq == 0
    start = pl.multiple_of(jnp.maximum(t0 - WINDOW, 0), tq)
    n_wt = WINDOW // tq + 1

    def win_scores(k, g):
        w = n_wt - 1 - k
        k0 = pl.multiple_of(start + w * tq, tq)
        key = k0 + lax.broadcasted_iota(jnp.int32, (tq, 1), 0)
        ok = key <= t_lane
        if w in (0, n_wt - 1):
            ok = ok & (key > t_lane - WINDOW)
        return jnp.where(ok, _dot(kwa_ref[g, pl.ds(k0, tq), :], q_win[g]), NEG)

    def win_values(k, g):
        k0 = pl.multiple_of(start + (n_wt - 1 - k) * tq, tq)
        return vwt_ref[g, :, pl.ds(k0, tq)]

    win = _flash_pipelined(n_wt, [win_scores(0, g) for g in groups],
                           lambda k: [win_scores(k, g) for g in groups],
                           lambda k: [win_values(k, g) for g in groups], NSA_D, s_buf, p_buf)

    gates = gt_ref[...]
    outs = []
    for h in range(NSA_HEADS):
        g, hh = divmod(h, NSA_HPG)
        cols = slice(hh * tq, (hh + 1) * tq)
        outs.append(gates[3 * h:3 * h + 1] * oct_ref[h * NSA_D:(h + 1) * NSA_D]
                    + gates[3 * h + 1:3 * h + 2] * sel[g][:, cols] + gates[3 * h + 2:3 * h + 3] * win[g][:, cols])
    y_ref[...] = jnp.concatenate(outs, axis=0).T.astype(BF16)


def _nsa_main(act, qt, sbt, oct, gt, ksa, kwa, vst, vwt, tq, tk_sel, tk_win):
    b, _, s = qt.shape
    act_tiles = act.shape[0] // (b * NSA_GROUPS * (s // tq))
    feat = lambda r: pl.BlockSpec((None, r, tq), lambda bi, i, _: (bi, 0, i))
    grid_spec = pltpu.PrefetchScalarGridSpec(
        num_scalar_prefetch=1,
        grid=(b, s // tq),
        in_specs=[feat(NSA_HEADS * LANES),
                  pl.BlockSpec((None, NSA_GROUPS, LANES, tq), lambda bi, i, _: (bi, 0, 0, i)),
                  feat(NSA_HEADS * NSA_D), feat(3 * NSA_HEADS),
                  pl.BlockSpec((None, NSA_GROUPS, s, 2 * LANES), lambda bi, i, _: (bi, 0, 0, 0)),
                  pl.BlockSpec((None, NSA_GROUPS, s, LANES), lambda bi, i, _: (bi, 0, 0, 0)),
                  pl.BlockSpec((None, NSA_GROUPS, NSA_D + V_EXTRA, s), lambda bi, i, _: (bi, 0, 0, 0)),
                  pl.BlockSpec((None, NSA_GROUPS, NSA_D + V_EXTRA, s), lambda bi, i, _: (bi, 0, 0, 0))],
        out_specs=pl.BlockSpec((None, tq, NSA_HEADS * NSA_D), lambda bi, i, _: (bi, i, 0)),
        scratch_shapes=[pltpu.VMEM((2, NSA_GROUPS, tk_sel, NSA_HPG * tq), F32),
                        pltpu.VMEM((2, NSA_GROUPS, tk_sel, NSA_HPG * tq), BF16),
                        pltpu.SMEM((NSA_GROUPS, act_tiles), jnp.int32)])
    return pl.pallas_call(
        functools.partial(_nsa_main_kernel, tq=tq, tk_sel=tk_sel, tk_win=tk_win, act_tiles=act_tiles),
        grid_spec=grid_spec,
        out_shape=jax.ShapeDtypeStruct((b, s, NSA_HEADS * NSA_D), BF16),
        compiler_params=_params("parallel", "arbitrary"),
        name="nsa_main",
    )(act, qt, sbt, oct, gt, ksa, kwa, vst, vwt)


def _mla_kernel(qt_ref, k_ref, vt_ref, y_ref, s_buf, p_buf, *, tq, tk):
    t0 = pl.program_id(2) * tq
    t_lane = t0 + lax.broadcasted_iota(jnp.int32, (1, tq), 1)
    heads = range(2)
    qts = [qt_ref[hh * LANES:(hh + 1) * LANES] for hh in heads]

    def scores(kt):
        k0 = pl.multiple_of(kt * tk, tk)
        return [_dot(k_ref[hh, pl.ds(k0, tk), :], qts[hh]) for hh in heads]

    def values(kt):
        k0 = pl.multiple_of(kt * tk, tk)
        return [vt_ref[hh, :, pl.ds(k0, tk)] for hh in heads]

    assert tk == tq
    diag = pl.program_id(2)
    causal = t0 + lax.broadcasted_iota(jnp.int32, (tk, 1), 0) <= t_lane
    st = _flash_pipelined(diag + 1, [jnp.where(causal, s, NEG) for s in scores(diag)],
                          lambda k: scores(k - 1),
                          lambda k: values(jnp.where(k == 0, diag, jnp.maximum(k - 1, 0))),
                          MLA_V, s_buf, p_buf, exp=jnp.exp2)
    o = jnp.concatenate(st, axis=0)
    y_ref[...] = o.T.astype(BF16)


def _mla(qmt, km, vmt, tq, tk):
    b, h, s, _ = km.shape
    k5 = km.reshape(b, h // 2, 2, s, LANES)
    vmt = vmt.reshape(b, h // 2, 2, MLA_V + V_EXTRA, s)
    return pl.pallas_call(
        functools.partial(_mla_kernel, tq=tq, tk=tk),
        grid=(b, h // 2, s // tq),
        in_specs=[pl.BlockSpec((None, 2 * LANES, tq), lambda bi, hp, i: (bi, hp, i)),
                  pl.BlockSpec((None, None, 2, s, LANES), lambda bi, hp, i: (bi, hp, 0, 0, 0)),
                  pl.BlockSpec((None, None, 2, MLA_V + V_EXTRA, s), lambda bi, hp, i: (bi, hp, 0, 0, 0))],
        out_specs=pl.BlockSpec((None, tq, 2 * MLA_V), lambda bi, hp, i: (bi, i, hp)),
        out_shape=jax.ShapeDtypeStruct((b, s, h * MLA_V), BF16),
        scratch_shapes=[pltpu.VMEM((2, 2, tk, tq), F32), pltpu.VMEM((2, 2, tk, tq), BF16)],
        compiler_params=_params("parallel", "parallel", "arbitrary"),
        name="mla",
    )(qmt, k5, vmt)


def _merge_kernel(x_ref, yn_ref, ym_ref, pre_ref, post_ref, wgm_ref, wpn_ref, wpm_ref, wo_ref, o_ref):
    x = x_ref[...]
    d = x.shape[1]
    h = _rms(x, pre_ref[...]).astype(BF16)
    gm = jax.nn.sigmoid(_dot(h, wgm_ref[...]))
    merged = gm[:, :d] * _dot(yn_ref[...], wpn_ref[...]) + gm[:, d:] * _dot(ym_ref[...], wpm_ref[...])
    y = _dot(merged.astype(BF16), wo_ref[...])
    o_ref[...] = x + _rms(y, post_ref[...])


def _merge(x1, y_nsa, y_mla, pre_g, post_g, w_gm, w_pn, w_pm, w_out, tm):
    n, d = x1.shape
    tok = lambda w: pl.BlockSpec((tm, w), lambda i: (i, 0))
    return pl.pallas_call(
        _merge_kernel,
        grid=(n // tm,),
        in_specs=[tok(d), tok(y_nsa.shape[1]), tok(y_mla.shape[1]), _const_spec((1, d)), _const_spec((1, d)),
                  _const_spec(w_gm.shape), _const_spec(w_pn.shape), _const_spec(w_pm.shape),
                  _const_spec(w_out.shape)],
        out_specs=tok(d),
        out_shape=jax.ShapeDtypeStruct((n, d), F32),
        compiler_params=_params("parallel"),
        name="merge",
    )(x1, y_nsa, y_mla, pre_g.reshape(1, d), post_g.reshape(1, d),
      w_gm.astype(BF16), w_pn.astype(BF16), w_pm.astype(BF16), w_out.astype(BF16))


def kernel(x, ff1_pre_g, ff1_post_g, ff1_w_gate, ff1_w_up, ff1_w_down, mix_pre_g, mix_post_g, w_in, cmp_pos_k, cmp_w1_k, cmp_w2_k, cmp_pos_v, cmp_w1_v, cmp_w2_v, mla_q_norm_g, mla_w_uq, mla_kv_norm_g, mla_w_ukv, w_proj_nsa, w_proj_mla, w_out, ff2_pre_g, ff2_post_g, ff2_w_gate, ff2_w_up, ff2_w_down):
    b, s, d = x.shape
    assert s % (SEL_LEN * SEL_TOPK) == 0 and s // SEL_LEN <= LANES
    tl = _tiles(s)
    n = b * s

    x1 = _ffn(x.reshape(n, d), ff1_pre_g, ff1_post_g, ff1_w_gate, ff1_w_up, ff1_w_down, tl["tm_ffn"])

    (qt, ksa, kwa, vst, vwt, cmp_in, gt, qmt, km, vmt) = _inproj(
        x1.reshape(b, s, d), mix_pre_g, _inproj_weights(w_in, mla_w_uq, mla_w_ukv),
        mla_q_norm_g, mla_kv_norm_g, _rope_tables(s), tl["tm"])

    kc, vct = _compress(cmp_in, cmp_pos_k, cmp_w1_k, cmp_w2_k, cmp_pos_v, cmp_w1_v, cmp_w2_v)
    oct, sbt, act = _nsa_cmp(qt, kc, vct, _overlap_matrix_t(s), _tile_pool_matrix(s, tl["tk_sel"]),
                             tl["tq_cmp"], tl["tq_nsa"])
    act = (act[..., 0] > 0).astype(jnp.int32).reshape(-1)
    y_nsa = _nsa_main(act, qt, sbt, oct, gt, ksa, kwa, vst, vwt, tl["tq_nsa"], tl["tk_sel"], tl["tk_win"])
    y_mla = _mla(qmt, km, vmt, tl["tq_mla"], tl["tq_mla"])

    o_gm = sum((NSA_HEADS * NSA_D, 6 * NSA_GROUPS * NSA_D, 3 * NSA_HEADS, MLA_Q_RANK, MLA_KV_RANK, MLA_ROPE))
    x2 = _merge(x1, y_nsa.reshape(n, -1), y_mla.reshape(n, -1), mix_pre_g, mix_post_g,
                w_in[:, o_gm:], w_proj_nsa, w_proj_mla, w_out, tl["tm"])
    x3 = _ffn(x2, ff2_pre_g, ff2_post_g, ff2_w_gate, ff2_w_up, ff2_w_down, tl["tm_ffn"])
    return x3.reshape(b, s, d)
```

```python
import functools

import numpy as np
import jax
import jax.numpy as jnp
from jax import lax
from jax.experimental import pallas as pl
from jax.experimental.pallas import tpu as pltpu

F32 = jnp.float32
BF16 = jnp.bfloat16

EPS = 1e-6
NEG = -1e30
FORCE_SCORE = 1e4
NSA_HEADS = 8
NSA_GROUPS = 2
NSA_HPG = NSA_HEADS // NSA_GROUPS
NSA_D = 64
CMP_LEN = 32
CMP_STRIDE = 16
CMP_HID = 256
END_SHIFT = 4
SEL_LEN = 64
SEL_SHIFT = 6
SEL_TOPK = 16
WINDOW = 512
MLA_HEADS = 8
MLA_NOPE = 64
MLA_ROPE = 32
MLA_V = 64
MLA_Q_RANK = 256
MLA_KV_RANK = 128
ROPE_THETA = 10000.0
LANES = 128
POS_SHIFT = 7
LOG2E = 1.4426950408889634
VMEM_LIMIT = 56 * 1024 * 1024
FFN_CHUNK = 256
V_EXTRA = 16


def _tiles(seq):
    return dict(
        tm=min(1024, seq),
        tq_cmp=min(1024, seq),
        tq_nsa=min(256, seq),
        tk_sel=min(256, seq),
        tq_mla=min(512, seq),
    )


def _params(*sem):
    return pltpu.CompilerParams(dimension_semantics=sem, vmem_limit_bytes=VMEM_LIMIT)


def _const_spec(shape):
    nd = len(shape)
    return pl.BlockSpec(shape, lambda *_: (0,) * nd, pipeline_mode=pl.Buffered(1))


def _rms(x, g):
    return x * lax.rsqrt(jnp.mean(x * x, axis=-1, keepdims=True) + EPS) * g


def _dot(a, b):
    return jnp.dot(a, b, preferred_element_type=F32)


def _dot_nt(a, b):
    return lax.dot_general(a, b, (((1,), (1,)), ((), ())), preferred_element_type=F32)


def _alibi_slopes():
    return [float(2.0 ** (-8.0 * (i + 1) / NSA_HEADS)) for i in range(NSA_HEADS)]


def _ffn_kernel(x_ref, pre_ref, post_ref, wg_ref, wu_ref, wd_ref, o_ref, *, fc):
    x = x_ref[...]
    h = _rms(x, pre_ref[...]).astype(BF16)
    acc = jnp.zeros(x.shape, F32)
    for c in range(wg_ref.shape[1] // fc):
        gate = _dot(h, wg_ref[:, c * fc:(c + 1) * fc])
        up = _dot(h, wu_ref[:, c * fc:(c + 1) * fc])
        act = (gate * jax.nn.sigmoid(gate) * up).astype(BF16)
        acc = acc + _dot(act, wd_ref[c * fc:(c + 1) * fc, :])
    o_ref[...] = x + 0.5 * _rms(acc, post_ref[...])


def _ffn(x2d, pre_g, post_g, w_gate, w_up, w_down, tm):
    n, d = x2d.shape
    f = w_gate.shape[1]
    fc = FFN_CHUNK if f % FFN_CHUNK == 0 else f
    return pl.pallas_call(
        functools.partial(_ffn_kernel, fc=fc),
        grid=(n // tm,),
        in_specs=[pl.BlockSpec((tm, d), lambda i: (i, 0)),
                  _const_spec((1, d)), _const_spec((1, d)),
                  _const_spec((d, f)), _const_spec((d, f)), _const_spec((f, d))],
        out_specs=pl.BlockSpec((tm, d), lambda i: (i, 0)),
        out_shape=jax.ShapeDtypeStruct((n, d), F32),
        compiler_params=_params("parallel"),
        name="ffn",
    )(x2d, pre_g.reshape(1, d), post_g.reshape(1, d),
      w_gate.astype(BF16), w_up.astype(BF16), w_down.astype(BF16))


_T_KSEL = 0
_T_KWIN = _T_KSEL + 2 * LANES
_T_CMP = _T_KWIN + 2 * LANES
_T_CQ = _T_CMP + 2 * LANES
_T_CKV = _T_CQ + MLA_Q_RANK
_T_KPE = _T_CKV + MLA_KV_RANK
_T_END = _T_KPE + 2 * LANES
_F_QN = 0
_F_VSEL = _F_QN + NSA_HEADS * NSA_D
_F_VWIN = _F_VSEL + LANES
_F_GATE = _F_VWIN + LANES
_F_END = _F_GATE + 32


def _inproj_weights(w_in, w_uq, w_ukv):
    d = w_in.shape[0]
    o_q = 0
    o_kv = o_q + NSA_HEADS * NSA_D
    o_g = o_kv + 6 * NSA_GROUPS * NSA_D
    o_cq = o_g + 3 * NSA_HEADS
    o_ckv = o_cq + MLA_Q_RANK
    o_kpe = o_ckv + MLA_KV_RANK
    zeros = lambda n: jnp.zeros((d, n), w_in.dtype)
    kv = lambda j, g: w_in[:, o_kv + (j * NSA_GROUPS + g) * NSA_D:o_kv + (j * NSA_GROUPS + g + 1) * NSA_D]
    half = MLA_ROPE // 2
    kp1 = w_in[:, o_kpe:o_kpe + half]
    kp2 = w_in[:, o_kpe + half:o_kpe + MLA_ROPE]
    tail = LANES - MLA_NOPE - MLA_ROPE
    cols = [kv(2, 0), zeros(NSA_D), kv(2, 1), zeros(NSA_D),
            kv(4, 0), zeros(NSA_D), kv(4, 1), zeros(NSA_D),
            kv(0, 0), kv(0, 1), kv(1, 0), kv(1, 1),
            w_in[:, o_cq:o_cq + MLA_Q_RANK], w_in[:, o_ckv:o_ckv + MLA_KV_RANK],
            zeros(MLA_NOPE), kp1, kp2, zeros(tail), zeros(MLA_NOPE), kp2, kp1, zeros(tail)]
    w_tok = jnp.concatenate(cols, axis=1).astype(BF16)
    assert w_tok.shape[1] == _T_END
    rows = [w_in[:, o_q:o_q + NSA_HEADS * NSA_D], kv(3, 0), kv(3, 1), kv(5, 0), kv(5, 1)]
    rows += [w_in[:, o_g:o_g + 3 * NSA_HEADS], zeros(_F_END - _F_GATE - 3 * NSA_HEADS)]
    w_feat = jnp.concatenate(rows, axis=1).T.astype(BF16)
    assert w_feat.shape[0] == _F_END

    dq = MLA_NOPE + MLA_ROPE
    zq = lambda n: jnp.zeros((w_uq.shape[0], n), w_uq.dtype)
    qa, qs = [], []
    for h in range(MLA_HEADS):
        nope = w_uq[:, h * dq:h * dq + MLA_NOPE]
        r1 = w_uq[:, h * dq + MLA_NOPE:h * dq + MLA_NOPE + half]
        r2 = w_uq[:, h * dq + MLA_NOPE + half:(h + 1) * dq]
        qa += [nope, r1, r2, zq(tail)]
        qs += [zq(MLA_NOPE), r2, r1, zq(tail)]
    w_q2t = jnp.concatenate(qa + qs, axis=1).T.astype(BF16)

    dkv = MLA_NOPE + MLA_V
    zk = lambda n: jnp.zeros((w_ukv.shape[0], n), w_ukv.dtype)
    ka, va = [], []
    for h in range(MLA_HEADS):
        ka += [w_ukv[:, h * dkv:h * dkv + MLA_NOPE], zk(LANES - MLA_NOPE)]
        va += [w_ukv[:, h * dkv + MLA_NOPE:(h + 1) * dkv]]
    w_k2 = jnp.concatenate(ka, axis=1).astype(BF16)
    w_v2t = jnp.concatenate(va, axis=1).T.astype(BF16)
    return w_tok, w_feat, w_q2t, w_k2, w_v2t


def _rope_tables(seq):
    half = MLA_ROPE // 2
    freqs = jnp.asarray(ROPE_THETA ** (-np.arange(half, dtype=np.float32) / half), F32)
    ang = jnp.arange(seq).astype(F32)[:, None] * freqs[None, :]
    cos, sin = jnp.cos(ang), jnp.sin(ang)
    pad = jnp.zeros((seq, LANES - MLA_NOPE - MLA_ROPE), F32)
    cq = jnp.concatenate([jnp.ones((seq, MLA_NOPE), F32), cos, cos, pad], axis=1)
    ck = jnp.concatenate([jnp.zeros((seq, MLA_NOPE), F32), cos, cos, pad], axis=1)
    sn = jnp.concatenate([jnp.zeros((seq, MLA_NOPE), F32), -sin, sin, pad], axis=1)
    return ck, sn, cq.T, sn.T


def _inproj_kernel(x_ref, g_ref, wt_ref, wf_ref, qg_ref, wq2_ref, kvg_ref, wk2_ref, wv2_ref,
                   ck_ref, sn_ref, cqt_ref, snt_ref,
                   qt_ref, ksa_ref, kwa_ref, vst_ref, vwt_ref, cmp_ref, gt_ref, qmt_ref, km_ref, vmt_ref,
                   *, tm):
    h = _rms(x_ref[...], g_ref[...]).astype(BF16)
    z = _dot(h, wt_ref[...])
    zt = _dot_nt(wf_ref[...], h)
    qt_ref[...] = (zt[_F_QN:_F_VSEL] * NSA_D ** -0.5).astype(BF16)
    ones_rows = (lax.broadcasted_iota(jnp.int32, (V_EXTRA, tm), 0) == 0).astype(BF16)
    for g in range(NSA_GROUPS):
        for ref, base in ((vst_ref, _F_VSEL), (vwt_ref, _F_VWIN)):
            ref[g, :NSA_D] = zt[base + g * NSA_D:base + (g + 1) * NSA_D].astype(BF16)
            ref[g, NSA_D:] = ones_rows
    gt_ref[...] = jax.nn.sigmoid(zt[_F_GATE:_F_GATE + 3 * NSA_HEADS])
    pos = pl.program_id(1) * tm + lax.broadcasted_iota(jnp.int32, (tm, LANES), 0)
    lane = lax.broadcasted_iota(jnp.int32, (tm, LANES), 1)
    onehot = (lane == lax.shift_right_logical(pos, SEL_SHIFT)).astype(BF16)
    in_blk = (pos & (SEL_LEN - 1)).astype(F32)
    pos_lo_hi = jnp.where(lane == NSA_D, pos & (LANES - 1), lax.shift_right_logical(pos, POS_SHIFT)).astype(F32)
    for g in range(NSA_GROUPS):
        ksa_ref[g, :, :LANES] = onehot
        ksa_ref[g, :, LANES:] = jnp.where(
            lane == NSA_D, in_blk, z[:, _T_KSEL + g * LANES:_T_KSEL + (g + 1) * LANES]).astype(BF16)
        kwa_ref[g] = jnp.where(
            (lane == NSA_D) | (lane == NSA_D + 1), pos_lo_hi,
            z[:, _T_KWIN + g * LANES:_T_KWIN + (g + 1) * LANES]).astype(BF16)
    for j in range(4):
        cmp_ref[j] = z[:, _T_CMP + j * NSA_D:_T_CMP + (j + 1) * NSA_D]
    cqn = _rms(z[:, _T_CQ:_T_CKV], qg_ref[...]).astype(BF16)
    q2t = _dot_nt(wq2_ref[...], cqn)
    cqt, snt = cqt_ref[...], snt_ref[...]
    nq = MLA_HEADS * LANES
    scale = (MLA_NOPE + MLA_ROPE) ** -0.5 * LOG2E
    for hh in range(MLA_HEADS):
        qa = q2t[hh * LANES:(hh + 1) * LANES]
        qs = q2t[nq + hh * LANES:nq + (hh + 1) * LANES]
        qmt_ref[hh * LANES:(hh + 1) * LANES] = ((qa * cqt + qs * snt) * scale).astype(BF16)
    ckvn = _rms(z[:, _T_CKV:_T_KPE], kvg_ref[...]).astype(BF16)
    k2 = _dot(ckvn, wk2_ref[...])
    krot = z[:, _T_KPE:_T_KPE + LANES] * ck_ref[...] + z[:, _T_KPE + LANES:_T_END] * sn_ref[...]
    for hh in range(MLA_HEADS):
        km_ref[hh] = (k2[:, hh * LANES:(hh + 1) * LANES] + krot).astype(BF16)
    vt = _dot_nt(wv2_ref[...], ckvn)
    for hh in range(MLA_HEADS):
        vmt_ref[hh, :MLA_V] = vt[hh * MLA_V:(hh + 1) * MLA_V].astype(BF16)
        vmt_ref[hh, MLA_V:] = ones_rows


def _inproj(x1, mix_pre_g, weights, q_norm_g, kv_norm_g, tables, tm):
    b, s, d = x1.shape
    w_tok, w_feat, w_q2t, w_k2, w_v2t = weights
    ck, sn, cqt, snt = tables
    tok = lambda w: pl.BlockSpec((None, tm, w), lambda bi, i: (bi, i, 0))
    feat = lambda r: pl.BlockSpec((None, r, tm), lambda bi, i: (bi, 0, i))
    heads = lambda n, w: pl.BlockSpec((None, n, tm, w), lambda bi, i: (bi, 0, i, 0))
    slabs = lambda n, r: pl.BlockSpec((None, n, r, tm), lambda bi, i: (bi, 0, 0, i))
    tab = pl.BlockSpec((tm, LANES), lambda bi, i: (i, 0))
    tabt = pl.BlockSpec((LANES, tm), lambda bi, i: (0, i))
    sds = jax.ShapeDtypeStruct
    return pl.pallas_call(
        functools.partial(_inproj_kernel, tm=tm),
        grid=(b, s // tm),
        in_specs=[tok(d), _const_spec((1, d)), _const_spec(w_tok.shape), _const_spec(w_feat.shape),
                  _const_spec((1, MLA_Q_RANK)), _const_spec(w_q2t.shape),
                  _const_spec((1, MLA_KV_RANK)), _const_spec(w_k2.shape), _const_spec(w_v2t.shape),
                  tab, tab, tabt, tabt],
        out_specs=[feat(NSA_HEADS * NSA_D), heads(NSA_GROUPS, 2 * LANES), heads(NSA_GROUPS, LANES),
                   slabs(NSA_GROUPS, NSA_D + V_EXTRA), slabs(NSA_GROUPS, NSA_D + V_EXTRA),
                   heads(4, NSA_D), feat(3 * NSA_HEADS),
                   feat(MLA_HEADS * LANES), heads(MLA_HEADS, LANES), slabs(MLA_HEADS, MLA_V + V_EXTRA)],
        out_shape=[sds((b, NSA_HEADS * NSA_D, s), BF16), sds((b, NSA_GROUPS, s, 2 * LANES), BF16),
                   sds((b, NSA_GROUPS, s, LANES), BF16),
                   sds((b, NSA_GROUPS, NSA_D + V_EXTRA, s), BF16), sds((b, NSA_GROUPS, NSA_D + V_EXTRA, s), BF16),
                   sds((b, 4, s, NSA_D), F32), sds((b, 3 * NSA_HEADS, s), F32),
                   sds((b, MLA_HEADS * LANES, s), BF16), sds((b, MLA_HEADS, s, LANES), BF16),
                   sds((b, MLA_HEADS, MLA_V + V_EXTRA, s), BF16)],
        compiler_params=_params("parallel", "parallel"),
        name="inproj",
    )(x1, mix_pre_g.reshape(1, d), w_tok, w_feat, q_norm_g.reshape(1, -1), w_q2t,
      kv_norm_g.reshape(1, -1), w_k2, w_v2t, ck, sn, cqt, snt)


def _compress_kernel(c_ref, pos_ref, w1_ref, w2k_ref, w2vt_ref, kc_ref, vct_ref):
    nc = c_ref.shape[1]
    lane = lax.broadcasted_iota(jnp.int32, (nc, NSA_D), 1)
    blk = lax.broadcasted_iota(jnp.int32, (nc, NSA_D), 0)
    end_cols = jnp.where(lane == 0, lax.shift_right_logical(blk, END_SHIFT),
                         jnp.where(lane == 1, blk & ((1 << END_SHIFT) - 1), 0)).astype(F32)
    for j in range(4):
        kv, g = divmod(j, NSA_GROUPS)
        c = c_ref[j]
        a0 = _dot((c + pos_ref[kv, 0]).astype(BF16), w1_ref[kv, 0])
        a1 = _dot((c + pos_ref[kv, 1]).astype(BF16), w1_ref[kv, 1])
        pre = a0 + pltpu.roll(a1, nc - 1, 0)
        hid = jax.nn.gelu(pre).astype(BF16)
        if kv == 0:
            kc_ref[g] = jnp.concatenate([_dot(hid, w2k_ref[...]), end_cols], axis=1).astype(BF16)
        else:
            vct_ref[g * NSA_D:(g + 1) * NSA_D] = _dot_nt(w2vt_ref[...], hid).astype(BF16)


def _compress(cmp_in, pos_k, w1_k, w2_k, pos_v, w1_v, w2_v):
    b, _, s, dk = cmp_in.shape
    nc = s // CMP_STRIDE
    kdim = CMP_STRIDE * dk
    c4 = cmp_in.reshape(b, 4, nc, kdim)
    pos = jnp.stack([pos_k.reshape(2, 1, kdim), pos_v.reshape(2, 1, kdim)])
    w1 = jnp.stack([w1_k.reshape(2, kdim, CMP_HID), w1_v.reshape(2, kdim, CMP_HID)]).astype(BF16)
    return pl.pallas_call(
        _compress_kernel,
        grid=(b,),
        in_specs=[pl.BlockSpec((None, 4, nc, kdim), lambda bi: (bi, 0, 0, 0)),
                  _const_spec(pos.shape), _const_spec(w1.shape),
                  _const_spec((CMP_HID, dk)), _const_spec((dk, CMP_HID))],
        out_specs=[pl.BlockSpec((None, NSA_GROUPS, nc, LANES), lambda bi: (bi, 0, 0, 0)),
                   pl.BlockSpec((None, LANES, nc), lambda bi: (bi, 0, 0))],
        out_shape=[jax.ShapeDtypeStruct((b, NSA_GROUPS, nc, LANES), BF16),
                   jax.ShapeDtypeStruct((b, LANES, nc), BF16)],
        compiler_params=_params("parallel"),
        name="compress",
    )(c4, pos, w1, w2_k.astype(BF16), w2_v.T.astype(BF16))


def _nsa_cmp_kernel(qt_ref, kc_ref, vct_ref, ovt_ref, pool_ref, oct_ref, sbt_ref, act_ref, *, tq, tq_main):
    nc = kc_ref.shape[1]
    t = pl.program_id(1) * tq + lax.broadcasted_iota(jnp.int32, (1, tq), 1)
    cmp_end = lax.broadcasted_iota(jnp.int32, (nc, 1), 0) * CMP_STRIDE + (CMP_LEN - 1)
    valid = cmp_end <= t
    any_valid = t >= CMP_LEN - 1
    slopes = _alibi_slopes()
    row = lax.broadcasted_iota(jnp.int32, (NSA_D, tq), 0)
    blk = lax.broadcasted_iota(jnp.int32, (LANES, tq), 0)
    blkf = blk.astype(F32)
    cur = lax.shift_right_logical(t, SEL_SHIFT)
    forced = (blk == 0) | (blk == cur) | (blk == cur - 1)
    for g in range(NSA_GROUPS):
        vo = jnp.concatenate([vct_ref[g * NSA_D:(g + 1) * NSA_D], ovt_ref[...]], axis=0)
        imp = jnp.zeros((LANES, tq), F32)
        for hh in range(NSA_HPG):
            h = g * NSA_HPG + hh
            hi_w = slopes[h] * (CMP_STRIDE << END_SHIFT)
            tail = jnp.where(row == 0, hi_w, jnp.where(row == 1, slopes[h] * CMP_STRIDE, 0.0)).astype(BF16)
            q = jnp.concatenate([qt_ref[h * NSA_D:(h + 1) * NSA_D], tail], axis=0)
            s = jnp.where(valid, _dot(kc_ref[g], q), NEG)
            e = jnp.exp(s - jnp.max(s, axis=0, keepdims=True))
            norm = jnp.where(any_valid, 1.0 / jnp.maximum(jnp.sum(e, axis=0, keepdims=True), 1e-30), 0.0)
            r = _dot(vo, e.astype(BF16)) * norm
            oct_ref[h * NSA_D:(h + 1) * NSA_D] = r[:NSA_D]
            imp = imp + r[NSA_D:]
        work = jnp.where(blk <= cur, jnp.where(forced, FORCE_SCORE, imp), NEG)
        chosen = jnp.zeros((LANES, tq), jnp.bool_)
        for _ in range(SEL_TOPK):
            top = jnp.max(work, axis=0, keepdims=True)
            idx = jnp.min(jnp.where(work == top, blkf, float(LANES)), axis=0, keepdims=True)
            hit = blkf == idx
            chosen = chosen | hit
            work = jnp.where(hit, -jnp.inf, work)
        sbt_ref[g] = jnp.where(chosen, 0.0, NEG).astype(BF16)
        used = _dot(pool_ref[...], chosen.astype(BF16))
        for j in range(tq // tq_main):
            any_q = jnp.max(used[:, j * tq_main:(j + 1) * tq_main], axis=1, keepdims=True)
            act_ref[g, j] = jnp.broadcast_to(any_q, act_ref.shape[2:])


def _nsa_cmp(qt, kc, vct, overlap_t, pool, tq, tq_main):
    b, _, s = qt.shape
    nc = kc.shape[2]
    nkt = pool.shape[0]
    return pl.pallas_call(
        functools.partial(_nsa_cmp_kernel, tq=tq, tq_main=tq_main),
        grid=(b, s // tq),
        in_specs=[pl.BlockSpec((None, NSA_HEADS * NSA_D, tq), lambda bi, i: (bi, 0, i)),
                  pl.BlockSpec((None, NSA_GROUPS, nc, LANES), lambda bi, i: (bi, 0, 0, 0)),
                  pl.BlockSpec((None, LANES, nc), lambda bi, i: (bi, 0, 0)),
                  _const_spec(overlap_t.shape), _const_spec(pool.shape)],
        out_specs=[pl.BlockSpec((None, NSA_HEADS * NSA_D, tq), lambda bi, i: (bi, 0, i)),
                   pl.BlockSpec((None, NSA_GROUPS, LANES, tq), lambda bi, i: (bi, 0, 0, i)),
                   pl.BlockSpec((None, NSA_GROUPS, tq // tq_main, nkt, LANES), lambda bi, i: (bi, 0, i, 0, 0))],
        out_shape=[jax.ShapeDtypeStruct((b, NSA_HEADS * NSA_D, s), F32),
                   jax.ShapeDtypeStruct((b, NSA_GROUPS, LANES, s), BF16),
                   jax.ShapeDtypeStruct((b, NSA_GROUPS, s // tq_main, nkt, LANES), F32)],
        compiler_params=_params("parallel", "parallel"),
        name="nsa_cmp",
    )(qt, kc, vct, overlap_t, pool)


def _tile_pool_matrix(seq, tk_sel):
    per_tile = tk_sel // SEL_LEN
    nkt = -(-(seq // tk_sel) // 16) * 16
    pool = np.zeros((nkt, LANES), np.float32)
    for c in range(seq // SEL_LEN):
        pool[c // per_tile, c] = 1.0
    return jnp.asarray(pool, BF16)


def _overlap_matrix_t(seq):
    n_c = (seq - CMP_LEN) // CMP_STRIDE + 1
    n_sel = seq // SEL_LEN
    c0 = np.arange(n_c) * CMP_STRIDE
    s0 = np.arange(n_sel) * SEL_LEN
    ov = np.clip(np.minimum(c0[:, None] + CMP_LEN, s0[None, :] + SEL_LEN)
                 - np.maximum(c0[:, None], s0[None, :]), 0, None) / CMP_LEN
    full = np.zeros((LANES, seq // CMP_STRIDE), np.float32)
    full[:n_sel, :n_c] = ov.T
    return jnp.asarray(full, BF16)


def _flash_pipelined(n_tiles, first_scores, scores, values, dv, s_buf, p_buf, exp=jnp.exp):
    chains = range(len(first_scores))
    nq = first_scores[0].shape[1]
    state = tuple((jnp.full((1, nq), NEG, F32), jnp.zeros((dv + V_EXTRA, nq), F32)) for _ in chains)
    for c in chains:
        s_buf[0, c] = first_scores[c]
        p_buf[1, c] = jnp.zeros(p_buf.shape[2:], BF16)
    colmax = tuple(jnp.max(first_scores[c], axis=0, keepdims=True) for c in chains)
    ones = tuple(jnp.ones((1, nq), F32) for _ in chains)

    static = isinstance(n_tiles, int)

    def step(i, carry, cur):
        st, alphas, cmax = carry
        nxt = 1 - cur
        vts = values(max(i - 1, 0) if static else jnp.maximum(i - 1, 0))
        s_next = None if static and i + 1 == n_tiles else scores(i + 1)
        new_st, new_alpha, new_cmax = [], [], []
        for c in chains:
            m, acc = st[c]
            acc = alphas[c] * acc + _dot(vts[c], p_buf[nxt, c])
            m_new = jnp.maximum(m, cmax[c])
            p_buf[cur, c] = exp(s_buf[cur, c] - m_new).astype(BF16)
            new_st.append((m_new, acc))
            new_alpha.append(exp(m - m_new))
            if s_next is not None:
                s_buf[nxt, c] = s_next[c]
                new_cmax.append(jnp.max(s_next[c], axis=0, keepdims=True))
        return tuple(new_st), tuple(new_alpha), tuple(new_cmax)

    def steps(first, count, carry):
        for u in range(count):
            carry = step(first + u, carry, u % 2)
        return carry

    carry = (state, ones, colmax)
    if static:
        st, alphas, _ = steps(0, n_tiles, carry)
        vts = values(n_tiles - 1)
    else:
        n_pairs = n_tiles // 2
        carry = lax.fori_loop(0, n_pairs, lambda j, c: steps(2 * j, 2, c), carry)
        st, alphas, _ = lax.fori_loop(2 * n_pairs, n_tiles, lambda i, c: steps(i, 1, c), carry)
        vts = values(jnp.maximum(n_tiles - 1, 0))
    last = (n_tiles + 1) & 1
    outs = []
    for c, (_, acc) in enumerate(st):
        acc = alphas[c] * acc + _dot(vts[c], p_buf[last, c])
        outs.append(acc[:dv] * (1.0 / acc[dv:dv + 1]))
    return outs


def _nsa_main_kernel(act_ref, qt_ref, sbt_ref, oct_ref, gt_ref, ksa_ref, kwa_ref, vst_ref, vwt_ref, y_ref,
                     s_buf, p_buf, tiles_ref, *, tq, tk_sel, act_tiles):
    assert tq == tk_sel
    t0 = pl.program_id(1) * tq
    nq = NSA_HPG * tq
    slopes = _alibi_slopes()
    t_lane = jnp.concatenate([t0 + lax.broadcasted_iota(jnp.int32, (1, tq), 1)] * NSA_HPG, axis=1)
    row = lax.broadcasted_iota(jnp.int32, (NSA_D, nq), 0)
    blk_rel = (lax.broadcasted_iota(jnp.int32, (LANES, tq), 0)
               - lax.shift_right_logical(t0, SEL_SHIFT)).astype(F32)

    q_sel, q_win = [], []
    for g in range(NSA_GROUPS):
        heads = range(g * NSA_HPG, (g + 1) * NSA_HPG)
        sl = jnp.concatenate([jnp.full((1, tq), slopes[h], F32) for h in heads], axis=1)
        qg = jnp.concatenate(
            [qt_ref[h * NSA_D:(h + 1) * NSA_D] for h in heads], axis=1)
        tail = jnp.where(row == 0, sl, 0.0).astype(BF16)
        tail_win = jnp.where(row == 0, sl, jnp.where(row == 1, sl * LANES, 0.0)).astype(BF16)
        sb = sbt_ref[g].astype(F32)
        bias = jnp.concatenate([sb + (slopes[h] * SEL_LEN) * blk_rel for h in heads], axis=1)
        q_sel.append(jnp.concatenate([bias.astype(BF16), qg, tail], axis=0))
        q_win.append(jnp.concatenate([qg, tail_win], axis=0))

    groups = range(NSA_GROUPS)

    def sel_scores(kt, g):
        k0 = pl.multiple_of(kt * tk_sel, tk_sel)
        return _dot(ksa_ref[g, pl.ds(k0, tk_sel), :], q_sel[g])

    def sel_values(kt, g):
        k0 = pl.multiple_of(kt * tk_sel, tk_sel)
        return vst_ref[g, :, pl.ds(k0, tk_sel)]

    n_full = (t0 + 1) // tk_sel
    n_kt = tiles_ref.shape[1]
    counts, spare = [], []
    for g in groups:
        base = ((pl.program_id(0) * NSA_GROUPS + g) * pl.num_programs(1) + pl.program_id(1)) * act_tiles
        for k in range(n_kt):
            tiles_ref[g, k] = 0

        def scan(kt, carry, g=g, base=base):
            cnt, unused = carry
            used = act_ref[base + kt]
            tiles_ref[g, cnt] = kt
            return cnt + used, jnp.where(used == 0, kt, unused)

        cnt, unused = lax.fori_loop(0, n_full, scan, (jnp.int32(0), jnp.int32(0)))
        counts.append(cnt)
        spare.append(unused)

    def tile_of(k, g):
        return jnp.where(k < counts[g], tiles_ref[g, jnp.minimum(k, n_kt - 1)], spare[g])

    diag = t0 // tk_sel
    causal = diag * tk_sel + lax.broadcasted_iota(jnp.int32, (tk_sel, 1), 0) <= t_lane
    at = lambda k, g: jnp.where(k == 0, diag, tile_of(jnp.maximum(k - 1, 0), g))
    sel = _flash_pipelined(jnp.maximum(counts[0], counts[1]) + 1,
                           [jnp.where(causal, sel_scores(diag, g), NEG) for g in groups],
                           lambda k: [sel_scores(tile_of(k - 1, g), g) for g in groups],
                           lambda k: [sel_values(at(k, g), g) for g in groups], NSA_D, s_buf, p_buf)

    assert WINDOW % tq == 0
    start = pl.multiple_of(jnp.maximum(t0 - WINDOW, 0), tq)
    n_wt = WINDOW // tq + 1

    def win_scores(k, g):
        w = n_wt - 1 - k
        k0 = pl.multiple_of(start + w * tq, tq)
        key = k0 + lax.broadcasted_iota(jnp.int32, (tq, 1), 0)
        ok = key <= t_lane
        if w in (0, n_wt - 1):
            ok = ok & (key > t_lane - WINDOW)
        return jnp.where(ok, _dot(kwa_ref[g, pl.ds(k0, tq), :], q_win[g]), NEG)

    def win_values(k, g):
        k0 = pl.multiple_of(start + (n_wt - 1 - k) * tq, tq)
        return vwt_ref[g, :, pl.ds(k0, tq)]

    win = _flash_pipelined(n_wt, [win_scores(0, g) for g in groups],
                           lambda k: [win_scores(k, g) for g in groups],
                           lambda k: [win_values(k, g) for g in groups], NSA_D, s_buf, p_buf)

    gates = gt_ref[...]
    outs = []
    for h in range(NSA_HEADS):
        g, hh = divmod(h, NSA_HPG)
        cols = slice(hh * tq, (hh + 1) * tq)
        outs.append(gates[3 * h:3 * h + 1] * oct_ref[h * NSA_D:(h + 1) * NSA_D]
                    + gates[3 * h + 1:3 * h + 2] * sel[g][:, cols] + gates[3 * h + 2:3 * h + 3] * win[g][:, cols])
    y_ref[...] = jnp.concatenate(outs, axis=0).T.astype(BF16)


def _nsa_main(act, qt, sbt, oct, gt, ksa, kwa, vst, vwt, tq, tk_sel):
    b, _, s = qt.shape
    act_tiles = act.shape[0] // (b * NSA_GROUPS * (s // tq))
    feat = lambda r: pl.BlockSpec((None, r, tq), lambda bi, i, _: (bi, 0, i))
    grid_spec = pltpu.PrefetchScalarGridSpec(
        num_scalar_prefetch=1,
        grid=(b, s // tq),
        in_specs=[feat(NSA_HEADS * NSA_D),
                  pl.BlockSpec((None, NSA_GROUPS, LANES, tq), lambda bi, i, _: (bi, 0, 0, i)),
                  feat(NSA_HEADS * NSA_D), feat(3 * NSA_HEADS),
                  pl.BlockSpec((None, NSA_GROUPS, s, 2 * LANES), lambda bi, i, _: (bi, 0, 0, 0)),
                  pl.BlockSpec((None, NSA_GROUPS, s, LANES), lambda bi, i, _: (bi, 0, 0, 0)),
                  pl.BlockSpec((None, NSA_GROUPS, NSA_D + V_EXTRA, s), lambda bi, i, _: (bi, 0, 0, 0)),
                  pl.BlockSpec((None, NSA_GROUPS, NSA_D + V_EXTRA, s), lambda bi, i, _: (bi, 0, 0, 0))],
        out_specs=pl.BlockSpec((None, tq, NSA_HEADS * NSA_D), lambda bi, i, _: (bi, i, 0)),
        scratch_shapes=[pltpu.VMEM((2, NSA_GROUPS, tk_sel, NSA_HPG * tq), F32),
                        pltpu.VMEM((2, NSA_GROUPS, tk_sel, NSA_HPG * tq), BF16),
                        pltpu.SMEM((NSA_GROUPS, act_tiles), jnp.int32)])
    return pl.pallas_call(
        functools.partial(_nsa_main_kernel, tq=tq, tk_sel=tk_sel, act_tiles=act_tiles),
        grid_spec=grid_spec,
        out_shape=jax.ShapeDtypeStruct((b, s, NSA_HEADS * NSA_D), BF16),
        compiler_params=_params("parallel", "arbitrary"),
        name="nsa_main",
    )(act, qt, sbt, oct, gt, ksa, kwa, vst, vwt)


def _mla_kernel(qt_ref, k_ref, vt_ref, y_ref, s_buf, p_buf, *, tq, tk):
    t0 = pl.program_id(2) * tq
    t_lane = t0 + lax.broadcasted_iota(jnp.int32, (1, tq), 1)
    heads = range(2)
    qts = [qt_ref[hh * LANES:(hh + 1) * LANES] for hh in heads]

    def scores(kt):
        k0 = pl.multiple_of(kt * tk, tk)
        return [_dot(k_ref[hh, pl.ds(k0, tk), :], qts[hh]) for hh in heads]

    def values(kt):
        k0 = pl.multiple_of(kt * tk, tk)
        return [vt_ref[hh, :, pl.ds(k0, tk)] for hh in heads]

    assert tk == tq
    diag = pl.program_id(2)
    causal = t0 + lax.broadcasted_iota(jnp.int32, (tk, 1), 0) <= t_lane
    st = _flash_pipelined(diag + 1, [jnp.where(causal, s, NEG) for s in scores(diag)],
                          lambda k: scores(k - 1),
                          lambda k: values(jnp.where(k == 0, diag, jnp.maximum(k - 1, 0))),
                          MLA_V, s_buf, p_buf, exp=jnp.exp2)
    o = jnp.concatenate(st, axis=0)
    y_ref[...] = o.T.astype(BF16)


def _mla(qmt, km, vmt, tq, tk):
    b, h, s, _ = km.shape
    k5 = km.reshape(b, h // 2, 2, s, LANES)
    vmt = vmt.reshape(b, h // 2, 2, MLA_V + V_EXTRA, s)
    return pl.pallas_call(
        functools.partial(_mla_kernel, tq=tq, tk=tk),
        grid=(b, h // 2, s // tq),
        in_specs=[pl.BlockSpec((None, 2 * LANES, tq), lambda bi, hp, i: (bi, hp, i)),
                  pl.BlockSpec((None, None, 2, s, LANES), lambda bi, hp, i: (bi, hp, 0, 0, 0)),
                  pl.BlockSpec((None, None, 2, MLA_V + V_EXTRA, s), lambda bi, hp, i: (bi, hp, 0, 0, 0))],
        out_specs=pl.BlockSpec((None, tq, 2 * MLA_V), lambda bi, hp, i: (bi, i, hp)),
        out_shape=jax.ShapeDtypeStruct((b, s, h * MLA_V), BF16),
        scratch_shapes=[pltpu.VMEM((2, 2, tk, tq), F32), pltpu.VMEM((2, 2, tk, tq), BF16)],
        compiler_params=_params("parallel", "parallel", "arbitrary"),
        name="mla",
    )(qmt, k5, vmt)


def _merge_kernel(x_ref, yn_ref, ym_ref, pre_ref, post_ref, wgm_ref, wpn_ref, wpm_ref, wo_ref, o_ref):
    x = x_ref[...]
    d = x.shape[1]
    h = _rms(x, pre_ref[...]).astype(BF16)
    gm = jax.nn.sigmoid(_dot(h, wgm_ref[...]))
    merged = gm[:, :d] * _dot(yn_ref[...], wpn_ref[...]) + gm[:, d:] * _dot(ym_ref[...], wpm_ref[...])
    y = _dot(merged.astype(BF16), wo_ref[...])
    o_ref[...] = x + _rms(y, post_ref[...])


def _merge(x1, y_nsa, y_mla, pre_g, post_g, w_gm, w_pn, w_pm, w_out, tm):
    n, d = x1.shape
    tok = lambda w: pl.BlockSpec((tm, w), lambda i: (i, 0))
    return pl.pallas_call(
        _merge_kernel,
        grid=(n // tm,),
        in_specs=[tok(d), tok(y_nsa.shape[1]), tok(y_mla.shape[1]), _const_spec((1, d)), _const_spec((1, d)),
                  _const_spec(w_gm.shape), _const_spec(w_pn.shape), _const_spec(w_pm.shape),
                  _const_spec(w_out.shape)],
        out_specs=tok(d),
        out_shape=jax.ShapeDtypeStruct((n, d), F32),
        compiler_params=_params("parallel"),
        name="merge",
    )(x1, y_nsa, y_mla, pre_g.reshape(1, d), post_g.reshape(1, d),
      w_gm.astype(BF16), w_pn.astype(BF16), w_pm.astype(BF16), w_out.astype(BF16))


def kernel(x, ff1_pre_g, ff1_post_g, ff1_w_gate, ff1_w_up, ff1_w_down, mix_pre_g, mix_post_g, w_in, cmp_pos_k, cmp_w1_k, cmp_w2_k, cmp_pos_v, cmp_w1_v, cmp_w2_v, mla_q_norm_g, mla_w_uq, mla_kv_norm_g, mla_w_ukv, w_proj_nsa, w_proj_mla, w_out, ff2_pre_g, ff2_post_g, ff2_w_gate, ff2_w_up, ff2_w_down):
    b, s, d = x.shape
    assert s % (SEL_LEN * SEL_TOPK) == 0 and s // SEL_LEN <= LANES
    tl = _tiles(s)
    n = b * s

    x1 = _ffn(x.reshape(n, d), ff1_pre_g, ff1_post_g, ff1_w_gate, ff1_w_up, ff1_w_down, tl["tm"])

    (qt, ksa, kwa, vst, vwt, cmp_in, gt, qmt, km, vmt) = _inproj(
        x1.reshape(b, s, d), mix_pre_g, _inproj_weights(w_in, mla_w_uq, mla_w_ukv),
        mla_q_norm_g, mla_kv_norm_g, _rope_tables(s), tl["tm"])

    kc, vct = _compress(cmp_in, cmp_pos_k, cmp_w1_k, cmp_w2_k, cmp_pos_v, cmp_w1_v, cmp_w2_v)
    oct, sbt, act = _nsa_cmp(qt, kc, vct, _overlap_matrix_t(s), _tile_pool_matrix(s, tl["tk_sel"]),
                             tl["tq_cmp"], tl["tq_nsa"])
    act = (act[..., 0] > 0).astype(jnp.int32).reshape(-1)
    y_nsa = _nsa_main(act, qt, sbt, oct, gt, ksa, kwa, vst, vwt, tl["tq_nsa"], tl["tk_sel"])
    y_mla = _mla(qmt, km, vmt, tl["tq_mla"], tl["tq_mla"])

    o_gm = sum((NSA_HEADS * NSA_D, 6 * NSA_GROUPS * NSA_D, 3 * NSA_HEADS, MLA_Q_RANK, MLA_KV_RANK, MLA_ROPE))
    x2 = _merge(x1, y_nsa.reshape(n, -1), y_mla.reshape(n, -1), mix_pre_g, mix_post_g,
                w_in[:, o_gm:], w_proj_nsa, w_proj_mla, w_out, tl["tm"])
    x3 = _ffn(x2, ff2_pre_g, ff2_post_g, ff2_w_gate, ff2_w_up, ff2_w_down, tl["tm"])
    return x3.reshape(b, s, d)
```

```python
import functools

import numpy as np
import jax
import jax.numpy as jnp
from jax import lax
from jax.experimental import pallas as pl
from jax.experimental.pallas import tpu as pltpu

F32 = jnp.float32
BF16 = jnp.bfloat16

EPS = 1e-6
NEG = -1e30
FORCE_SCORE = 1e4
NSA_HEADS = 8
NSA_GROUPS = 2
NSA_HPG = NSA_HEADS // NSA_GROUPS
NSA_D = 64
CMP_LEN = 32
CMP_STRIDE = 16
CMP_HID = 256
END_SHIFT = 4
SEL_LEN = 64
SEL_SHIFT = 6
SEL_TOPK = 16
WINDOW = 512
MLA_HEADS = 8
MLA_NOPE = 64
MLA_ROPE = 32
MLA_V = 64
MLA_Q_RANK = 256
MLA_KV_RANK = 128
ROPE_THETA = 10000.0
LANES = 128
POS_SHIFT = 7
LOG2E = 1.4426950408889634
VMEM_LIMIT = 56 * 1024 * 1024
FFN_CHUNK = 256
V_EXTRA = 16


def _tiles(seq):
    return dict(
        tm=min(1024, seq),
        tq_cmp=min(1024, seq),
        tq_nsa=min(256, seq),
        tk_sel=min(256, seq),
        tq_mla=min(1024, seq),
    )


def _params(*sem):
    return pltpu.CompilerParams(dimension_semantics=sem, vmem_limit_bytes=VMEM_LIMIT)


def _const_spec(shape):
    nd = len(shape)
    return pl.BlockSpec(shape, lambda *_: (0,) * nd, pipeline_mode=pl.Buffered(1))


def _rms(x, g):
    return x * lax.rsqrt(jnp.mean(x * x, axis=-1, keepdims=True) + EPS) * g


def _dot(a, b):
    return jnp.dot(a, b, preferred_element_type=F32)


def _dot_nt(a, b):
    return lax.dot_general(a, b, (((1,), (1,)), ((), ())), preferred_element_type=F32)


def _alibi_slopes():
    return [float(2.0 ** (-8.0 * (i + 1) / NSA_HEADS)) for i in range(NSA_HEADS)]


def _ffn_kernel(x_ref, pre_ref, post_ref, wg_ref, wu_ref, wd_ref, o_ref, *, fc):
    x = x_ref[...]
    h = _rms(x, pre_ref[...]).astype(BF16)
    acc = jnp.zeros(x.shape, F32)
    for c in range(wg_ref.shape[1] // fc):
        gate = _dot(h, wg_ref[:, c * fc:(c + 1) * fc])
        up = _dot(h, wu_ref[:, c * fc:(c + 1) * fc])
        act = (gate * jax.nn.sigmoid(gate) * up).astype(BF16)
        acc = acc + _dot(act, wd_ref[c * fc:(c + 1) * fc, :])
    o_ref[...] = x + 0.5 * _rms(acc, post_ref[...])


def _ffn(x2d, pre_g, post_g, w_gate, w_up, w_down, tm):
    n, d = x2d.shape
    f = w_gate.shape[1]
    fc = FFN_CHUNK if f % FFN_CHUNK == 0 else f
    return pl.pallas_call(
        functools.partial(_ffn_kernel, fc=fc),
        grid=(n // tm,),
        in_specs=[pl.BlockSpec((tm, d), lambda i: (i, 0)),
                  _const_spec((1, d)), _const_spec((1, d)),
                  _const_spec((d, f)), _const_spec((d, f)), _const_spec((f, d))],
        out_specs=pl.BlockSpec((tm, d), lambda i: (i, 0)),
        out_shape=jax.ShapeDtypeStruct((n, d), F32),
        compiler_params=_params("parallel"),
        name="ffn",
    )(x2d, pre_g.reshape(1, d), post_g.reshape(1, d),
      w_gate.astype(BF16), w_up.astype(BF16), w_down.astype(BF16))


_T_KSEL = 0
_T_KWIN = _T_KSEL + 2 * LANES
_T_CMP = _T_KWIN + 2 * LANES
_T_CQ = _T_CMP + 2 * LANES
_T_CKV = _T_CQ + MLA_Q_RANK
_T_KPE = _T_CKV + MLA_KV_RANK
_T_END = _T_KPE + 2 * LANES
_F_QN = 0
_F_VSEL = _F_QN + NSA_HEADS * NSA_D
_F_VWIN = _F_VSEL + LANES
_F_GATE = _F_VWIN + LANES
_F_END = _F_GATE + 32


def _inproj_weights(w_in, w_uq, w_ukv):
    d = w_in.shape[0]
    o_q = 0
    o_kv = o_q + NSA_HEADS * NSA_D
    o_g = o_kv + 6 * NSA_GROUPS * NSA_D
    o_cq = o_g + 3 * NSA_HEADS
    o_ckv = o_cq + MLA_Q_RANK
    o_kpe = o_ckv + MLA_KV_RANK
    zeros = lambda n: jnp.zeros((d, n), w_in.dtype)
    kv = lambda j, g: w_in[:, o_kv + (j * NSA_GROUPS + g) * NSA_D:o_kv + (j * NSA_GROUPS + g + 1) * NSA_D]
    half = MLA_ROPE // 2
    kp1 = w_in[:, o_kpe:o_kpe + half]
    kp2 = w_in[:, o_kpe + half:o_kpe + MLA_ROPE]
    tail = LANES - MLA_NOPE - MLA_ROPE
    cols = [kv(2, 0), zeros(NSA_D), kv(2, 1), zeros(NSA_D),
            kv(4, 0), zeros(NSA_D), kv(4, 1), zeros(NSA_D),
            kv(0, 0), kv(0, 1), kv(1, 0), kv(1, 1),
            w_in[:, o_cq:o_cq + MLA_Q_RANK], w_in[:, o_ckv:o_ckv + MLA_KV_RANK],
            zeros(MLA_NOPE), kp1, kp2, zeros(tail), zeros(MLA_NOPE), kp2, kp1, zeros(tail)]
    w_tok = jnp.concatenate(cols, axis=1).astype(BF16)
    assert w_tok.shape[1] == _T_END
    rows = [w_in[:, o_q:o_q + NSA_HEADS * NSA_D], kv(3, 0), kv(3, 1), kv(5, 0), kv(5, 1)]
    rows += [w_in[:, o_g:o_g + 3 * NSA_HEADS], zeros(_F_END - _F_GATE - 3 * NSA_HEADS)]
    w_feat = jnp.concatenate(rows, axis=1).T.astype(BF16)
    assert w_feat.shape[0] == _F_END

    dq = MLA_NOPE + MLA_ROPE
    zq = lambda n: jnp.zeros((w_uq.shape[0], n), w_uq.dtype)
    qa, qs = [], []
    for h in range(MLA_HEADS):
        nope = w_uq[:, h * dq:h * dq + MLA_NOPE]
        r1 = w_uq[:, h * dq + MLA_NOPE:h * dq + MLA_NOPE + half]
        r2 = w_uq[:, h * dq + MLA_NOPE + half:(h + 1) * dq]
        qa += [nope, r1, r2, zq(tail)]
        qs += [zq(MLA_NOPE), r2, r1, zq(tail)]
    w_q2t = jnp.concatenate(qa + qs, axis=1).T.astype(BF16)

    dkv = MLA_NOPE + MLA_V
    zk = lambda n: jnp.zeros((w_ukv.shape[0], n), w_ukv.dtype)
    ka, va = [], []
    for h in range(MLA_HEADS):
        ka += [w_ukv[:, h * dkv:h * dkv + MLA_NOPE], zk(LANES - MLA_NOPE)]
        va += [w_ukv[:, h * dkv + MLA_NOPE:(h + 1) * dkv]]
    w_k2 = jnp.concatenate(ka, axis=1).astype(BF16)
    w_v2t = jnp.concatenate(va, axis=1).T.astype(BF16)
    return w_tok, w_feat, w_q2t, w_k2, w_v2t


def _rope_tables(seq):
    half = MLA_ROPE // 2
    freqs = jnp.asarray(ROPE_THETA ** (-np.arange(half, dtype=np.float32) / half), F32)
    ang = jnp.arange(seq).astype(F32)[:, None] * freqs[None, :]
    cos, sin = jnp.cos(ang), jnp.sin(ang)
    pad = jnp.zeros((seq, LANES - MLA_NOPE - MLA_ROPE), F32)
    cq = jnp.concatenate([jnp.ones((seq, MLA_NOPE), F32), cos, cos, pad], axis=1)
    ck = jnp.concatenate([jnp.zeros((seq, MLA_NOPE), F32), cos, cos, pad], axis=1)
    sn = jnp.concatenate([jnp.zeros((seq, MLA_NOPE), F32), -sin, sin, pad], axis=1)
    return ck, sn, cq.T, sn.T


def _inproj_kernel(x_ref, g_ref, wt_ref, wf_ref, qg_ref, wq2_ref, kvg_ref, wk2_ref, wv2_ref,
                   ck_ref, sn_ref, cqt_ref, snt_ref,
                   qt_ref, ksa_ref, kwa_ref, vst_ref, vwt_ref, cmp_ref, gt_ref, qmt_ref, km_ref, vmt_ref,
                   *, tm):
    h = _rms(x_ref[...], g_ref[...]).astype(BF16)
    z = _dot(h, wt_ref[...])
    zt = _dot_nt(wf_ref[...], h)
    qt_ref[...] = (zt[_F_QN:_F_VSEL] * NSA_D ** -0.5).astype(BF16)
    ones_rows = (lax.broadcasted_iota(jnp.int32, (V_EXTRA, tm), 0) == 0).astype(BF16)
    for g in range(NSA_GROUPS):
        for ref, base in ((vst_ref, _F_VSEL), (vwt_ref, _F_VWIN)):
            ref[g, :NSA_D] = zt[base + g * NSA_D:base + (g + 1) * NSA_D].astype(BF16)
            ref[g, NSA_D:] = ones_rows
    gt_ref[...] = jax.nn.sigmoid(zt[_F_GATE:_F_GATE + 3 * NSA_HEADS])
    pos = pl.program_id(1) * tm + lax.broadcasted_iota(jnp.int32, (tm, LANES), 0)
    lane = lax.broadcasted_iota(jnp.int32, (tm, LANES), 1)
    onehot = (lane == lax.shift_right_logical(pos, SEL_SHIFT)).astype(BF16)
    in_blk = (pos & (SEL_LEN - 1)).astype(F32)
    pos_lo_hi = jnp.where(lane == NSA_D, pos & (LANES - 1), lax.shift_right_logical(pos, POS_SHIFT)).astype(F32)
    for g in range(NSA_GROUPS):
        ksa_ref[g, :, :LANES] = onehot
        ksa_ref[g, :, LANES:] = jnp.where(
            lane == NSA_D, in_blk, z[:, _T_KSEL + g * LANES:_T_KSEL + (g + 1) * LANES]).astype(BF16)
        kwa_ref[g] = jnp.where(
            (lane == NSA_D) | (lane == NSA_D + 1), pos_lo_hi,
            z[:, _T_KWIN + g * LANES:_T_KWIN + (g + 1) * LANES]).astype(BF16)
    for j in range(4):
        cmp_ref[j] = z[:, _T_CMP + j * NSA_D:_T_CMP + (j + 1) * NSA_D]
    cqn = _rms(z[:, _T_CQ:_T_CKV], qg_ref[...]).astype(BF16)
    q2t = _dot_nt(wq2_ref[...], cqn)
    cqt, snt = cqt_ref[...], snt_ref[...]
    nq = MLA_HEADS * LANES
    scale = (MLA_NOPE + MLA_ROPE) ** -0.5 * LOG2E
    for hh in range(MLA_HEADS):
        qa = q2t[hh * LANES:(hh + 1) * LANES]
        qs = q2t[nq + hh * LANES:nq + (hh + 1) * LANES]
        qmt_ref[hh * LANES:(hh + 1) * LANES] = ((qa * cqt + qs * snt) * scale).astype(BF16)
    ckvn = _rms(z[:, _T_CKV:_T_KPE], kvg_ref[...]).astype(BF16)
    k2 = _dot(ckvn, wk2_ref[...])
    krot = z[:, _T_KPE:_T_KPE + LANES] * ck_ref[...] + z[:, _T_KPE + LANES:_T_END] * sn_ref[...]
    for hh in range(MLA_HEADS):
        km_ref[hh] = (k2[:, hh * LANES:(hh + 1) * LANES] + krot).astype(BF16)
    vt = _dot_nt(wv2_ref[...], ckvn)
    for hh in range(MLA_HEADS):
        vmt_ref[hh, :MLA_V] = vt[hh * MLA_V:(hh + 1) * MLA_V].astype(BF16)
        vmt_ref[hh, MLA_V:] = ones_rows


def _inproj(x1, mix_pre_g, weights, q_norm_g, kv_norm_g, tables, tm):
    b, s, d = x1.shape
    w_tok, w_feat, w_q2t, w_k2, w_v2t = weights
    ck, sn, cqt, snt = tables
    tok = lambda w: pl.BlockSpec((None, tm, w), lambda bi, i: (bi, i, 0))
    feat = lambda r: pl.BlockSpec((None, r, tm), lambda bi, i: (bi, 0, i))
    heads = lambda n, w: pl.BlockSpec((None, n, tm, w), lambda bi, i: (bi, 0, i, 0))
    slabs = lambda n, r: pl.BlockSpec((None, n, r, tm), lambda bi, i: (bi, 0, 0, i))
    tab = pl.BlockSpec((tm, LANES), lambda bi, i: (i, 0))
    tabt = pl.BlockSpec((LANES, tm), lambda bi, i: (0, i))
    sds = jax.ShapeDtypeStruct
    return pl.pallas_call(
        functools.partial(_inproj_kernel, tm=tm),
        grid=(b, s // tm),
        in_specs=[tok(d), _const_spec((1, d)), _const_spec(w_tok.shape), _const_spec(w_feat.shape),
                  _const_spec((1, MLA_Q_RANK)), _const_spec(w_q2t.shape),
                  _const_spec((1, MLA_KV_RANK)), _const_spec(w_k2.shape), _const_spec(w_v2t.shape),
                  tab, tab, tabt, tabt],
        out_specs=[feat(NSA_HEADS * NSA_D), heads(NSA_GROUPS, 2 * LANES), heads(NSA_GROUPS, LANES),
                   slabs(NSA_GROUPS, NSA_D + V_EXTRA), slabs(NSA_GROUPS, NSA_D + V_EXTRA),
                   heads(4, NSA_D), feat(3 * NSA_HEADS),
                   feat(MLA_HEADS * LANES), heads(MLA_HEADS, LANES), slabs(MLA_HEADS, MLA_V + V_EXTRA)],
        out_shape=[sds((b, NSA_HEADS * NSA_D, s), BF16), sds((b, NSA_GROUPS, s, 2 * LANES), BF16),
                   sds((b, NSA_GROUPS, s, LANES), BF16),
                   sds((b, NSA_GROUPS, NSA_D + V_EXTRA, s), BF16), sds((b, NSA_GROUPS, NSA_D + V_EXTRA, s), BF16),
                   sds((b, 4, s, NSA_D), F32), sds((b, 3 * NSA_HEADS, s), F32),
                   sds((b, MLA_HEADS * LANES, s), BF16), sds((b, MLA_HEADS, s, LANES), BF16),
                   sds((b, MLA_HEADS, MLA_V + V_EXTRA, s), BF16)],
        compiler_params=_params("parallel", "parallel"),
        name="inproj",
    )(x1, mix_pre_g.reshape(1, d), w_tok, w_feat, q_norm_g.reshape(1, -1), w_q2t,
      kv_norm_g.reshape(1, -1), w_k2, w_v2t, ck, sn, cqt, snt)


def _compress_kernel(c_ref, pos_ref, w1_ref, w2k_ref, w2vt_ref, kc_ref, vct_ref):
    nc = c_ref.shape[1]
    lane = lax.broadcasted_iota(jnp.int32, (nc, NSA_D), 1)
    blk = lax.broadcasted_iota(jnp.int32, (nc, NSA_D), 0)
    end_cols = jnp.where(lane == 0, lax.shift_right_logical(blk, END_SHIFT),
                         jnp.where(lane == 1, blk & ((1 << END_SHIFT) - 1), 0)).astype(F32)
    for j in range(4):
        kv, g = divmod(j, NSA_GROUPS)
        c = c_ref[j]
        a0 = _dot((c + pos_ref[kv, 0]).astype(BF16), w1_ref[kv, 0])
        a1 = _dot((c + pos_ref[kv, 1]).astype(BF16), w1_ref[kv, 1])
        pre = a0 + pltpu.roll(a1, nc - 1, 0)
        hid = jax.nn.gelu(pre).astype(BF16)
        if kv == 0:
            kc_ref[g] = jnp.concatenate([_dot(hid, w2k_ref[...]), end_cols], axis=1).astype(BF16)
        else:
            vct_ref[g * NSA_D:(g + 1) * NSA_D] = _dot_nt(w2vt_ref[...], hid).astype(BF16)


def _compress(cmp_in, pos_k, w1_k, w2_k, pos_v, w1_v, w2_v):
    b, _, s, dk = cmp_in.shape
    nc = s // CMP_STRIDE
    kdim = CMP_STRIDE * dk
    c4 = cmp_in.reshape(b, 4, nc, kdim)
    pos = jnp.stack([pos_k.reshape(2, 1, kdim), pos_v.reshape(2, 1, kdim)])
    w1 = jnp.stack([w1_k.reshape(2, kdim, CMP_HID), w1_v.reshape(2, kdim, CMP_HID)]).astype(BF16)
    return pl.pallas_call(
        _compress_kernel,
        grid=(b,),
        in_specs=[pl.BlockSpec((None, 4, nc, kdim), lambda bi: (bi, 0, 0, 0)),
                  _const_spec(pos.shape), _const_spec(w1.shape),
                  _const_spec((CMP_HID, dk)), _const_spec((dk, CMP_HID))],
        out_specs=[pl.BlockSpec((None, NSA_GROUPS, nc, LANES), lambda bi: (bi, 0, 0, 0)),
                   pl.BlockSpec((None, LANES, nc), lambda bi: (bi, 0, 0))],
        out_shape=[jax.ShapeDtypeStruct((b, NSA_GROUPS, nc, LANES), BF16),
                   jax.ShapeDtypeStruct((b, LANES, nc), BF16)],
        compiler_params=_params("parallel"),
        name="compress",
    )(c4, pos, w1, w2_k.astype(BF16), w2_v.T.astype(BF16))


def _nsa_cmp_kernel(qt_ref, kc_ref, vct_ref, ovt_ref, pool_ref, oct_ref, sbt_ref, act_ref, *, tq, tq_main):
    nc = kc_ref.shape[1]
    t = pl.program_id(1) * tq + lax.broadcasted_iota(jnp.int32, (1, tq), 1)
    cmp_end = lax.broadcasted_iota(jnp.int32, (nc, 1), 0) * CMP_STRIDE + (CMP_LEN - 1)
    valid = cmp_end <= t
    any_valid = t >= CMP_LEN - 1
    slopes = _alibi_slopes()
    row = lax.broadcasted_iota(jnp.int32, (NSA_D, tq), 0)
    blk = lax.broadcasted_iota(jnp.int32, (LANES, tq), 0)
    blkf = blk.astype(F32)
    cur = lax.shift_right_logical(t, SEL_SHIFT)
    forced = (blk == 0) | (blk == cur) | (blk == cur - 1)
    for g in range(NSA_GROUPS):
        vo = jnp.concatenate([vct_ref[g * NSA_D:(g + 1) * NSA_D], ovt_ref[...]], axis=0)
        imp = jnp.zeros((LANES, tq), F32)
        for hh in range(NSA_HPG):
            h = g * NSA_HPG + hh
            hi_w = slopes[h] * (CMP_STRIDE << END_SHIFT)
            tail = jnp.where(row == 0, hi_w, jnp.where(row == 1, slopes[h] * CMP_STRIDE, 0.0)).astype(BF16)
            q = jnp.concatenate([qt_ref[h * NSA_D:(h + 1) * NSA_D], tail], axis=0)
            s = jnp.where(valid, _dot(kc_ref[g], q), NEG)
            e = jnp.exp(s - jnp.max(s, axis=0, keepdims=True))
            norm = jnp.where(any_valid, 1.0 / jnp.maximum(jnp.sum(e, axis=0, keepdims=True), 1e-30), 0.0)
            r = _dot(vo, e.astype(BF16)) * norm
            oct_ref[h * NSA_D:(h + 1) * NSA_D] = r[:NSA_D]
            imp = imp + r[NSA_D:]
        work = jnp.where(blk <= cur, jnp.where(forced, FORCE_SCORE, imp), NEG)
        chosen = jnp.zeros((LANES, tq), jnp.bool_)
        for _ in range(SEL_TOPK):
            top = jnp.max(work, axis=0, keepdims=True)
            idx = jnp.min(jnp.where(work == top, blkf, float(LANES)), axis=0, keepdims=True)
            hit = blkf == idx
            chosen = chosen | hit
            work = jnp.where(hit, -jnp.inf, work)
        sbt_ref[g] = jnp.where(chosen, 0.0, NEG).astype(BF16)
        used = _dot(pool_ref[...], chosen.astype(BF16))
        for j in range(tq // tq_main):
            any_q = jnp.max(used[:, j * tq_main:(j + 1) * tq_main], axis=1, keepdims=True)
            act_ref[g, j] = jnp.broadcast_to(any_q, act_ref.shape[2:])


def _nsa_cmp(qt, kc, vct, overlap_t, pool, tq, tq_main):
    b, _, s = qt.shape
    nc = kc.shape[2]
    nkt = pool.shape[0]
    return pl.pallas_call(
        functools.partial(_nsa_cmp_kernel, tq=tq, tq_main=tq_main),
        grid=(b, s // tq),
        in_specs=[pl.BlockSpec((None, NSA_HEADS * NSA_D, tq), lambda bi, i: (bi, 0, i)),
                  pl.BlockSpec((None, NSA_GROUPS, nc, LANES), lambda bi, i: (bi, 0, 0, 0)),
                  pl.BlockSpec((None, LANES, nc), lambda bi, i: (bi, 0, 0)),
                  _const_spec(overlap_t.shape), _const_spec(pool.shape)],
        out_specs=[pl.BlockSpec((None, NSA_HEADS * NSA_D, tq), lambda bi, i: (bi, 0, i)),
                   pl.BlockSpec((None, NSA_GROUPS, LANES, tq), lambda bi, i: (bi, 0, 0, i)),
                   pl.BlockSpec((None, NSA_GROUPS, tq // tq_main, nkt, LANES), lambda bi, i: (bi, 0, i, 0, 0))],
        out_shape=[jax.ShapeDtypeStruct((b, NSA_HEADS * NSA_D, s), F32),
                   jax.ShapeDtypeStruct((b, NSA_GROUPS, LANES, s), BF16),
                   jax.ShapeDtypeStruct((b, NSA_GROUPS, s // tq_main, nkt, LANES), F32)],
        compiler_params=_params("parallel", "parallel"),
        name="nsa_cmp",
    )(qt, kc, vct, overlap_t, pool)


def _tile_pool_matrix(seq, tk_sel):
    per_tile = tk_sel // SEL_LEN
    nkt = -(-(seq // tk_sel) // 16) * 16
    pool = np.zeros((nkt, LANES), np.float32)
    for c in range(seq // SEL_LEN):
        pool[c // per_tile, c] = 1.0
    return jnp.asarray(pool, BF16)


def _overlap_matrix_t(seq):
    n_c = (seq - CMP_LEN) // CMP_STRIDE + 1
    n_sel = seq // SEL_LEN
    c0 = np.arange(n_c) * CMP_STRIDE
    s0 = np.arange(n_sel) * SEL_LEN
    ov = np.clip(np.minimum(c0[:, None] + CMP_LEN, s0[None, :] + SEL_LEN)
                 - np.maximum(c0[:, None], s0[None, :]), 0, None) / CMP_LEN
    full = np.zeros((LANES, seq // CMP_STRIDE), np.float32)
    full[:n_sel, :n_c] = ov.T
    return jnp.asarray(full, BF16)


def _flash_pipelined(n_tiles, first_scores, scores, values, dv, s_buf, p_buf, exp=jnp.exp, n_static=0):
    chains = range(len(first_scores))
    nq = first_scores[0].shape[1]
    state = tuple((jnp.full((1, nq), NEG, F32), jnp.zeros((dv + V_EXTRA, nq), F32)) for _ in chains)
    for c in chains:
        s_buf[0, c] = first_scores[c]
        p_buf[1, c] = jnp.zeros(p_buf.shape[2:], BF16)
    colmax = tuple(jnp.max(first_scores[c], axis=0, keepdims=True) for c in chains)
    ones = tuple(jnp.ones((1, nq), F32) for _ in chains)

    static = isinstance(n_tiles, int)

    def step(i, carry, cur):
        st, alphas, cmax = carry
        nxt = 1 - cur
        vts = values(max(i - 1, 0) if isinstance(i, int) else jnp.maximum(i - 1, 0))
        s_next = None if static and i + 1 == n_tiles else scores(i + 1)
        new_st, new_alpha, new_cmax = [], [], []
        for c in chains:
            m, acc = st[c]
            acc = alphas[c] * acc + _dot(vts[c], p_buf[nxt, c])
            m_new = jnp.maximum(m, cmax[c])
            p_buf[cur, c] = exp(s_buf[cur, c] - m_new).astype(BF16)
            new_st.append((m_new, acc))
            new_alpha.append(exp(m - m_new))
            if s_next is not None:
                s_buf[nxt, c] = s_next[c]
                new_cmax.append(jnp.max(s_next[c], axis=0, keepdims=True))
        return tuple(new_st), tuple(new_alpha), tuple(new_cmax)

    def steps(first, count, carry):
        for u in range(count):
            carry = step(first + u, carry, u % 2)
        return carry

    carry = (state, ones, colmax)
    if static:
        st, alphas, _ = steps(0, n_tiles, carry)
        vts = values(n_tiles - 1)
    else:
        carry = steps(0, n_static, carry)
        n_pairs = n_tiles // 2
        carry = lax.fori_loop(n_static // 2, n_pairs, lambda j, c: steps(2 * j, 2, c), carry)
        st, alphas, _ = lax.fori_loop(2 * n_pairs, n_tiles, lambda i, c: steps(i, 1, c), carry)
        vts = values(jnp.maximum(n_tiles - 1, 0))
    last = (n_tiles + 1) & 1
    outs = []
    for c, (_, acc) in enumerate(st):
        acc = alphas[c] * acc + _dot(vts[c], p_buf[last, c])
        outs.append(acc[:dv] * (1.0 / acc[dv:dv + 1]))
    return outs


def _nsa_main_kernel(act_ref, qt_ref, sbt_ref, oct_ref, gt_ref, ksa_ref, kwa_ref, vst_ref, vwt_ref, y_ref,
                     s_buf, p_buf, tiles_ref, *, tq, tk_sel, act_tiles):
    assert tq == tk_sel
    t0 = pl.program_id(1) * tq
    nq = NSA_HPG * tq
    slopes = _alibi_slopes()
    t_lane = jnp.concatenate([t0 + lax.broadcasted_iota(jnp.int32, (1, tq), 1)] * NSA_HPG, axis=1)
    row = lax.broadcasted_iota(jnp.int32, (NSA_D, nq), 0)
    blk_rel = (lax.broadcasted_iota(jnp.int32, (LANES, tq), 0)
               - lax.shift_right_logical(t0, SEL_SHIFT)).astype(F32)

    q_sel, q_win = [], []
    for g in range(NSA_GROUPS):
        heads = range(g * NSA_HPG, (g + 1) * NSA_HPG)
        sl = jnp.concatenate([jnp.full((1, tq), slopes[h], F32) for h in heads], axis=1)
        qg = jnp.concatenate(
            [qt_ref[h * NSA_D:(h + 1) * NSA_D] for h in heads], axis=1)
        tail = jnp.where(row == 0, sl, 0.0).astype(BF16)
        tail_win = jnp.where(row == 0, sl, jnp.where(row == 1, sl * LANES, 0.0)).astype(BF16)
        sb = sbt_ref[g].astype(F32)
        bias = jnp.concatenate([sb + (slopes[h] * SEL_LEN) * blk_rel for h in heads], axis=1)
        q_sel.append(jnp.concatenate([bias.astype(BF16), qg, tail], axis=0))
        q_win.append(jnp.concatenate([qg, tail_win], axis=0))

    groups = range(NSA_GROUPS)

    def sel_scores(kt, g):
        k0 = pl.multiple_of(kt * tk_sel, tk_sel)
        return _dot(ksa_ref[g, pl.ds(k0, tk_sel), :], q_sel[g])

    def sel_values(kt, g):
        k0 = pl.multiple_of(kt * tk_sel, tk_sel)
        return vst_ref[g, :, pl.ds(k0, tk_sel)]

    n_full = (t0 + 1) // tk_sel
    n_kt = tiles_ref.shape[1]
    counts, spare = [], []
    for g in groups:
        base = ((pl.program_id(0) * NSA_GROUPS + g) * pl.num_programs(1) + pl.program_id(1)) * act_tiles
        for k in range(n_kt):
            tiles_ref[g, k] = 0

        def scan(kt, carry, g=g, base=base):
            cnt, unused = carry
            used = act_ref[base + kt]
            tiles_ref[g, cnt] = kt
            return cnt + used, jnp.where(used == 0, kt, unused)

        cnt, unused = lax.fori_loop(0, n_full, scan, (jnp.int32(0), jnp.int32(0)))
        counts.append(cnt)
        spare.append(unused)

    def tile_of(k, g):
        return jnp.where(k < counts[g], tiles_ref[g, jnp.minimum(k, n_kt - 1)], spare[g])

    diag = t0 // tk_sel
    causal = diag * tk_sel + lax.broadcasted_iota(jnp.int32, (tk_sel, 1), 0) <= t_lane
    at = lambda k, g: jnp.where(k == 0, diag, tile_of(jnp.maximum(k - 1, 0), g))
    sel = _flash_pipelined(jnp.maximum(counts[0], counts[1]) + 1,
                           [jnp.where(causal, sel_scores(diag, g), NEG) for g in groups],
                           lambda k: [sel_scores(tile_of(k - 1, g), g) for g in groups],
                           lambda k: [sel_values(at(k, g), g) for g in groups], NSA_D, s_buf, p_buf)

    assert WINDOW % tq == 0
    start = pl.multiple_of(jnp.maximum(t0 - WINDOW, 0), tq)
    n_wt = WINDOW // tq + 1

    def win_scores(k, g):
        w = n_wt - 1 - k
        k0 = pl.multiple_of(start + w * tq, tq)
        key = k0 + lax.broadcasted_iota(jnp.int32, (tq, 1), 0)
        ok = key <= t_lane
        if w in (0, n_wt - 1):
            ok = ok & (key > t_lane - WINDOW)
        return jnp.where(ok, _dot(kwa_ref[g, pl.ds(k0, tq), :], q_win[g]), NEG)

    def win_values(k, g):
        k0 = pl.multiple_of(start + (n_wt - 1 - k) * tq, tq)
        return vwt_ref[g, :, pl.ds(k0, tq)]

    win = _flash_pipelined(n_wt, [win_scores(0, g) for g in groups],
                           lambda k: [win_scores(k, g) for g in groups],
                           lambda k: [win_values(k, g) for g in groups], NSA_D, s_buf, p_buf)

    gates = gt_ref[...]
    outs = []
    for h in range(NSA_HEADS):
        g, hh = divmod(h, NSA_HPG)
        cols = slice(hh * tq, (hh + 1) * tq)
        outs.append(gates[3 * h:3 * h + 1] * oct_ref[h * NSA_D:(h + 1) * NSA_D]
                    + gates[3 * h + 1:3 * h + 2] * sel[g][:, cols] + gates[3 * h + 2:3 * h + 3] * win[g][:, cols])
    y_ref[...] = jnp.concatenate(outs, axis=0).T.astype(BF16)


def _nsa_main(act, qt, sbt, oct, gt, ksa, kwa, vst, vwt, tq, tk_sel):
    b, _, s = qt.shape
    act_tiles = act.shape[0] // (b * NSA_GROUPS * (s // tq))
    feat = lambda r: pl.BlockSpec((None, r, tq), lambda bi, i, _: (bi, 0, i))
    grid_spec = pltpu.PrefetchScalarGridSpec(
        num_scalar_prefetch=1,
        grid=(b, s // tq),
        in_specs=[feat(NSA_HEADS * NSA_D),
                  pl.BlockSpec((None, NSA_GROUPS, LANES, tq), lambda bi, i, _: (bi, 0, 0, i)),
                  feat(NSA_HEADS * NSA_D), feat(3 * NSA_HEADS),
                  pl.BlockSpec((None, NSA_GROUPS, s, 2 * LANES), lambda bi, i, _: (bi, 0, 0, 0)),
                  pl.BlockSpec((None, NSA_GROUPS, s, LANES), lambda bi, i, _: (bi, 0, 0, 0)),
                  pl.BlockSpec((None, NSA_GROUPS, NSA_D + V_EXTRA, s), lambda bi, i, _: (bi, 0, 0, 0)),
                  pl.BlockSpec((None, NSA_GROUPS, NSA_D + V_EXTRA, s), lambda bi, i, _: (bi, 0, 0, 0))],
        out_specs=pl.BlockSpec((None, tq, NSA_HEADS * NSA_D), lambda bi, i, _: (bi, i, 0)),
        scratch_shapes=[pltpu.VMEM((2, NSA_GROUPS, tk_sel, NSA_HPG * tq), F32),
                        pltpu.VMEM((2, NSA_GROUPS, tk_sel, NSA_HPG * tq), BF16),
                        pltpu.SMEM((NSA_GROUPS, act_tiles), jnp.int32)])
    return pl.pallas_call(
        functools.partial(_nsa_main_kernel, tq=tq, tk_sel=tk_sel, act_tiles=act_tiles),
        grid_spec=grid_spec,
        out_shape=jax.ShapeDtypeStruct((b, s, NSA_HEADS * NSA_D), BF16),
        compiler_params=_params("parallel", "arbitrary"),
        name="nsa_main",
    )(act, qt, sbt, oct, gt, ksa, kwa, vst, vwt)


def _mla_kernel(qt_ref, k_ref, vt_ref, y_ref, s_buf, p_buf, *, tq, tk):
    t0 = pl.program_id(2) * tq
    t_lane = t0 + lax.broadcasted_iota(jnp.int32, (1, tq), 1)
    heads = range(2)
    qts = [qt_ref[hh * LANES:(hh + 1) * LANES] for hh in heads]

    def scores(kt):
        k0 = pl.multiple_of(kt * tk, tk)
        return [_dot(k_ref[hh, pl.ds(k0, tk), :], qts[hh]) for hh in heads]

    def values(kt):
        k0 = pl.multiple_of(kt * tk, tk)
        return [vt_ref[hh, :, pl.ds(k0, tk)] for hh in heads]

    assert tq == 2 * tk
    first = 2 * pl.program_id(2)

    def at(k):
        return first + k if isinstance(k, int) and k < 2 else jnp.where(k < 2, first + k, k - 2)

    def masked_scores(k):
        if isinstance(k, int) and k < 2:
            causal = (first + k) * tk + lax.broadcasted_iota(jnp.int32, (tk, 1), 0) <= t_lane
            return [jnp.where(causal, s, NEG) for s in scores(first + k)]
        return scores(k - 2)

    st = _flash_pipelined(first + 2, masked_scores(0), masked_scores, lambda k: values(at(k)),
                          MLA_V, s_buf, p_buf, exp=jnp.exp2, n_static=2)
    o = jnp.concatenate(st, axis=0)
    y_ref[...] = o.T.astype(BF16)


def _mla(qmt, km, vmt, tq, tk):
    b, h, s, _ = km.shape
    k5 = km.reshape(b, h // 2, 2, s, LANES)
    vmt = vmt.reshape(b, h // 2, 2, MLA_V + V_EXTRA, s)
    return pl.pallas_call(
        functools.partial(_mla_kernel, tq=tq, tk=tk),
        grid=(b, h // 2, s // tq),
        in_specs=[pl.BlockSpec((None, 2 * LANES, tq), lambda bi, hp, i: (bi, hp, i)),
                  pl.BlockSpec((None, None, 2, s, LANES), lambda bi, hp, i: (bi, hp, 0, 0, 0)),
                  pl.BlockSpec((None, None, 2, MLA_V + V_EXTRA, s), lambda bi, hp, i: (bi, hp, 0, 0, 0))],
        out_specs=pl.BlockSpec((None, tq, 2 * MLA_V), lambda bi, hp, i: (bi, i, hp)),
        out_shape=jax.ShapeDtypeStruct((b, s, h * MLA_V), BF16),
        scratch_shapes=[pltpu.VMEM((2, 2, tk, tq), F32), pltpu.VMEM((2, 2, tk, tq), BF16)],
        compiler_params=_params("parallel", "parallel", "arbitrary"),
        name="mla",
    )(qmt, k5, vmt)


def _merge_kernel(x_ref, yn_ref, ym_ref, pre_ref, post_ref, wgm_ref, wpn_ref, wpm_ref, wo_ref, o_ref):
    x = x_ref[...]
    d = x.shape[1]
    h = _rms(x, pre_ref[...]).astype(BF16)
    gm = jax.nn.sigmoid(_dot(h, wgm_ref[...]))
    merged = gm[:, :d] * _dot(yn_ref[...], wpn_ref[...]) + gm[:, d:] * _dot(ym_ref[...], wpm_ref[...])
    y = _dot(merged.astype(BF16), wo_ref[...])
    o_ref[...] = x + _rms(y, post_ref[...])


def _merge(x1, y_nsa, y_mla, pre_g, post_g, w_gm, w_pn, w_pm, w_out, tm):
    n, d = x1.shape
    tok = lambda w: pl.BlockSpec((tm, w), lambda i: (i, 0))
    return pl.pallas_call(
        _merge_kernel,
        grid=(n // tm,),
        in_specs=[tok(d), tok(y_nsa.shape[1]), tok(y_mla.shape[1]), _const_spec((1, d)), _const_spec((1, d)),
                  _const_spec(w_gm.shape), _const_spec(w_pn.shape), _const_spec(w_pm.shape),
                  _const_spec(w_out.shape)],
        out_specs=tok(d),
        out_shape=jax.ShapeDtypeStruct((n, d), F32),
        compiler_params=_params("parallel"),
        name="merge",
    )(x1, y_nsa, y_mla, pre_g.reshape(1, d), post_g.reshape(1, d),
      w_gm.astype(BF16), w_pn.astype(BF16), w_pm.astype(BF16), w_out.astype(BF16))


def kernel(x, ff1_pre_g, ff1_post_g, ff1_w_gate, ff1_w_up, ff1_w_down, mix_pre_g, mix_post_g, w_in, cmp_pos_k, cmp_w1_k, cmp_w2_k, cmp_pos_v, cmp_w1_v, cmp_w2_v, mla_q_norm_g, mla_w_uq, mla_kv_norm_g, mla_w_ukv, w_proj_nsa, w_proj_mla, w_out, ff2_pre_g, ff2_post_g, ff2_w_gate, ff2_w_up, ff2_w_down):
    b, s, d = x.shape
    assert s % (SEL_LEN * SEL_TOPK) == 0 and s // SEL_LEN <= LANES
    tl = _tiles(s)
    n = b * s

    x1 = _ffn(x.reshape(n, d), ff1_pre_g, ff1_post_g, ff1_w_gate, ff1_w_up, ff1_w_down, tl["tm"])

    (qt, ksa, kwa, vst, vwt, cmp_in, gt, qmt, km, vmt) = _inproj(
        x1.reshape(b, s, d), mix_pre_g, _inproj_weights(w_in, mla_w_uq, mla_w_ukv),
        mla_q_norm_g, mla_kv_norm_g, _rope_tables(s), tl["tm"])

    kc, vct = _compress(cmp_in, cmp_pos_k, cmp_w1_k, cmp_w2_k, cmp_pos_v, cmp_w1_v, cmp_w2_v)
    oct, sbt, act = _nsa_cmp(qt, kc, vct, _overlap_matrix_t(s), _tile_pool_matrix(s, tl["tk_sel"]),
                             tl["tq_cmp"], tl["tq_nsa"])
    act = (act[..., 0] > 0).astype(jnp.int32).reshape(-1)
    y_nsa = _nsa_main(act, qt, sbt, oct, gt, ksa, kwa, vst, vwt, tl["tq_nsa"], tl["tk_sel"])
    y_mla = _mla(qmt, km, vmt, tl["tq_mla"], tl["tq_mla"] // 2)

    o_gm = sum((NSA_HEADS * NSA_D, 6 * NSA_GROUPS * NSA_D, 3 * NSA_HEADS, MLA_Q_RANK, MLA_KV_RANK, MLA_ROPE))
    x2 = _merge(x1, y_nsa.reshape(n, -1), y_mla.reshape(n, -1), mix_pre_g, mix_post_g,
                w_in[:, o_gm:], w_proj_nsa, w_proj_mla, w_out, tl["tm"])
    x3 = _ffn(x2, ff2_pre_g, ff2_post_g, ff2_w_gate, ff2_w_up, ff2_w_down, tl["tm"])
    return x3.reshape(b, s, d)
```

```python
import functools

import numpy as np
import jax
import jax.numpy as jnp
from jax import lax
from jax.experimental import pallas as pl
from jax.experimental.pallas import tpu as pltpu

F32 = jnp.float32
BF16 = jnp.bfloat16

EPS = 1e-6
NEG = -1e30
FORCE_SCORE = 1e4
NSA_HEADS = 8
NSA_GROUPS = 2
NSA_HPG = NSA_HEADS // NSA_GROUPS
NSA_D = 64
CMP_LEN = 32
CMP_STRIDE = 16
CMP_HID = 256
END_SHIFT = 4
SEL_LEN = 64
SEL_SHIFT = 6
SEL_TOPK = 16
WINDOW = 512
MLA_HEADS = 8
MLA_NOPE = 64
MLA_ROPE = 32
MLA_V = 64
MLA_Q_RANK = 256
MLA_KV_RANK = 128
ROPE_THETA = 10000.0
LANES = 128
POS_SHIFT = 7
LOG2E = 1.4426950408889634
VMEM_LIMIT = 56 * 1024 * 1024
FFN_CHUNK = 256
V_EXTRA = 16


def _tiles(seq):
    return dict(
        tm=min(1024, seq),
        tq_cmp=min(1024, seq),
        tq_nsa=min(256, seq),
        tk_sel=min(256, seq),
        tq_mla=min(1024, seq),
    )


def _params(*sem):
    return pltpu.CompilerParams(dimension_semantics=sem, vmem_limit_bytes=VMEM_LIMIT)


def _const_spec(shape):
    nd = len(shape)
    return pl.BlockSpec(shape, lambda *_: (0,) * nd, pipeline_mode=pl.Buffered(1))


def _rms(x, g):
    return x * lax.rsqrt(jnp.mean(x * x, axis=-1, keepdims=True) + EPS) * g


def _dot(a, b):
    return jnp.dot(a, b, preferred_element_type=F32)


def _dot_nt(a, b):
    return lax.dot_general(a, b, (((1,), (1,)), ((), ())), preferred_element_type=F32)


def _alibi_slopes():
    return [float(2.0 ** (-8.0 * (i + 1) / NSA_HEADS)) for i in range(NSA_HEADS)]


def _ffn_kernel(x_ref, pre_ref, post_ref, wg_ref, wu_ref, wd_ref, o_ref, *, fc):
    x = x_ref[...]
    h = _rms(x, pre_ref[...]).astype(BF16)
    acc = jnp.zeros(x.shape, F32)
    for c in range(wg_ref.shape[1] // fc):
        gate = _dot(h, wg_ref[:, c * fc:(c + 1) * fc])
        up = _dot(h, wu_ref[:, c * fc:(c + 1) * fc])
        act = (gate * jax.nn.sigmoid(gate) * up).astype(BF16)
        acc = acc + _dot(act, wd_ref[c * fc:(c + 1) * fc, :])
    o_ref[...] = x + 0.5 * _rms(acc, post_ref[...])


def _ffn(x2d, pre_g, post_g, w_gate, w_up, w_down, tm):
    n, d = x2d.shape
    f = w_gate.shape[1]
    fc = FFN_CHUNK if f % FFN_CHUNK == 0 else f
    return pl.pallas_call(
        functools.partial(_ffn_kernel, fc=fc),
        grid=(n // tm,),
        in_specs=[pl.BlockSpec((tm, d), lambda i: (i, 0)),
                  _const_spec((1, d)), _const_spec((1, d)),
                  _const_spec((d, f)), _const_spec((d, f)), _const_spec((f, d))],
        out_specs=pl.BlockSpec((tm, d), lambda i: (i, 0)),
        out_shape=jax.ShapeDtypeStruct((n, d), F32),
        compiler_params=_params("parallel"),
        name="ffn",
    )(x2d, pre_g.reshape(1, d), post_g.reshape(1, d),
      w_gate.astype(BF16), w_up.astype(BF16), w_down.astype(BF16))


_T_KSEL = 0
_T_KWIN = _T_KSEL + 2 * LANES
_T_CMP = _T_KWIN + 2 * LANES
_T_CQ = _T_CMP + 2 * LANES
_T_CKV = _T_CQ + MLA_Q_RANK
_T_KPE = _T_CKV + MLA_KV_RANK
_T_END = _T_KPE + 2 * LANES
_F_QN = 0
_F_VSEL = _F_QN + NSA_HEADS * NSA_D
_F_VWIN = _F_VSEL + LANES
_F_GATE = _F_VWIN + LANES
_F_END = _F_GATE + 32


def _inproj_weights(w_in, w_uq, w_ukv):
    d = w_in.shape[0]
    o_q = 0
    o_kv = o_q + NSA_HEADS * NSA_D
    o_g = o_kv + 6 * NSA_GROUPS * NSA_D
    o_cq = o_g + 3 * NSA_HEADS
    o_ckv = o_cq + MLA_Q_RANK
    o_kpe = o_ckv + MLA_KV_RANK
    zeros = lambda n: jnp.zeros((d, n), w_in.dtype)
    kv = lambda j, g: w_in[:, o_kv + (j * NSA_GROUPS + g) * NSA_D:o_kv + (j * NSA_GROUPS + g + 1) * NSA_D]
    half = MLA_ROPE // 2
    kp1 = w_in[:, o_kpe:o_kpe + half]
    kp2 = w_in[:, o_kpe + half:o_kpe + MLA_ROPE]
    tail = LANES - MLA_NOPE - MLA_ROPE
    cols = [kv(2, 0), zeros(NSA_D), kv(2, 1), zeros(NSA_D),
            kv(4, 0), zeros(NSA_D), kv(4, 1), zeros(NSA_D),
            kv(0, 0), kv(0, 1), kv(1, 0), kv(1, 1),
            w_in[:, o_cq:o_cq + MLA_Q_RANK], w_in[:, o_ckv:o_ckv + MLA_KV_RANK],
            zeros(MLA_NOPE), kp1, kp2, zeros(tail), zeros(MLA_NOPE), kp2, kp1, zeros(tail)]
    w_tok = jnp.concatenate(cols, axis=1).astype(BF16)
    assert w_tok.shape[1] == _T_END
    rows = [w_in[:, o_q:o_q + NSA_HEADS * NSA_D], kv(3, 0), kv(3, 1), kv(5, 0), kv(5, 1)]
    rows += [w_in[:, o_g:o_g + 3 * NSA_HEADS], zeros(_F_END - _F_GATE - 3 * NSA_HEADS)]
    w_feat = jnp.concatenate(rows, axis=1).T.astype(BF16)
    assert w_feat.shape[0] == _F_END

    dq = MLA_NOPE + MLA_ROPE
    zq = lambda n: jnp.zeros((w_uq.shape[0], n), w_uq.dtype)
    qa, qs = [], []
    for h in range(MLA_HEADS):
        nope = w_uq[:, h * dq:h * dq + MLA_NOPE]
        r1 = w_uq[:, h * dq + MLA_NOPE:h * dq + MLA_NOPE + half]
        r2 = w_uq[:, h * dq + MLA_NOPE + half:(h + 1) * dq]
        qa += [nope, r1, r2, zq(tail)]
        qs += [zq(MLA_NOPE), r2, r1, zq(tail)]
    w_q2t = jnp.concatenate(qa + qs, axis=1).T.astype(BF16)

    dkv = MLA_NOPE + MLA_V
    zk = lambda n: jnp.zeros((w_ukv.shape[0], n), w_ukv.dtype)
    ka, va = [], []
    for h in range(MLA_HEADS):
        ka += [w_ukv[:, h * dkv:h * dkv + MLA_NOPE], zk(LANES - MLA_NOPE)]
        va += [w_ukv[:, h * dkv + MLA_NOPE:(h + 1) * dkv]]
    w_k2 = jnp.concatenate(ka, axis=1).astype(BF16)
    w_v2t = jnp.concatenate(va, axis=1).T.astype(BF16)
    return w_tok, w_feat, w_q2t, w_k2, w_v2t


def _rope_tables(seq):
    half = MLA_ROPE // 2
    freqs = jnp.asarray(ROPE_THETA ** (-np.arange(half, dtype=np.float32) / half), F32)
    ang = jnp.arange(seq).astype(F32)[:, None] * freqs[None, :]
    cos, sin = jnp.cos(ang), jnp.sin(ang)
    pad = jnp.zeros((seq, LANES - MLA_NOPE - MLA_ROPE), F32)
    cq = jnp.concatenate([jnp.ones((seq, MLA_NOPE), F32), cos, cos, pad], axis=1)
    ck = jnp.concatenate([jnp.zeros((seq, MLA_NOPE), F32), cos, cos, pad], axis=1)
    sn = jnp.concatenate([jnp.zeros((seq, MLA_NOPE), F32), -sin, sin, pad], axis=1)
    return ck, sn, cq.T, sn.T


def _inproj_kernel(x_ref, g_ref, wt_ref, wf_ref, qg_ref, wq2_ref, kvg_ref, wk2_ref, wv2_ref,
                   ck_ref, sn_ref, cqt_ref, snt_ref,
                   qt_ref, ksa_ref, kwa_ref, vst_ref, vwt_ref, cmp_ref, gt_ref, qmt_ref, km_ref, vmt_ref,
                   *, tm):
    h = _rms(x_ref[...], g_ref[...]).astype(BF16)
    z = _dot(h, wt_ref[...])
    zt = _dot_nt(wf_ref[...], h)
    qt_ref[...] = (zt[_F_QN:_F_VSEL] * NSA_D ** -0.5).astype(BF16)
    ones_rows = (lax.broadcasted_iota(jnp.int32, (V_EXTRA, tm), 0) == 0).astype(BF16)
    for g in range(NSA_GROUPS):
        for ref, base in ((vst_ref, _F_VSEL), (vwt_ref, _F_VWIN)):
            ref[g, :NSA_D] = zt[base + g * NSA_D:base + (g + 1) * NSA_D].astype(BF16)
            ref[g, NSA_D:] = ones_rows
    gt_ref[...] = jax.nn.sigmoid(zt[_F_GATE:_F_GATE + 3 * NSA_HEADS])
    pos = pl.program_id(1) * tm + lax.broadcasted_iota(jnp.int32, (tm, LANES), 0)
    lane = lax.broadcasted_iota(jnp.int32, (tm, LANES), 1)
    onehot = (lane == lax.shift_right_logical(pos, SEL_SHIFT)).astype(BF16)
    in_blk = (pos & (SEL_LEN - 1)).astype(F32)
    pos_lo_hi = jnp.where(lane == NSA_D, pos & (LANES - 1), lax.shift_right_logical(pos, POS_SHIFT)).astype(F32)
    for g in range(NSA_GROUPS):
        ksa_ref[g, :, :LANES] = onehot
        ksa_ref[g, :, LANES:] = jnp.where(
            lane == NSA_D, in_blk, z[:, _T_KSEL + g * LANES:_T_KSEL + (g + 1) * LANES]).astype(BF16)
        kwa_ref[g] = jnp.where(
            (lane == NSA_D) | (lane == NSA_D + 1), pos_lo_hi,
            z[:, _T_KWIN + g * LANES:_T_KWIN + (g + 1) * LANES]).astype(BF16)
    for j in range(4):
        cmp_ref[j] = z[:, _T_CMP + j * NSA_D:_T_CMP + (j + 1) * NSA_D]
    cqn = _rms(z[:, _T_CQ:_T_CKV], qg_ref[...]).astype(BF16)
    q2t = _dot_nt(wq2_ref[...], cqn)
    cqt, snt = cqt_ref[...], snt_ref[...]
    nq = MLA_HEADS * LANES
    scale = (MLA_NOPE + MLA_ROPE) ** -0.5 * LOG2E
    for hh in range(MLA_HEADS):
        qa = q2t[hh * LANES:(hh + 1) * LANES]
        qs = q2t[nq + hh * LANES:nq + (hh + 1) * LANES]
        qmt_ref[hh * LANES:(hh + 1) * LANES] = ((qa * cqt + qs * snt) * scale).astype(BF16)
    ckvn = _rms(z[:, _T_CKV:_T_KPE], kvg_ref[...]).astype(BF16)
    k2 = _dot(ckvn, wk2_ref[...])
    krot = z[:, _T_KPE:_T_KPE + LANES] * ck_ref[...] + z[:, _T_KPE + LANES:_T_END] * sn_ref[...]
    for hh in range(MLA_HEADS):
        km_ref[hh] = (k2[:, hh * LANES:(hh + 1) * LANES] + krot).astype(BF16)
    vt = _dot_nt(wv2_ref[...], ckvn)
    for hh in range(MLA_HEADS):
        vmt_ref[hh, :MLA_V] = vt[hh * MLA_V:(hh + 1) * MLA_V].astype(BF16)
        vmt_ref[hh, MLA_V:] = ones_rows


def _inproj(x1, mix_pre_g, weights, q_norm_g, kv_norm_g, tables, tm):
    b, s, d = x1.shape
    w_tok, w_feat, w_q2t, w_k2, w_v2t = weights
    ck, sn, cqt, snt = tables
    tok = lambda w: pl.BlockSpec((None, tm, w), lambda bi, i: (bi, i, 0))
    feat = lambda r: pl.BlockSpec((None, r, tm), lambda bi, i: (bi, 0, i))
    heads = lambda n, w: pl.BlockSpec((None, n, tm, w), lambda bi, i: (bi, 0, i, 0))
    slabs = lambda n, r: pl.BlockSpec((None, n, r, tm), lambda bi, i: (bi, 0, 0, i))
    tab = pl.BlockSpec((tm, LANES), lambda bi, i: (i, 0))
    tabt = pl.BlockSpec((LANES, tm), lambda bi, i: (0, i))
    sds = jax.ShapeDtypeStruct
    return pl.pallas_call(
        functools.partial(_inproj_kernel, tm=tm),
        grid=(b, s // tm),
        in_specs=[tok(d), _const_spec((1, d)), _const_spec(w_tok.shape), _const_spec(w_feat.shape),
                  _const_spec((1, MLA_Q_RANK)), _const_spec(w_q2t.shape),
                  _const_spec((1, MLA_KV_RANK)), _const_spec(w_k2.shape), _const_spec(w_v2t.shape),
                  tab, tab, tabt, tabt],
        out_specs=[feat(NSA_HEADS * NSA_D), heads(NSA_GROUPS, 2 * LANES), heads(NSA_GROUPS, LANES),
                   slabs(NSA_GROUPS, NSA_D + V_EXTRA), slabs(NSA_GROUPS, NSA_D + V_EXTRA),
                   heads(4, NSA_D), feat(3 * NSA_HEADS),
                   feat(MLA_HEADS * LANES), heads(MLA_HEADS, LANES), slabs(MLA_HEADS, MLA_V + V_EXTRA)],
        out_shape=[sds((b, NSA_HEADS * NSA_D, s), BF16), sds((b, NSA_GROUPS, s, 2 * LANES), BF16),
                   sds((b, NSA_GROUPS, s, LANES), BF16),
                   sds((b, NSA_GROUPS, NSA_D + V_EXTRA, s), BF16), sds((b, NSA_GROUPS, NSA_D + V_EXTRA, s), BF16),
                   sds((b, 4, s, NSA_D), F32), sds((b, 3 * NSA_HEADS, s), F32),
                   sds((b, MLA_HEADS * LANES, s), BF16), sds((b, MLA_HEADS, s, LANES), BF16),
                   sds((b, MLA_HEADS, MLA_V + V_EXTRA, s), BF16)],
        compiler_params=_params("parallel", "parallel"),
        name="inproj",
    )(x1, mix_pre_g.reshape(1, d), w_tok, w_feat, q_norm_g.reshape(1, -1), w_q2t,
      kv_norm_g.reshape(1, -1), w_k2, w_v2t, ck, sn, cqt, snt)


def _compress_kernel(c_ref, pos_ref, w1_ref, w2k_ref, w2vt_ref, kc_ref, vct_ref):
    nc = c_ref.shape[1]
    lane = lax.broadcasted_iota(jnp.int32, (nc, NSA_D), 1)
    blk = lax.broadcasted_iota(jnp.int32, (nc, NSA_D), 0)
    end_cols = jnp.where(lane == 0, lax.shift_right_logical(blk, END_SHIFT),
                         jnp.where(lane == 1, blk & ((1 << END_SHIFT) - 1), 0)).astype(F32)
    for j in range(4):
        kv, g = divmod(j, NSA_GROUPS)
        c = c_ref[j]
        a0 = _dot((c + pos_ref[kv, 0]).astype(BF16), w1_ref[kv, 0])
        a1 = _dot((c + pos_ref[kv, 1]).astype(BF16), w1_ref[kv, 1])
        pre = a0 + pltpu.roll(a1, nc - 1, 0)
        hid = jax.nn.gelu(pre).astype(BF16)
        if kv == 0:
            kc_ref[g] = jnp.concatenate([_dot(hid, w2k_ref[...]), end_cols], axis=1).astype(BF16)
        else:
            vct_ref[g * NSA_D:(g + 1) * NSA_D] = _dot_nt(w2vt_ref[...], hid).astype(BF16)


def _compress(cmp_in, pos_k, w1_k, w2_k, pos_v, w1_v, w2_v):
    b, _, s, dk = cmp_in.shape
    nc = s // CMP_STRIDE
    kdim = CMP_STRIDE * dk
    c4 = cmp_in.reshape(b, 4, nc, kdim)
    pos = jnp.stack([pos_k.reshape(2, 1, kdim), pos_v.reshape(2, 1, kdim)])
    w1 = jnp.stack([w1_k.reshape(2, kdim, CMP_HID), w1_v.reshape(2, kdim, CMP_HID)]).astype(BF16)
    return pl.pallas_call(
        _compress_kernel,
        grid=(b,),
        in_specs=[pl.BlockSpec((None, 4, nc, kdim), lambda bi: (bi, 0, 0, 0)),
                  _const_spec(pos.shape), _const_spec(w1.shape),
                  _const_spec((CMP_HID, dk)), _const_spec((dk, CMP_HID))],
        out_specs=[pl.BlockSpec((None, NSA_GROUPS, nc, LANES), lambda bi: (bi, 0, 0, 0)),
                   pl.BlockSpec((None, LANES, nc), lambda bi: (bi, 0, 0))],
        out_shape=[jax.ShapeDtypeStruct((b, NSA_GROUPS, nc, LANES), BF16),
                   jax.ShapeDtypeStruct((b, LANES, nc), BF16)],
        compiler_params=_params("parallel"),
        name="compress",
    )(c4, pos, w1, w2_k.astype(BF16), w2_v.T.astype(BF16))


def _nsa_cmp_kernel(qt_ref, kc_ref, vct_ref, ovt_ref, pool_ref, oct_ref, sbt_ref, act_ref, *, tq, tq_main):
    nc = kc_ref.shape[1]
    t = pl.program_id(1) * tq + lax.broadcasted_iota(jnp.int32, (1, tq), 1)
    cmp_end = lax.broadcasted_iota(jnp.int32, (nc, 1), 0) * CMP_STRIDE + (CMP_LEN - 1)
    valid = cmp_end <= t
    any_valid = t >= CMP_LEN - 1
    slopes = _alibi_slopes()
    row = lax.broadcasted_iota(jnp.int32, (NSA_D, tq), 0)
    blk = lax.broadcasted_iota(jnp.int32, (LANES, tq), 0)
    blkf = blk.astype(F32)
    cur = lax.shift_right_logical(t, SEL_SHIFT)
    forced = (blk == 0) | (blk == cur) | (blk == cur - 1)
    for g in range(NSA_GROUPS):
        vo = jnp.concatenate([vct_ref[g * NSA_D:(g + 1) * NSA_D], ovt_ref[...]], axis=0)
        imp = jnp.zeros((LANES, tq), F32)
        for hh in range(NSA_HPG):
            h = g * NSA_HPG + hh
            hi_w = slopes[h] * (CMP_STRIDE << END_SHIFT)
            tail = jnp.where(row == 0, hi_w, jnp.where(row == 1, slopes[h] * CMP_STRIDE, 0.0)).astype(BF16)
            q = jnp.concatenate([qt_ref[h * NSA_D:(h + 1) * NSA_D], tail], axis=0)
            s = jnp.where(valid, _dot(kc_ref[g], q), NEG)
            e = jnp.exp(s - jnp.max(s, axis=0, keepdims=True))
            norm = jnp.where(any_valid, 1.0 / jnp.maximum(jnp.sum(e, axis=0, keepdims=True), 1e-30), 0.0)
            r = _dot(vo, e.astype(BF16)) * norm
            oct_ref[h * NSA_D:(h + 1) * NSA_D] = r[:NSA_D]
            imp = imp + r[NSA_D:]
        work = jnp.where(blk <= cur, jnp.where(forced, FORCE_SCORE, imp), NEG)
        chosen = jnp.zeros((LANES, tq), jnp.bool_)
        for _ in range(SEL_TOPK):
            top = jnp.max(work, axis=0, keepdims=True)
            idx = jnp.min(jnp.where(work == top, blkf, float(LANES)), axis=0, keepdims=True)
            hit = blkf == idx
            chosen = chosen | hit
            work = jnp.where(hit, -jnp.inf, work)
        sbt_ref[g] = jnp.where(chosen, 0.0, NEG).astype(BF16)
        used = _dot(pool_ref[...], chosen.astype(BF16))
        for j in range(tq // tq_main):
            any_q = jnp.max(used[:, j * tq_main:(j + 1) * tq_main], axis=1, keepdims=True)
            act_ref[g, j] = jnp.broadcast_to(any_q, act_ref.shape[2:])


def _nsa_cmp(qt, kc, vct, overlap_t, pool, tq, tq_main):
    b, _, s = qt.shape
    nc = kc.shape[2]
    nkt = pool.shape[0]
    return pl.pallas_call(
        functools.partial(_nsa_cmp_kernel, tq=tq, tq_main=tq_main),
        grid=(b, s // tq),
        in_specs=[pl.BlockSpec((None, NSA_HEADS * NSA_D, tq), lambda bi, i: (bi, 0, i)),
                  pl.BlockSpec((None, NSA_GROUPS, nc, LANES), lambda bi, i: (bi, 0, 0, 0)),
                  pl.BlockSpec((None, LANES, nc), lambda bi, i: (bi, 0, 0)),
                  _const_spec(overlap_t.shape), _const_spec(pool.shape)],
        out_specs=[pl.BlockSpec((None, NSA_HEADS * NSA_D, tq), lambda bi, i: (bi, 0, i)),
                   pl.BlockSpec((None, NSA_GROUPS, LANES, tq), lambda bi, i: (bi, 0, 0, i)),
                   pl.BlockSpec((None, NSA_GROUPS, tq // tq_main, nkt, LANES), lambda bi, i: (bi, 0, i, 0, 0))],
        out_shape=[jax.ShapeDtypeStruct((b, NSA_HEADS * NSA_D, s), F32),
                   jax.ShapeDtypeStruct((b, NSA_GROUPS, LANES, s), BF16),
                   jax.ShapeDtypeStruct((b, NSA_GROUPS, s // tq_main, nkt, LANES), F32)],
        compiler_params=_params("parallel", "parallel"),
        name="nsa_cmp",
    )(qt, kc, vct, overlap_t, pool)


def _tile_pool_matrix(seq, tk_sel):
    per_tile = tk_sel // SEL_LEN
    nkt = -(-(seq // tk_sel) // 16) * 16
    pool = np.zeros((nkt, LANES), np.float32)
    for c in range(seq // SEL_LEN):
        pool[c // per_tile, c] = 1.0
    return jnp.asarray(pool, BF16)


def _overlap_matrix_t(seq):
    n_c = (seq - CMP_LEN) // CMP_STRIDE + 1
    n_sel = seq // SEL_LEN
    c0 = np.arange(n_c) * CMP_STRIDE
    s0 = np.arange(n_sel) * SEL_LEN
    ov = np.clip(np.minimum(c0[:, None] + CMP_LEN, s0[None, :] + SEL_LEN)
                 - np.maximum(c0[:, None], s0[None, :]), 0, None) / CMP_LEN
    full = np.zeros((LANES, seq // CMP_STRIDE), np.float32)
    full[:n_sel, :n_c] = ov.T
    return jnp.asarray(full, BF16)


def _flash_pipelined(n_tiles, first_scores, scores, values, dv, s_buf, p_buf, exp=jnp.exp, n_static=0):
    chains = range(len(first_scores))
    nq = first_scores[0].shape[1]
    state = tuple((jnp.full((1, nq), NEG, F32), jnp.zeros((dv + V_EXTRA, nq), F32)) for _ in chains)
    for c in chains:
        s_buf[0, c] = first_scores[c]
        p_buf[1, c] = jnp.zeros(p_buf.shape[2:], BF16)
    colmax = tuple(jnp.max(first_scores[c], axis=0, keepdims=True) for c in chains)
    ones = tuple(jnp.ones((1, nq), F32) for _ in chains)

    static = isinstance(n_tiles, int)

    def step(i, carry, cur):
        st, alphas, cmax = carry
        nxt = 1 - cur
        vts = values(max(i - 1, 0) if isinstance(i, int) else jnp.maximum(i - 1, 0))
        s_next = None if static and i + 1 == n_tiles else scores(i + 1)
        new_st, new_alpha, new_cmax = [], [], []
        for c in chains:
            m, acc = st[c]
            acc = alphas[c] * acc + _dot(vts[c], p_buf[nxt, c])
            m_new = jnp.maximum(m, cmax[c])
            p_buf[cur, c] = exp(s_buf[cur, c] - m_new).astype(BF16)
            new_st.append((m_new, acc))
            new_alpha.append(exp(m - m_new))
            if s_next is not None:
                s_buf[nxt, c] = s_next[c]
                new_cmax.append(jnp.max(s_next[c], axis=0, keepdims=True))
        return tuple(new_st), tuple(new_alpha), tuple(new_cmax)

    def steps(first, count, carry):
        for u in range(count):
            carry = step(first + u, carry, u % 2)
        return carry

    carry = (state, ones, colmax)
    if static:
        st, alphas, _ = steps(0, n_tiles, carry)
        vts = values(n_tiles - 1)
    else:
        carry = steps(0, n_static, carry)
        n_pairs = n_tiles // 2
        carry = lax.fori_loop(n_static // 2, n_pairs, lambda j, c: steps(2 * j, 2, c), carry)
        st, alphas, _ = lax.fori_loop(2 * n_pairs, n_tiles, lambda i, c: steps(i, 1, c), carry)
        vts = values(jnp.maximum(n_tiles - 1, 0))
    last = (n_tiles + 1) & 1
    outs = []
    for c, (_, acc) in enumerate(st):
        acc = alphas[c] * acc + _dot(vts[c], p_buf[last, c])
        outs.append(acc[:dv] * (1.0 / acc[dv:dv + 1]))
    return outs


def _nsa_main_kernel(act_ref, qt_ref, sbt_ref, oct_ref, gt_ref, ksa_ref, kwa_ref, vst_ref, vwt_ref, y_ref,
                     s_buf, p_buf, tiles_ref, *, tq, tk_sel, act_tiles):
    t0 = pl.program_id(1) * tq
    nq = NSA_HPG * tq
    slopes = _alibi_slopes()
    t_lane = jnp.concatenate([t0 + lax.broadcasted_iota(jnp.int32, (1, tq), 1)] * NSA_HPG, axis=1)
    row = lax.broadcasted_iota(jnp.int32, (NSA_D, nq), 0)
    blk_rel = (lax.broadcasted_iota(jnp.int32, (LANES, tq), 0)
               - lax.shift_right_logical(t0, SEL_SHIFT)).astype(F32)

    q_sel, q_win = [], []
    for g in range(NSA_GROUPS):
        heads = range(g * NSA_HPG, (g + 1) * NSA_HPG)
        sl = jnp.concatenate([jnp.full((1, tq), slopes[h], F32) for h in heads], axis=1)
        qg = jnp.concatenate(
            [qt_ref[h * NSA_D:(h + 1) * NSA_D] for h in heads], axis=1)
        tail = jnp.where(row == 0, sl, 0.0).astype(BF16)
        tail_win = jnp.where(row == 0, sl, jnp.where(row == 1, sl * LANES, 0.0)).astype(BF16)
        sb = sbt_ref[g].astype(F32)
        bias = jnp.concatenate([sb + (slopes[h] * SEL_LEN) * blk_rel for h in heads], axis=1)
        q_sel.append(jnp.concatenate([bias.astype(BF16), qg, tail], axis=0))
        q_win.append(jnp.concatenate([qg, tail_win], axis=0))

    groups = range(NSA_GROUPS)

    def sel_scores(kt, g):
        k0 = pl.multiple_of(kt * tk_sel, tk_sel)
        return _dot(ksa_ref[g, pl.ds(k0, tk_sel), :], q_sel[g])

    def sel_values(kt, g):
        k0 = pl.multiple_of(kt * tk_sel, tk_sel)
        return vst_ref[g, :, pl.ds(k0, tk_sel)]

    n_full = (t0 + 1) // tk_sel
    n_kt = tiles_ref.shape[1]
    counts, spare = [], []
    for g in groups:
        base = ((pl.program_id(0) * NSA_GROUPS + g) * pl.num_programs(1) + pl.program_id(1)) * act_tiles
        for k in range(n_kt):
            tiles_ref[g, k] = 0

        def scan(kt, carry, g=g, base=base):
            cnt, unused = carry
            used = act_ref[base + kt]
            tiles_ref[g, cnt] = kt
            return cnt + used, jnp.where(used == 0, kt, unused)

        cnt, unused = lax.fori_loop(0, n_full, scan, (jnp.int32(0), jnp.int32(0)))
        counts.append(cnt)
        spare.append(unused)

    def tile_of(k, g):
        return jnp.where(k < counts[g], tiles_ref[g, jnp.minimum(k, n_kt - 1)], spare[g])

    n_diag = tq // tk_sel
    first = t0 // tk_sel

    def at(k, g):
        if isinstance(k, int) and k < n_diag:
            return first + k
        return jnp.where(k < n_diag, first + k, tile_of(jnp.maximum(k - n_diag, 0), g))

    def sel_at(k):
        if isinstance(k, int) and k < n_diag:
            causal = (first + k) * tk_sel + lax.broadcasted_iota(jnp.int32, (tk_sel, 1), 0) <= t_lane
            return [jnp.where(causal, sel_scores(first + k, g), NEG) for g in groups]
        return [sel_scores(tile_of(k - n_diag, g), g) for g in groups]

    n_pos = jnp.maximum(counts[0], counts[1]) + n_diag
    can_pad = (counts[0] < n_full) & (counts[1] < n_full)
    n_pos = n_pos + jnp.where(can_pad, n_pos & 1, 0)
    sel = _flash_pipelined(n_pos, sel_at(0), sel_at,
                           lambda k: [sel_values(at(k, g), g) for g in groups], NSA_D, s_buf, p_buf,
                           n_static=n_diag - n_diag % 2)

    assert WINDOW % tk_sel == 0 and tq % tk_sel == 0
    start = pl.multiple_of(jnp.maximum(t0 - WINDOW, 0), tk_sel)
    n_wt = (WINDOW + tq) // tk_sel

    def win_scores(k, g):
        w = n_wt - 1 - k
        k0 = pl.multiple_of(start + w * tk_sel, tk_sel)
        key = k0 + lax.broadcasted_iota(jnp.int32, (tk_sel, 1), 0)
        ok = key <= t_lane
        if w < n_diag:
            ok = ok & (key > t_lane - WINDOW)
        return jnp.where(ok, _dot(kwa_ref[g, pl.ds(k0, tk_sel), :], q_win[g]), NEG)

    def win_values(k, g):
        k0 = pl.multiple_of(start + (n_wt - 1 - k) * tk_sel, tk_sel)
        return vwt_ref[g, :, pl.ds(k0, tk_sel)]

    win = _flash_pipelined(n_wt, [win_scores(0, g) for g in groups],
                           lambda k: [win_scores(k, g) for g in groups],
                           lambda k: [win_values(k, g) for g in groups], NSA_D, s_buf, p_buf)

    gates = gt_ref[...]
    outs = []
    for h in range(NSA_HEADS):
        g, hh = divmod(h, NSA_HPG)
        cols = slice(hh * tq, (hh + 1) * tq)
        outs.append(gates[3 * h:3 * h + 1] * oct_ref[h * NSA_D:(h + 1) * NSA_D]
                    + gates[3 * h + 1:3 * h + 2] * sel[g][:, cols] + gates[3 * h + 2:3 * h + 3] * win[g][:, cols])
    y_ref[...] = jnp.concatenate(outs, axis=0).T.astype(BF16)


def _nsa_main(act, qt, sbt, oct, gt, ksa, kwa, vst, vwt, tq, tk_sel):
    b, _, s = qt.shape
    act_tiles = act.shape[0] // (b * NSA_GROUPS * (s // tq))
    feat = lambda r: pl.BlockSpec((None, r, tq), lambda bi, i, _: (bi, 0, i))
    whole = lambda *dims: pl.BlockSpec((None, NSA_GROUPS) + dims, lambda bi, i, _: (bi, 0, 0, 0))
    grid_spec = pltpu.PrefetchScalarGridSpec(
        num_scalar_prefetch=1,
        grid=(b, s // tq),
        in_specs=[feat(NSA_HEADS * NSA_D),
                  pl.BlockSpec((None, NSA_GROUPS, LANES, tq), lambda bi, i, _: (bi, 0, 0, i)),
                  feat(NSA_HEADS * NSA_D), feat(3 * NSA_HEADS),
                  whole(s, 2 * LANES), whole(s, LANES),
                  whole(NSA_D + V_EXTRA, s), whole(NSA_D + V_EXTRA, s)],
        out_specs=pl.BlockSpec((None, tq, NSA_HEADS * NSA_D), lambda bi, i, _: (bi, i, 0)),
        scratch_shapes=[pltpu.VMEM((2, NSA_GROUPS, tk_sel, NSA_HPG * tq), F32),
                        pltpu.VMEM((2, NSA_GROUPS, tk_sel, NSA_HPG * tq), BF16),
                        pltpu.SMEM((NSA_GROUPS, act_tiles), jnp.int32)])
    return pl.pallas_call(
        functools.partial(_nsa_main_kernel, tq=tq, tk_sel=tk_sel, act_tiles=act_tiles),
        grid_spec=grid_spec,
        out_shape=jax.ShapeDtypeStruct((b, s, NSA_HEADS * NSA_D), BF16),
        compiler_params=_params("parallel", "arbitrary"),
        name="nsa_main",
    )(act, qt, sbt, oct, gt, ksa, kwa, vst, vwt)


def _mla_kernel(qt_ref, k_ref, vt_ref, y_ref, s_buf, p_buf, *, tq, tk):
    t0 = pl.program_id(2) * tq
    t_lane = t0 + lax.broadcasted_iota(jnp.int32, (1, tq), 1)
    heads = range(2)
    qts = [qt_ref[hh * LANES:(hh + 1) * LANES] for hh in heads]

    def scores(kt):
        k0 = pl.multiple_of(kt * tk, tk)
        return [_dot(k_ref[hh, pl.ds(k0, tk), :], qts[hh]) for hh in heads]

    def values(kt):
        k0 = pl.multiple_of(kt * tk, tk)
        return [vt_ref[hh, :, pl.ds(k0, tk)] for hh in heads]

    assert tq == 2 * tk
    first = 2 * pl.program_id(2)

    def at(k):
        return first + k if isinstance(k, int) and k < 2 else jnp.where(k < 2, first + k, k - 2)

    def masked_scores(k):
        if isinstance(k, int) and k < 2:
            causal = (first + k) * tk + lax.broadcasted_iota(jnp.int32, (tk, 1), 0) <= t_lane
            return [jnp.where(causal, s, NEG) for s in scores(first + k)]
        return scores(k - 2)

    st = _flash_pipelined(first + 2, masked_scores(0), masked_scores, lambda k: values(at(k)),
                          MLA_V, s_buf, p_buf, exp=jnp.exp2, n_static=2)
    o = jnp.concatenate(st, axis=0)
    y_ref[...] = o.T.astype(BF16)


def _mla(qmt, km, vmt, tq, tk):
    b, h, s, _ = km.shape
    k5 = km.reshape(b, h // 2, 2, s, LANES)
    vmt = vmt.reshape(b, h // 2, 2, MLA_V + V_EXTRA, s)
    return pl.pallas_call(
        functools.partial(_mla_kernel, tq=tq, tk=tk),
        grid=(b, h // 2, s // tq),
        in_specs=[pl.BlockSpec((None, 2 * LANES, tq), lambda bi, hp, i: (bi, hp, i)),
                  pl.BlockSpec((None, None, 2, s, LANES), lambda bi, hp, i: (bi, hp, 0, 0, 0)),
                  pl.BlockSpec((None, None, 2, MLA_V + V_EXTRA, s), lambda bi, hp, i: (bi, hp, 0, 0, 0))],
        out_specs=pl.BlockSpec((None, tq, 2 * MLA_V), lambda bi, hp, i: (bi, i, hp)),
        out_shape=jax.ShapeDtypeStruct((b, s, h * MLA_V), BF16),
        scratch_shapes=[pltpu.VMEM((2, 2, tk, tq), F32), pltpu.VMEM((2, 2, tk, tq), BF16)],
        compiler_params=_params("parallel", "parallel", "arbitrary"),
        name="mla",
    )(qmt, k5, vmt)


def _merge_kernel(x_ref, yn_ref, ym_ref, pre_ref, post_ref, wgm_ref, wpn_ref, wpm_ref, wo_ref, o_ref):
    x = x_ref[...]
    d = x.shape[1]
    h = _rms(x, pre_ref[...]).astype(BF16)
    gm = jax.nn.sigmoid(_dot(h, wgm_ref[...]))
    merged = gm[:, :d] * _dot(yn_ref[...], wpn_ref[...]) + gm[:, d:] * _dot(ym_ref[...], wpm_ref[...])
    y = _dot(merged.astype(BF16), wo_ref[...])
    o_ref[...] = x + _rms(y, post_ref[...])


def _merge(x1, y_nsa, y_mla, pre_g, post_g, w_gm, w_pn, w_pm, w_out, tm):
    n, d = x1.shape
    tok = lambda w: pl.BlockSpec((tm, w), lambda i: (i, 0))
    return pl.pallas_call(
        _merge_kernel,
        grid=(n // tm,),
        in_specs=[tok(d), tok(y_nsa.shape[1]), tok(y_mla.shape[1]), _const_spec((1, d)), _const_spec((1, d)),
                  _const_spec(w_gm.shape), _const_spec(w_pn.shape), _const_spec(w_pm.shape),
                  _const_spec(w_out.shape)],
        out_specs=tok(d),
        out_shape=jax.ShapeDtypeStruct((n, d), F32),
        compiler_params=_params("parallel"),
        name="merge",
    )(x1, y_nsa, y_mla, pre_g.reshape(1, d), post_g.reshape(1, d),
      w_gm.astype(BF16), w_pn.astype(BF16), w_pm.astype(BF16), w_out.astype(BF16))


def kernel(x, ff1_pre_g, ff1_post_g, ff1_w_gate, ff1_w_up, ff1_w_down, mix_pre_g, mix_post_g, w_in, cmp_pos_k, cmp_w1_k, cmp_w2_k, cmp_pos_v, cmp_w1_v, cmp_w2_v, mla_q_norm_g, mla_w_uq, mla_kv_norm_g, mla_w_ukv, w_proj_nsa, w_proj_mla, w_out, ff2_pre_g, ff2_post_g, ff2_w_gate, ff2_w_up, ff2_w_down):
    b, s, d = x.shape
    assert s % (SEL_LEN * SEL_TOPK) == 0 and s // SEL_LEN <= LANES
    tl = _tiles(s)
    n = b * s

    x1 = _ffn(x.reshape(n, d), ff1_pre_g, ff1_post_g, ff1_w_gate, ff1_w_up, ff1_w_down, tl["tm"])

    (qt, ksa, kwa, vst, vwt, cmp_in, gt, qmt, km, vmt) = _inproj(
        x1.reshape(b, s, d), mix_pre_g, _inproj_weights(w_in, mla_w_uq, mla_w_ukv),
        mla_q_norm_g, mla_kv_norm_g, _rope_tables(s), tl["tm"])

    kc, vct = _compress(cmp_in, cmp_pos_k, cmp_w1_k, cmp_w2_k, cmp_pos_v, cmp_w1_v, cmp_w2_v)
    oct, sbt, act = _nsa_cmp(qt, kc, vct, _overlap_matrix_t(s), _tile_pool_matrix(s, tl["tk_sel"]),
                             tl["tq_cmp"], tl["tq_nsa"])
    act = (act[..., 0] > 0).astype(jnp.int32).reshape(-1)
    y_nsa = _nsa_main(act, qt, sbt, oct, gt, ksa, kwa, vst, vwt, tl["tq_nsa"], tl["tk_sel"])
    y_mla = _mla(qmt, km, vmt, tl["tq_mla"], tl["tq_mla"] // 2)

    o_gm = sum((NSA_HEADS * NSA_D, 6 * NSA_GROUPS * NSA_D, 3 * NSA_HEADS, MLA_Q_RANK, MLA_KV_RANK, MLA_ROPE))
    x2 = _merge(x1, y_nsa.reshape(n, -1), y_mla.reshape(n, -1), mix_pre_g, mix_post_g,
                w_in[:, o_gm:], w_proj_nsa, w_proj_mla, w_out, tl["tm"])
    x3 = _ffn(x2, ff2_pre_g, ff2_post_g, ff2_w_gate, ff2_w_up, ff2_w_down, tl["tm"])
    return x3.reshape(b, s, d)
```

```python
import functools

import numpy as np
import jax
import jax.numpy as jnp
from jax import lax
from jax.experimental import pallas as pl
from jax.experimental.pallas import tpu as pltpu

F32 = jnp.float32
BF16 = jnp.bfloat16

EPS = 1e-6
NEG = -1e30
FORCE_SCORE = 1e4
NSA_HEADS = 8
NSA_GROUPS = 2
NSA_HPG = NSA_HEADS // NSA_GROUPS
NSA_D = 64
CMP_LEN = 32
CMP_STRIDE = 16
CMP_HID = 256
END_SHIFT = 4
SEL_LEN = 64
SEL_SHIFT = 6
SEL_TOPK = 16
WINDOW = 512
MLA_HEADS = 8
MLA_NOPE = 64
MLA_ROPE = 32
MLA_V = 64
MLA_Q_RANK = 256
MLA_KV_RANK = 128
ROPE_THETA = 10000.0
LANES = 128
POS_SHIFT = 7
LOG2E = 1.4426950408889634
VMEM_LIMIT = 56 * 1024 * 1024
FFN_CHUNK = 256
V_EXTRA = 16


def _tiles(seq):
    return dict(
        tm=min(1024, seq),
        tq_cmp=min(1024, seq),
        tq_nsa=min(256, seq),
        tk_sel=min(256, seq),
        tq_mla=min(1024, seq),
    )


def _params(*sem):
    return pltpu.CompilerParams(dimension_semantics=sem, vmem_limit_bytes=VMEM_LIMIT)


def _const_spec(shape):
    nd = len(shape)
    return pl.BlockSpec(shape, lambda *_: (0,) * nd, pipeline_mode=pl.Buffered(1))


def _rms(x, g):
    return x * lax.rsqrt(jnp.mean(x * x, axis=-1, keepdims=True) + EPS) * g


def _dot(a, b):
    return jnp.dot(a, b, preferred_element_type=F32)


def _dot_nt(a, b):
    return lax.dot_general(a, b, (((1,), (1,)), ((), ())), preferred_element_type=F32)


def _alibi_slopes():
    return [float(2.0 ** (-8.0 * (i + 1) / NSA_HEADS)) for i in range(NSA_HEADS)]


def _ffn_kernel(x_ref, pre_ref, post_ref, wg_ref, wu_ref, wd_ref, o_ref, *, fc):
    x = x_ref[...]
    h = _rms(x, pre_ref[...]).astype(BF16)
    acc = jnp.zeros(x.shape, F32)
    for c in range(wg_ref.shape[1] // fc):
        gate = _dot(h, wg_ref[:, c * fc:(c + 1) * fc])
        up = _dot(h, wu_ref[:, c * fc:(c + 1) * fc])
        act = (gate * jax.nn.sigmoid(gate) * up).astype(BF16)
        acc = acc + _dot(act, wd_ref[c * fc:(c + 1) * fc, :])
    o_ref[...] = x + 0.5 * _rms(acc, post_ref[...])


def _ffn(x2d, pre_g, post_g, w_gate, w_up, w_down, tm):
    n, d = x2d.shape
    f = w_gate.shape[1]
    fc = FFN_CHUNK if f % FFN_CHUNK == 0 else f
    return pl.pallas_call(
        functools.partial(_ffn_kernel, fc=fc),
        grid=(n // tm,),
        in_specs=[pl.BlockSpec((tm, d), lambda i: (i, 0)),
                  _const_spec((1, d)), _const_spec((1, d)),
                  _const_spec((d, f)), _const_spec((d, f)), _const_spec((f, d))],
        out_specs=pl.BlockSpec((tm, d), lambda i: (i, 0)),
        out_shape=jax.ShapeDtypeStruct((n, d), F32),
        compiler_params=_params("parallel"),
        name="ffn",
    )(x2d, pre_g.reshape(1, d), post_g.reshape(1, d),
      w_gate.astype(BF16), w_up.astype(BF16), w_down.astype(BF16))


_T_KSEL = 0
_T_KWIN = _T_KSEL + 2 * LANES
_T_CMP = _T_KWIN + 2 * LANES
_T_CQ = _T_CMP + 2 * LANES
_T_CKV = _T_CQ + MLA_Q_RANK
_T_KPE = _T_CKV + MLA_KV_RANK
_T_END = _T_KPE + 2 * LANES
_F_QN = 0
_F_VSEL = _F_QN + NSA_HEADS * NSA_D
_F_VWIN = _F_VSEL + LANES
_F_GATE = _F_VWIN + LANES
_F_END = _F_GATE + 32


def _inproj_weights(w_in, w_uq, w_ukv):
    d = w_in.shape[0]
    o_q = 0
    o_kv = o_q + NSA_HEADS * NSA_D
    o_g = o_kv + 6 * NSA_GROUPS * NSA_D
    o_cq = o_g + 3 * NSA_HEADS
    o_ckv = o_cq + MLA_Q_RANK
    o_kpe = o_ckv + MLA_KV_RANK
    zeros = lambda n: jnp.zeros((d, n), w_in.dtype)
    kv = lambda j, g: w_in[:, o_kv + (j * NSA_GROUPS + g) * NSA_D:o_kv + (j * NSA_GROUPS + g + 1) * NSA_D]
    half = MLA_ROPE // 2
    kp1 = w_in[:, o_kpe:o_kpe + half]
    kp2 = w_in[:, o_kpe + half:o_kpe + MLA_ROPE]
    tail = LANES - MLA_NOPE - MLA_ROPE
    cols = [kv(2, 0), zeros(NSA_D), kv(2, 1), zeros(NSA_D),
            kv(4, 0), zeros(NSA_D), kv(4, 1), zeros(NSA_D),
            kv(0, 0), kv(0, 1), kv(1, 0), kv(1, 1),
            w_in[:, o_cq:o_cq + MLA_Q_RANK], w_in[:, o_ckv:o_ckv + MLA_KV_RANK],
            zeros(MLA_NOPE), kp1, kp2, zeros(tail), zeros(MLA_NOPE), kp2, kp1, zeros(tail)]
    w_tok = jnp.concatenate(cols, axis=1).astype(BF16)
    assert w_tok.shape[1] == _T_END
    rows = [w_in[:, o_q:o_q + NSA_HEADS * NSA_D], kv(3, 0), kv(3, 1), kv(5, 0), kv(5, 1)]
    rows += [w_in[:, o_g:o_g + 3 * NSA_HEADS], zeros(_F_END - _F_GATE - 3 * NSA_HEADS)]
    w_feat = jnp.concatenate(rows, axis=1).T.astype(BF16)
    assert w_feat.shape[0] == _F_END

    dq = MLA_NOPE + MLA_ROPE
    zq = lambda n: jnp.zeros((w_uq.shape[0], n), w_uq.dtype)
    qa, qs = [], []
    for h in range(MLA_HEADS):
        nope = w_uq[:, h * dq:h * dq + MLA_NOPE]
        r1 = w_uq[:, h * dq + MLA_NOPE:h * dq + MLA_NOPE + half]
        r2 = w_uq[:, h * dq + MLA_NOPE + half:(h + 1) * dq]
        qa += [nope, r1, r2, zq(tail)]
        qs += [zq(MLA_NOPE), r2, r1, zq(tail)]
    w_q2t = jnp.concatenate(qa + qs, axis=1).T.astype(BF16)

    dkv = MLA_NOPE + MLA_V
    zk = lambda n: jnp.zeros((w_ukv.shape[0], n), w_ukv.dtype)
    ka, va = [], []
    for h in range(MLA_HEADS):
        ka += [w_ukv[:, h * dkv:h * dkv + MLA_NOPE], zk(LANES - MLA_NOPE)]
        va += [w_ukv[:, h * dkv + MLA_NOPE:(h + 1) * dkv]]
    w_k2 = jnp.concatenate(ka, axis=1).astype(BF16)
    w_v2t = jnp.concatenate(va, axis=1).T.astype(BF16)
    return w_tok, w_feat, w_q2t, w_k2, w_v2t


def _rope_tables(seq):
    half = MLA_ROPE // 2
    freqs = jnp.asarray(ROPE_THETA ** (-np.arange(half, dtype=np.float32) / half), F32)
    ang = jnp.arange(seq).astype(F32)[:, None] * freqs[None, :]
    cos, sin = jnp.cos(ang), jnp.sin(ang)
    pad = jnp.zeros((seq, LANES - MLA_NOPE - MLA_ROPE), F32)
    cq = jnp.concatenate([jnp.ones((seq, MLA_NOPE), F32), cos, cos, pad], axis=1)
    ck = jnp.concatenate([jnp.zeros((seq, MLA_NOPE), F32), cos, cos, pad], axis=1)
    sn = jnp.concatenate([jnp.zeros((seq, MLA_NOPE), F32), -sin, sin, pad], axis=1)
    return ck, sn, cq.T, sn.T


def _inproj_kernel(x_ref, g_ref, wt_ref, wf_ref, qg_ref, wq2_ref, kvg_ref, wk2_ref, wv2_ref,
                   ck_ref, sn_ref, cqt_ref, snt_ref,
                   qt_ref, ksa_ref, kwa_ref, vst_ref, vwt_ref, cmp_ref, gt_ref, qmt_ref, km_ref, vmt_ref,
                   *, tm):
    h = _rms(x_ref[...], g_ref[...]).astype(BF16)
    z = _dot(h, wt_ref[...])
    zt = _dot_nt(wf_ref[...], h)
    qt_ref[...] = (zt[_F_QN:_F_VSEL] * NSA_D ** -0.5).astype(BF16)
    ones_rows = (lax.broadcasted_iota(jnp.int32, (V_EXTRA, tm), 0) == 0).astype(BF16)
    for g in range(NSA_GROUPS):
        for ref, base in ((vst_ref, _F_VSEL), (vwt_ref, _F_VWIN)):
            ref[g, :NSA_D] = zt[base + g * NSA_D:base + (g + 1) * NSA_D].astype(BF16)
            ref[g, NSA_D:] = ones_rows
    gt_ref[...] = jax.nn.sigmoid(zt[_F_GATE:_F_GATE + 3 * NSA_HEADS])
    pos = pl.program_id(1) * tm + lax.broadcasted_iota(jnp.int32, (tm, LANES), 0)
    lane = lax.broadcasted_iota(jnp.int32, (tm, LANES), 1)
    onehot = (lane == lax.shift_right_logical(pos, SEL_SHIFT)).astype(BF16)
    in_blk = (pos & (SEL_LEN - 1)).astype(F32)
    pos_lo_hi = jnp.where(lane == NSA_D, pos & (LANES - 1), lax.shift_right_logical(pos, POS_SHIFT)).astype(F32)
    for g in range(NSA_GROUPS):
        ksa_ref[g, :, :LANES] = onehot
        ksa_ref[g, :, LANES:] = jnp.where(
            lane == NSA_D, in_blk, z[:, _T_KSEL + g * LANES:_T_KSEL + (g + 1) * LANES]).astype(BF16)
        kwa_ref[g] = jnp.where(
            (lane == NSA_D) | (lane == NSA_D + 1), pos_lo_hi,
            z[:, _T_KWIN + g * LANES:_T_KWIN + (g + 1) * LANES]).astype(BF16)
    for j in range(4):
        cmp_ref[j] = z[:, _T_CMP + j * NSA_D:_T_CMP + (j + 1) * NSA_D]
    cqn = _rms(z[:, _T_CQ:_T_CKV], qg_ref[...]).astype(BF16)
    q2t = _dot_nt(wq2_ref[...], cqn)
    cqt, snt = cqt_ref[...], snt_ref[...]
    nq = MLA_HEADS * LANES
    scale = (MLA_NOPE + MLA_ROPE) ** -0.5 * LOG2E
    for hh in range(MLA_HEADS):
        qa = q2t[hh * LANES:(hh + 1) * LANES]
        qs = q2t[nq + hh * LANES:nq + (hh + 1) * LANES]
        qmt_ref[hh * LANES:(hh + 1) * LANES] = ((qa * cqt + qs * snt) * scale).astype(BF16)
    ckvn = _rms(z[:, _T_CKV:_T_KPE], kvg_ref[...]).astype(BF16)
    k2 = _dot(ckvn, wk2_ref[...])
    krot = z[:, _T_KPE:_T_KPE + LANES] * ck_ref[...] + z[:, _T_KPE + LANES:_T_END] * sn_ref[...]
    for hh in range(MLA_HEADS):
        km_ref[hh] = (k2[:, hh * LANES:(hh + 1) * LANES] + krot).astype(BF16)
    vt = _dot_nt(wv2_ref[...], ckvn)
    for hh in range(MLA_HEADS):
        vmt_ref[hh, :MLA_V] = vt[hh * MLA_V:(hh + 1) * MLA_V].astype(BF16)
        vmt_ref[hh, MLA_V:] = ones_rows


def _inproj(x1, mix_pre_g, weights, q_norm_g, kv_norm_g, tables, tm):
    b, s, d = x1.shape
    w_tok, w_feat, w_q2t, w_k2, w_v2t = weights
    ck, sn, cqt, snt = tables
    tok = lambda w: pl.BlockSpec((None, tm, w), lambda bi, i: (bi, i, 0))
    feat = lambda r: pl.BlockSpec((None, r, tm), lambda bi, i: (bi, 0, i))
    heads = lambda n, w: pl.BlockSpec((None, n, tm, w), lambda bi, i: (bi, 0, i, 0))
    slabs = lambda n, r: pl.BlockSpec((None, n, r, tm), lambda bi, i: (bi, 0, 0, i))
    tab = pl.BlockSpec((tm, LANES), lambda bi, i: (i, 0))
    tabt = pl.BlockSpec((LANES, tm), lambda bi, i: (0, i))
    sds = jax.ShapeDtypeStruct
    return pl.pallas_call(
        functools.partial(_inproj_kernel, tm=tm),
        grid=(b, s // tm),
        in_specs=[tok(d), _const_spec((1, d)), _const_spec(w_tok.shape), _const_spec(w_feat.shape),
                  _const_spec((1, MLA_Q_RANK)), _const_spec(w_q2t.shape),
                  _const_spec((1, MLA_KV_RANK)), _const_spec(w_k2.shape), _const_spec(w_v2t.shape),
                  tab, tab, tabt, tabt],
        out_specs=[feat(NSA_HEADS * NSA_D), heads(NSA_GROUPS, 2 * LANES), heads(NSA_GROUPS, LANES),
                   slabs(NSA_GROUPS, NSA_D + V_EXTRA), slabs(NSA_GROUPS, NSA_D + V_EXTRA),
                   heads(4, NSA_D), feat(3 * NSA_HEADS),
                   feat(MLA_HEADS * LANES), heads(MLA_HEADS, LANES), slabs(MLA_HEADS, MLA_V + V_EXTRA)],
        out_shape=[sds((b, NSA_HEADS * NSA_D, s), BF16), sds((b, NSA_GROUPS, s, 2 * LANES), BF16),
                   sds((b, NSA_GROUPS, s, LANES), BF16),
                   sds((b, NSA_GROUPS, NSA_D + V_EXTRA, s), BF16), sds((b, NSA_GROUPS, NSA_D + V_EXTRA, s), BF16),
                   sds((b, 4, s, NSA_D), F32), sds((b, 3 * NSA_HEADS, s), F32),
                   sds((b, MLA_HEADS * LANES, s), BF16), sds((b, MLA_HEADS, s, LANES), BF16),
                   sds((b, MLA_HEADS, MLA_V + V_EXTRA, s), BF16)],
        compiler_params=_params("parallel", "parallel"),
        name="inproj",
    )(x1, mix_pre_g.reshape(1, d), w_tok, w_feat, q_norm_g.reshape(1, -1), w_q2t,
      kv_norm_g.reshape(1, -1), w_k2, w_v2t, ck, sn, cqt, snt)


def _compress_kernel(c_ref, pos_ref, w1_ref, w2k_ref, w2vt_ref, kc_ref, vct_ref):
    nc = c_ref.shape[1]
    lane = lax.broadcasted_iota(jnp.int32, (nc, NSA_D), 1)
    blk = lax.broadcasted_iota(jnp.int32, (nc, NSA_D), 0)
    end_cols = jnp.where(lane == 0, lax.shift_right_logical(blk, END_SHIFT),
                         jnp.where(lane == 1, blk & ((1 << END_SHIFT) - 1), 0)).astype(F32)
    for j in range(4):
        kv, g = divmod(j, NSA_GROUPS)
        c = c_ref[j]
        a0 = _dot((c + pos_ref[kv, 0]).astype(BF16), w1_ref[kv, 0])
        a1 = _dot((c + pos_ref[kv, 1]).astype(BF16), w1_ref[kv, 1])
        pre = a0 + pltpu.roll(a1, nc - 1, 0)
        hid = jax.nn.gelu(pre).astype(BF16)
        if kv == 0:
            kc_ref[g] = jnp.concatenate([_dot(hid, w2k_ref[...]), end_cols], axis=1).astype(BF16)
        else:
            vct_ref[g * NSA_D:(g + 1) * NSA_D] = _dot_nt(w2vt_ref[...], hid).astype(BF16)


def _compress(cmp_in, pos_k, w1_k, w2_k, pos_v, w1_v, w2_v):
    b, _, s, dk = cmp_in.shape
    nc = s // CMP_STRIDE
    kdim = CMP_STRIDE * dk
    c4 = cmp_in.reshape(b, 4, nc, kdim)
    pos = jnp.stack([pos_k.reshape(2, 1, kdim), pos_v.reshape(2, 1, kdim)])
    w1 = jnp.stack([w1_k.reshape(2, kdim, CMP_HID), w1_v.reshape(2, kdim, CMP_HID)]).astype(BF16)
    return pl.pallas_call(
        _compress_kernel,
        grid=(b,),
        in_specs=[pl.BlockSpec((None, 4, nc, kdim), lambda bi: (bi, 0, 0, 0)),
                  _const_spec(pos.shape), _const_spec(w1.shape),
                  _const_spec((CMP_HID, dk)), _const_spec((dk, CMP_HID))],
        out_specs=[pl.BlockSpec((None, NSA_GROUPS, nc, LANES), lambda bi: (bi, 0, 0, 0)),
                   pl.BlockSpec((None, LANES, nc), lambda bi: (bi, 0, 0))],
        out_shape=[jax.ShapeDtypeStruct((b, NSA_GROUPS, nc, LANES), BF16),
                   jax.ShapeDtypeStruct((b, LANES, nc), BF16)],
        compiler_params=_params("parallel"),
        name="compress",
    )(c4, pos, w1, w2_k.astype(BF16), w2_v.T.astype(BF16))


def _nsa_cmp_kernel(qt_ref, kc_ref, vct_ref, ovt_ref, pool_ref, oct_ref, sbt_ref, act_ref, *, tq, tq_main):
    nc = kc_ref.shape[1]
    t = pl.program_id(1) * tq + lax.broadcasted_iota(jnp.int32, (1, tq), 1)
    cmp_end = lax.broadcasted_iota(jnp.int32, (nc, 1), 0) * CMP_STRIDE + (CMP_LEN - 1)
    valid = cmp_end <= t
    any_valid = t >= CMP_LEN - 1
    slopes = _alibi_slopes()
    row = lax.broadcasted_iota(jnp.int32, (NSA_D, tq), 0)
    blk = lax.broadcasted_iota(jnp.int32, (LANES, tq), 0)
    blkf = blk.astype(F32)
    cur = lax.shift_right_logical(t, SEL_SHIFT)
    forced = (blk == 0) | (blk == cur) | (blk == cur - 1)
    for g in range(NSA_GROUPS):
        vo = jnp.concatenate([vct_ref[g * NSA_D:(g + 1) * NSA_D], ovt_ref[...]], axis=0)
        imp = jnp.zeros((LANES, tq), F32)
        for hh in range(NSA_HPG):
            h = g * NSA_HPG + hh
            hi_w = slopes[h] * (CMP_STRIDE << END_SHIFT)
            tail = jnp.where(row == 0, hi_w, jnp.where(row == 1, slopes[h] * CMP_STRIDE, 0.0)).astype(BF16)
            q = jnp.concatenate([qt_ref[h * NSA_D:(h + 1) * NSA_D], tail], axis=0)
            s = jnp.where(valid, _dot(kc_ref[g], q), NEG)
            e = jnp.exp(s - jnp.max(s, axis=0, keepdims=True))
            norm = jnp.where(any_valid, 1.0 / jnp.maximum(jnp.sum(e, axis=0, keepdims=True), 1e-30), 0.0)
            r = _dot(vo, e.astype(BF16)) * norm
            oct_ref[h * NSA_D:(h + 1) * NSA_D] = r[:NSA_D]
            imp = imp + r[NSA_D:]
        work = jnp.where(blk <= cur, jnp.where(forced, FORCE_SCORE, imp), NEG)
        chosen = jnp.zeros((LANES, tq), jnp.bool_)
        for _ in range(SEL_TOPK):
            top = jnp.max(work, axis=0, keepdims=True)
            idx = jnp.min(jnp.where(work == top, blkf, float(LANES)), axis=0, keepdims=True)
            hit = blkf == idx
            chosen = chosen | hit
            work = jnp.where(hit, -jnp.inf, work)
        sbt_ref[g] = jnp.where(chosen, 0.0, NEG).astype(BF16)
        used = _dot(pool_ref[...], chosen.astype(BF16))
        for j in range(tq // tq_main):
            any_q = jnp.max(used[:, j * tq_main:(j + 1) * tq_main], axis=1, keepdims=True)
            act_ref[g, j] = jnp.broadcast_to(any_q, act_ref.shape[2:])


def _nsa_cmp(qt, kc, vct, overlap_t, pool, tq, tq_main):
    b, _, s = qt.shape
    nc = kc.shape[2]
    nkt = pool.shape[0]
    return pl.pallas_call(
        functools.partial(_nsa_cmp_kernel, tq=tq, tq_main=tq_main),
        grid=(b, s // tq),
        in_specs=[pl.BlockSpec((None, NSA_HEADS * NSA_D, tq), lambda bi, i: (bi, 0, i)),
                  pl.BlockSpec((None, NSA_GROUPS, nc, LANES), lambda bi, i: (bi, 0, 0, 0)),
                  pl.BlockSpec((None, LANES, nc), lambda bi, i: (bi, 0, 0)),
                  _const_spec(overlap_t.shape), _const_spec(pool.shape)],
        out_specs=[pl.BlockSpec((None, NSA_HEADS * NSA_D, tq), lambda bi, i: (bi, 0, i)),
                   pl.BlockSpec((None, NSA_GROUPS, LANES, tq), lambda bi, i: (bi, 0, 0, i)),
                   pl.BlockSpec((None, NSA_GROUPS, tq // tq_main, nkt, LANES), lambda bi, i: (bi, 0, i, 0, 0))],
        out_shape=[jax.ShapeDtypeStruct((b, NSA_HEADS * NSA_D, s), F32),
                   jax.ShapeDtypeStruct((b, NSA_GROUPS, LANES, s), BF16),
                   jax.ShapeDtypeStruct((b, NSA_GROUPS, s // tq_main, nkt, LANES), F32)],
        compiler_params=_params("parallel", "parallel"),
        name="nsa_cmp",
    )(qt, kc, vct, overlap_t, pool)


def _tile_pool_matrix(seq, tk_sel):
    per_tile = tk_sel // SEL_LEN
    nkt = -(-(seq // tk_sel) // 16) * 16
    pool = np.zeros((nkt, LANES), np.float32)
    for c in range(seq // SEL_LEN):
        pool[c // per_tile, c] = 1.0
    return jnp.asarray(pool, BF16)


def _overlap_matrix_t(seq):
    n_c = (seq - CMP_LEN) // CMP_STRIDE + 1
    n_sel = seq // SEL_LEN
    c0 = np.arange(n_c) * CMP_STRIDE
    s0 = np.arange(n_sel) * SEL_LEN
    ov = np.clip(np.minimum(c0[:, None] + CMP_LEN, s0[None, :] + SEL_LEN)
                 - np.maximum(c0[:, None], s0[None, :]), 0, None) / CMP_LEN
    full = np.zeros((LANES, seq // CMP_STRIDE), np.float32)
    full[:n_sel, :n_c] = ov.T
    return jnp.asarray(full, BF16)


def _flash_pipelined(n_tiles, first_scores, scores, values, dv, s_buf, p_buf, exp=jnp.exp, n_static=0):
    chains = range(len(first_scores))
    nq = first_scores[0].shape[1]
    state = tuple((jnp.full((1, nq), NEG, F32), jnp.zeros((dv + V_EXTRA, nq), F32)) for _ in chains)
    for c in chains:
        s_buf[0, c] = first_scores[c]
        p_buf[1, c] = jnp.zeros(p_buf.shape[2:], BF16)
    colmax = tuple(jnp.max(first_scores[c], axis=0, keepdims=True) for c in chains)
    ones = tuple(jnp.ones((1, nq), F32) for _ in chains)

    static = isinstance(n_tiles, int)

    def step(i, carry, cur):
        st, alphas, cmax = carry
        nxt = 1 - cur
        vts = values(max(i - 1, 0) if isinstance(i, int) else jnp.maximum(i - 1, 0))
        s_next = None if static and i + 1 == n_tiles else scores(i + 1)
        new_st, new_alpha, new_cmax = [], [], []
        for c in chains:
            m, acc = st[c]
            acc = alphas[c] * acc + _dot(vts[c], p_buf[nxt, c])
            m_new = jnp.maximum(m, cmax[c])
            p_buf[cur, c] = exp(s_buf[cur, c] - m_new).astype(BF16)
            new_st.append((m_new, acc))
            new_alpha.append(exp(m - m_new))
            if s_next is not None:
                s_buf[nxt, c] = s_next[c]
                new_cmax.append(jnp.max(s_next[c], axis=0, keepdims=True))
        return tuple(new_st), tuple(new_alpha), tuple(new_cmax)

    def steps(first, count, carry):
        for u in range(count):
            carry = step(first + u, carry, u % 2)
        return carry

    carry = (state, ones, colmax)
    if static:
        st, alphas, _ = steps(0, n_tiles, carry)
        vts = values(n_tiles - 1)
    else:
        carry = steps(0, n_static, carry)
        n_pairs = n_tiles // 2
        carry = lax.fori_loop(n_static // 2, n_pairs, lambda j, c: steps(2 * j, 2, c), carry)
        st, alphas, _ = lax.fori_loop(2 * n_pairs, n_tiles, lambda i, c: steps(i, 1, c), carry)
        vts = values(jnp.maximum(n_tiles - 1, 0))
    last = (n_tiles + 1) & 1
    outs = []
    for c, (_, acc) in enumerate(st):
        acc = alphas[c] * acc + _dot(vts[c], p_buf[last, c])
        outs.append(acc[:dv] * (1.0 / acc[dv:dv + 1]))
    return outs


def _nsa_main_kernel(act_ref, qt_ref, sbt_ref, oct_ref, gt_ref, ksa_ref, kwa_ref, vst_ref, vwt_ref, y_ref,
                     s_buf, p_buf, tiles_ref, *, tq, tk_sel, act_tiles):
    t0 = pl.program_id(1) * tq
    nq = NSA_HPG * tq
    slopes = _alibi_slopes()
    t_lane = jnp.concatenate([t0 + lax.broadcasted_iota(jnp.int32, (1, tq), 1)] * NSA_HPG, axis=1)
    row = lax.broadcasted_iota(jnp.int32, (NSA_D, nq), 0)
    blk_rel = (lax.broadcasted_iota(jnp.int32, (LANES, tq), 0)
               - lax.shift_right_logical(t0, SEL_SHIFT)).astype(F32)

    q_sel, q_win = [], []
    for g in range(NSA_GROUPS):
        heads = range(g * NSA_HPG, (g + 1) * NSA_HPG)
        sl = jnp.concatenate([jnp.full((1, tq), slopes[h], F32) for h in heads], axis=1)
        qg = jnp.concatenate(
            [qt_ref[h * NSA_D:(h + 1) * NSA_D] for h in heads], axis=1)
        tail = jnp.where(row == 0, sl, 0.0).astype(BF16)
        tail_win = jnp.where(row == 0, sl, jnp.where(row == 1, sl * LANES, 0.0)).astype(BF16)
        sb = sbt_ref[g].astype(F32)
        bias = jnp.concatenate([sb + (slopes[h] * SEL_LEN) * blk_rel for h in heads], axis=1)
        q_sel.append(jnp.concatenate([bias.astype(BF16), qg, tail], axis=0))
        q_win.append(jnp.concatenate([qg, tail_win], axis=0))

    groups = range(NSA_GROUPS)

    def sel_scores(kt, g):
        k0 = pl.multiple_of(kt * tk_sel, tk_sel)
        return _dot(ksa_ref[g, pl.ds(k0, tk_sel), :], q_sel[g])

    def sel_values(kt, g):
        k0 = pl.multiple_of(kt * tk_sel, tk_sel)
        return vst_ref[g, :, pl.ds(k0, tk_sel)]

    n_full = (t0 + 1) // tk_sel
    n_kt = tiles_ref.shape[1]
    counts, spare = [], []
    for g in groups:
        base = ((pl.program_id(0) * NSA_GROUPS + g) * pl.num_programs(1) + pl.program_id(1)) * act_tiles
        for k in range(n_kt):
            tiles_ref[g, k] = 0

        def scan(kt, carry, g=g, base=base):
            cnt, unused = carry
            used = act_ref[base + kt]
            tiles_ref[g, cnt] = kt
            return cnt + used, jnp.where(used == 0, kt, unused)

        cnt, unused = lax.fori_loop(0, n_full, scan, (jnp.int32(0), jnp.int32(0)))
        counts.append(cnt)
        spare.append(unused)

    def tile_of(k, g):
        return jnp.where(k < counts[g], tiles_ref[g, jnp.minimum(k, n_kt - 1)], spare[g])

    n_diag = tq // tk_sel
    first = t0 // tk_sel

    def at(k, g):
        if isinstance(k, int) and k < n_diag:
            return first + k
        return jnp.where(k < n_diag, first + k, tile_of(jnp.maximum(k - n_diag, 0), g))

    def sel_at(k):
        if isinstance(k, int) and k < n_diag:
            causal = (first + k) * tk_sel + lax.broadcasted_iota(jnp.int32, (tk_sel, 1), 0) <= t_lane
            return [jnp.where(causal, sel_scores(first + k, g), NEG) for g in groups]
        return [sel_scores(tile_of(k - n_diag, g), g) for g in groups]

    sel = _flash_pipelined(jnp.maximum(counts[0], counts[1]) + n_diag, sel_at(0), sel_at,
                           lambda k: [sel_values(at(k, g), g) for g in groups], NSA_D, s_buf, p_buf,
                           n_static=n_diag - n_diag % 2)

    assert WINDOW % tk_sel == 0 and tq % tk_sel == 0
    start = pl.multiple_of(jnp.maximum(t0 - WINDOW, 0), tk_sel)
    n_wt = (WINDOW + tq) // tk_sel

    def win_scores(k, g):
        w = n_wt - 1 - k
        k0 = pl.multiple_of(start + w * tk_sel, tk_sel)
        key = k0 + lax.broadcasted_iota(jnp.int32, (tk_sel, 1), 0)
        ok = key <= t_lane
        if w < n_diag:
            ok = ok & (key > t_lane - WINDOW)
        return jnp.where(ok, _dot(kwa_ref[g, pl.ds(k0, tk_sel), :], q_win[g]), NEG)

    def win_values(k, g):
        k0 = pl.multiple_of(start + (n_wt - 1 - k) * tk_sel, tk_sel)
        return vwt_ref[g, :, pl.ds(k0, tk_sel)]

    win = _flash_pipelined(n_wt, [win_scores(0, g) for g in groups],
                           lambda k: [win_scores(k, g) for g in groups],
                           lambda k: [win_values(k, g) for g in groups], NSA_D, s_buf, p_buf)

    gates = gt_ref[...]
    outs = []
    for h in range(NSA_HEADS):
        g, hh = divmod(h, NSA_HPG)
        cols = slice(hh * tq, (hh + 1) * tq)
        outs.append(gates[3 * h:3 * h + 1] * oct_ref[h * NSA_D:(h + 1) * NSA_D]
                    + gates[3 * h + 1:3 * h + 2] * sel[g][:, cols] + gates[3 * h + 2:3 * h + 3] * win[g][:, cols])
    y_ref[...] = jnp.concatenate(outs, axis=0).T.astype(BF16)


def _nsa_main(act, qt, sbt, oct, gt, ksa, kwa, vst, vwt, tq, tk_sel):
    b, _, s = qt.shape
    act_tiles = act.shape[0] // (b * NSA_GROUPS * (s // tq))
    feat = lambda r: pl.BlockSpec((None, r, tq), lambda bi, i, _: (bi, 0, i))
    whole = lambda *dims: pl.BlockSpec((None, NSA_GROUPS) + dims, lambda bi, i, _: (bi, 0, 0, 0))
    grid_spec = pltpu.PrefetchScalarGridSpec(
        num_scalar_prefetch=1,
        grid=(b, s // tq),
        in_specs=[feat(NSA_HEADS * NSA_D),
                  pl.BlockSpec((None, NSA_GROUPS, LANES, tq), lambda bi, i, _: (bi, 0, 0, i)),
                  feat(NSA_HEADS * NSA_D), feat(3 * NSA_HEADS),
                  whole(s, 2 * LANES), whole(s, LANES),
                  whole(NSA_D + V_EXTRA, s), whole(NSA_D + V_EXTRA, s)],
        out_specs=pl.BlockSpec((None, tq, NSA_HEADS * NSA_D), lambda bi, i, _: (bi, i, 0)),
        scratch_shapes=[pltpu.VMEM((2, NSA_GROUPS, tk_sel, NSA_HPG * tq), F32),
                        pltpu.VMEM((2, NSA_GROUPS, tk_sel, NSA_HPG * tq), BF16),
                        pltpu.SMEM((NSA_GROUPS, act_tiles), jnp.int32)])
    return pl.pallas_call(
        functools.partial(_nsa_main_kernel, tq=tq, tk_sel=tk_sel, act_tiles=act_tiles),
        grid_spec=grid_spec,
        out_shape=jax.ShapeDtypeStruct((b, s, NSA_HEADS * NSA_D), BF16),
        compiler_params=_params("parallel", "arbitrary"),
        name="nsa_main",
    )(act, qt, sbt, oct, gt, ksa, kwa, vst, vwt)


def _mla_kernel(qt_ref, k_ref, vt_ref, y_ref, s_buf, p_buf, *, tq, tk):
    t0 = pl.program_id(2) * tq
    t_lane = t0 + lax.broadcasted_iota(jnp.int32, (1, tq), 1)
    heads = range(2)
    qts = [qt_ref[hh * LANES:(hh + 1) * LANES] for hh in heads]

    def scores(kt):
        k0 = pl.multiple_of(kt * tk, tk)
        return [_dot(k_ref[hh, pl.ds(k0, tk), :], qts[hh]) for hh in heads]

    def values(kt):
        k0 = pl.multiple_of(kt * tk, tk)
        return [vt_ref[hh, :, pl.ds(k0, tk)] for hh in heads]

    assert tq == 2 * tk
    first = 2 * pl.program_id(2)

    def at(k):
        return first + k if isinstance(k, int) and k < 2 else jnp.where(k < 2, first + k, k - 2)

    def masked_scores(k):
        if isinstance(k, int) and k < 2:
            causal = (first + k) * tk + lax.broadcasted_iota(jnp.int32, (tk, 1), 0) <= t_lane
            return [jnp.where(causal, s, NEG) for s in scores(first + k)]
        return scores(k - 2)

    st = _flash_pipelined(first + 2, masked_scores(0), masked_scores, lambda k: values(at(k)),
                          MLA_V, s_buf, p_buf, exp=jnp.exp2, n_static=2)
    o = jnp.concatenate(st, axis=0)
    y_ref[...] = o.T.astype(BF16)


def _mla(qmt, km, vmt, tq, tk):
    b, h, s, _ = km.shape
    k5 = km.reshape(b, h // 2, 2, s, LANES)
    vmt = vmt.reshape(b, h // 2, 2, MLA_V + V_EXTRA, s)
    return pl.pallas_call(
        functools.partial(_mla_kernel, tq=tq, tk=tk),
        grid=(b, h // 2, s // tq),
        in_specs=[pl.BlockSpec((None, 2 * LANES, tq), lambda bi, hp, i: (bi, hp, i)),
                  pl.BlockSpec((None, None, 2, s, LANES), lambda bi, hp, i: (bi, hp, 0, 0, 0)),
                  pl.BlockSpec((None, None, 2, MLA_V + V_EXTRA, s), lambda bi, hp, i: (bi, hp, 0, 0, 0))],
        out_specs=pl.BlockSpec((None, tq, 2 * MLA_V), lambda bi, hp, i: (bi, i, hp)),
        out_shape=jax.ShapeDtypeStruct((b, s, h * MLA_V), BF16),
        scratch_shapes=[pltpu.VMEM((2, 2, tk, tq), F32), pltpu.VMEM((2, 2, tk, tq), BF16)],
        compiler_params=_params("parallel", "parallel", "arbitrary"),
        name="mla",
    )(qmt, k5, vmt)


def _merge_kernel(x_ref, yn_ref, ym_ref, pre_ref, post_ref, wgm_ref, wpn_ref, wpm_ref, wo_ref, o_ref):
    x = x_ref[...]
    d = x.shape[1]
    h = _rms(x, pre_ref[...]).astype(BF16)
    gm = jax.nn.sigmoid(_dot(h, wgm_ref[...]))
    merged = gm[:, :d] * _dot(yn_ref[...], wpn_ref[...]) + gm[:, d:] * _dot(ym_ref[...], wpm_ref[...])
    y = _dot(merged.astype(BF16), wo_ref[...])
    o_ref[...] = x + _rms(y, post_ref[...])


def _merge(x1, y_nsa, y_mla, pre_g, post_g, w_gm, w_pn, w_pm, w_out, tm):
    n, d = x1.shape
    tok = lambda w: pl.BlockSpec((tm, w), lambda i: (i, 0))
    return pl.pallas_call(
        _merge_kernel,
        grid=(n // tm,),
        in_specs=[tok(d), tok(y_nsa.shape[1]), tok(y_mla.shape[1]), _const_spec((1, d)), _const_spec((1, d)),
                  _const_spec(w_gm.shape), _const_spec(w_pn.shape), _const_spec(w_pm.shape),
                  _const_spec(w_out.shape)],
        out_specs=tok(d),
        out_shape=jax.ShapeDtypeStruct((n, d), F32),
        compiler_params=_params("parallel"),
        name="merge",
    )(x1, y_nsa, y_mla, pre_g.reshape(1, d), post_g.reshape(1, d),
      w_gm.astype(BF16), w_pn.astype(BF16), w_pm.astype(BF16), w_out.astype(BF16))


def kernel(x, ff1_pre_g, ff1_post_g, ff1_w_gate, ff1_w_up, ff1_w_down, mix_pre_g, mix_post_g, w_in, cmp_pos_k, cmp_w1_k, cmp_w2_k, cmp_pos_v, cmp_w1_v, cmp_w2_v, mla_q_norm_g, mla_w_uq, mla_kv_norm_g, mla_w_ukv, w_proj_nsa, w_proj_mla, w_out, ff2_pre_g, ff2_post_g, ff2_w_gate, ff2_w_up, ff2_w_down):
    b, s, d = x.shape
    assert s % (SEL_LEN * SEL_TOPK) == 0 and s // SEL_LEN <= LANES
    tl = _tiles(s)
    n = b * s

    x1 = _ffn(x.reshape(n, d), ff1_pre_g, ff1_post_g, ff1_w_gate, ff1_w_up, ff1_w_down, tl["tm"])

    (qt, ksa, kwa, vst, vwt, cmp_in, gt, qmt, km, vmt) = _inproj(
        x1.reshape(b, s, d), mix_pre_g, _inproj_weights(w_in, mla_w_uq, mla_w_ukv),
        mla_q_norm_g, mla_kv_norm_g, _rope_tables(s), tl["tm"])

    kc, vct = _compress(cmp_in, cmp_pos_k, cmp_w1_k, cmp_w2_k, cmp_pos_v, cmp_w1_v, cmp_w2_v)
    oct, sbt, act = _nsa_cmp(qt, kc, vct, _overlap_matrix_t(s), _tile_pool_matrix(s, tl["tk_sel"]),
                             tl["tq_cmp"], tl["tq_nsa"])
    act = (act[..., 0] > 0).astype(jnp.int32).reshape(-1)
    y_nsa = _nsa_main(act, qt, sbt, oct, gt, ksa, kwa, vst, vwt, tl["tq_nsa"], tl["tk_sel"])
    y_mla = _mla(qmt, km, vmt, tl["tq_mla"], tl["tq_mla"] // 2)

    o_gm = sum((NSA_HEADS * NSA_D, 6 * NSA_GROUPS * NSA_D, 3 * NSA_HEADS, MLA_Q_RANK, MLA_KV_RANK, MLA_ROPE))
    x2 = _merge(x1, y_nsa.reshape(n, -1), y_mla.reshape(n, -1), mix_pre_g, mix_post_g,
                w_in[:, o_gm:], w_proj_nsa, w_proj_mla, w_out, tl["tm"])
    x3 = _ffn(x2, ff2_pre_g, ff2_post_g, ff2_w_gate, ff2_w_up, ff2_w_down, tl["tm"])
    return x3.reshape(b, s, d)
```

```python
import functools

import numpy as np
import jax
import jax.numpy as jnp
from jax import lax
from jax.experimental import pallas as pl
from jax.experimental.pallas import tpu as pltpu

F32 = jnp.float32
BF16 = jnp.bfloat16

EPS = 1e-6
NEG = -1e30
FORCE_SCORE = 1e4
NSA_HEADS = 8
NSA_GROUPS = 2
NSA_HPG = NSA_HEADS // NSA_GROUPS
NSA_D = 64
CMP_LEN = 32
CMP_STRIDE = 16
CMP_HID = 256
END_SHIFT = 4
SEL_LEN = 64
SEL_SHIFT = 6
SEL_TOPK = 16
WINDOW = 512
MLA_HEADS = 8
MLA_NOPE = 64
MLA_ROPE = 32
MLA_V = 64
MLA_Q_RANK = 256
MLA_KV_RANK = 128
ROPE_THETA = 10000.0
LANES = 128
POS_SHIFT = 7
LOG2E = 1.4426950408889634
VMEM_LIMIT = 56 * 1024 * 1024
FFN_CHUNK = 256
MLA_HEADS_PER_STEP = 4
V_EXTRA = 16


def _tiles(seq):
    return dict(
        tm=min(1024, seq),
        tq_cmp=min(1024, seq),
        tq_nsa=min(256, seq),
        tk_sel=min(256, seq),
        tq_mla=min(1024, seq),
    )


def _params(*sem):
    return pltpu.CompilerParams(dimension_semantics=sem, vmem_limit_bytes=VMEM_LIMIT)


def _const_spec(shape):
    nd = len(shape)
    return pl.BlockSpec(shape, lambda *_: (0,) * nd, pipeline_mode=pl.Buffered(1))


def _rms(x, g):
    return x * lax.rsqrt(jnp.mean(x * x, axis=-1, keepdims=True) + EPS) * g


def _dot(a, b):
    return jnp.dot(a, b, preferred_element_type=F32)


def _dot_nt(a, b):
    return lax.dot_general(a, b, (((1,), (1,)), ((), ())), preferred_element_type=F32)


def _alibi_slopes():
    return [float(2.0 ** (-8.0 * (i + 1) / NSA_HEADS)) for i in range(NSA_HEADS)]


def _ffn_kernel(x_ref, pre_ref, post_ref, wg_ref, wu_ref, wd_ref, o_ref, *, fc):
    x = x_ref[...]
    h = _rms(x, pre_ref[...]).astype(BF16)
    acc = jnp.zeros(x.shape, F32)
    for c in range(wg_ref.shape[1] // fc):
        gate = _dot(h, wg_ref[:, c * fc:(c + 1) * fc])
        up = _dot(h, wu_ref[:, c * fc:(c + 1) * fc])
        act = (gate * jax.nn.sigmoid(gate) * up).astype(BF16)
        acc = acc + _dot(act, wd_ref[c * fc:(c + 1) * fc, :])
    o_ref[...] = x + 0.5 * _rms(acc, post_ref[...])


def _ffn(x2d, pre_g, post_g, w_gate, w_up, w_down, tm):
    n, d = x2d.shape
    f = w_gate.shape[1]
    fc = FFN_CHUNK if f % FFN_CHUNK == 0 else f
    return pl.pallas_call(
        functools.partial(_ffn_kernel, fc=fc),
        grid=(n // tm,),
        in_specs=[pl.BlockSpec((tm, d), lambda i: (i, 0)),
                  _const_spec((1, d)), _const_spec((1, d)),
                  _const_spec((d, f)), _const_spec((d, f)), _const_spec((f, d))],
        out_specs=pl.BlockSpec((tm, d), lambda i: (i, 0)),
        out_shape=jax.ShapeDtypeStruct((n, d), F32),
        compiler_params=_params("parallel"),
        name="ffn",
    )(x2d, pre_g.reshape(1, d), post_g.reshape(1, d),
      w_gate.astype(BF16), w_up.astype(BF16), w_down.astype(BF16))


_T_KSEL = 0
_T_KWIN = _T_KSEL + 2 * LANES
_T_CMP = _T_KWIN + 2 * LANES
_T_CQ = _T_CMP + 2 * LANES
_T_CKV = _T_CQ + MLA_Q_RANK
_T_KPE = _T_CKV + MLA_KV_RANK
_T_END = _T_KPE + 2 * LANES
_F_QN = 0
_F_VSEL = _F_QN + NSA_HEADS * NSA_D
_F_VWIN = _F_VSEL + LANES
_F_GATE = _F_VWIN + LANES
_F_END = _F_GATE + 32


def _inproj_weights(w_in, w_uq, w_ukv):
    d = w_in.shape[0]
    o_q = 0
    o_kv = o_q + NSA_HEADS * NSA_D
    o_g = o_kv + 6 * NSA_GROUPS * NSA_D
    o_cq = o_g + 3 * NSA_HEADS
    o_ckv = o_cq + MLA_Q_RANK
    o_kpe = o_ckv + MLA_KV_RANK
    zeros = lambda n: jnp.zeros((d, n), w_in.dtype)
    kv = lambda j, g: w_in[:, o_kv + (j * NSA_GROUPS + g) * NSA_D:o_kv + (j * NSA_GROUPS + g + 1) * NSA_D]
    half = MLA_ROPE // 2
    kp1 = w_in[:, o_kpe:o_kpe + half]
    kp2 = w_in[:, o_kpe + half:o_kpe + MLA_ROPE]
    tail = LANES - MLA_NOPE - MLA_ROPE
    cols = [kv(2, 0), zeros(NSA_D), kv(2, 1), zeros(NSA_D),
            kv(4, 0), zeros(NSA_D), kv(4, 1), zeros(NSA_D),
            kv(0, 0), kv(0, 1), kv(1, 0), kv(1, 1),
            w_in[:, o_cq:o_cq + MLA_Q_RANK], w_in[:, o_ckv:o_ckv + MLA_KV_RANK],
            zeros(MLA_NOPE), kp1, kp2, zeros(tail), zeros(MLA_NOPE), kp2, kp1, zeros(tail)]
    w_tok = jnp.concatenate(cols, axis=1).astype(BF16)
    assert w_tok.shape[1] == _T_END
    rows = [w_in[:, o_q:o_q + NSA_HEADS * NSA_D], kv(3, 0), kv(3, 1), kv(5, 0), kv(5, 1)]
    rows += [w_in[:, o_g:o_g + 3 * NSA_HEADS], zeros(_F_END - _F_GATE - 3 * NSA_HEADS)]
    w_feat = jnp.concatenate(rows, axis=1).T.astype(BF16)
    assert w_feat.shape[0] == _F_END

    dq = MLA_NOPE + MLA_ROPE
    zq = lambda n: jnp.zeros((w_uq.shape[0], n), w_uq.dtype)
    qa, qs = [], []
    for h in range(MLA_HEADS):
        nope = w_uq[:, h * dq:h * dq + MLA_NOPE]
        r1 = w_uq[:, h * dq + MLA_NOPE:h * dq + MLA_NOPE + half]
        r2 = w_uq[:, h * dq + MLA_NOPE + half:(h + 1) * dq]
        qa += [nope, r1, r2, zq(tail)]
        qs += [zq(MLA_NOPE), r2, r1, zq(tail)]
    w_q2t = jnp.concatenate(qa + qs, axis=1).T.astype(BF16)

    dkv = MLA_NOPE + MLA_V
    zk = lambda n: jnp.zeros((w_ukv.shape[0], n), w_ukv.dtype)
    ka, va = [], []
    for h in range(MLA_HEADS):
        ka += [w_ukv[:, h * dkv:h * dkv + MLA_NOPE], zk(LANES - MLA_NOPE)]
        va += [w_ukv[:, h * dkv + MLA_NOPE:(h + 1) * dkv]]
    w_k2 = jnp.concatenate(ka, axis=1).astype(BF16)
    w_v2t = jnp.concatenate(va, axis=1).T.astype(BF16)
    return w_tok, w_feat, w_q2t, w_k2, w_v2t


def _rope_tables(seq):
    half = MLA_ROPE // 2
    freqs = jnp.asarray(ROPE_THETA ** (-np.arange(half, dtype=np.float32) / half), F32)
    ang = jnp.arange(seq).astype(F32)[:, None] * freqs[None, :]
    cos, sin = jnp.cos(ang), jnp.sin(ang)
    pad = jnp.zeros((seq, LANES - MLA_NOPE - MLA_ROPE), F32)
    cq = jnp.concatenate([jnp.ones((seq, MLA_NOPE), F32), cos, cos, pad], axis=1)
    ck = jnp.concatenate([jnp.zeros((seq, MLA_NOPE), F32), cos, cos, pad], axis=1)
    sn = jnp.concatenate([jnp.zeros((seq, MLA_NOPE), F32), -sin, sin, pad], axis=1)
    return ck, sn, cq.T, sn.T


def _inproj_kernel(x_ref, g_ref, wt_ref, wf_ref, qg_ref, wq2_ref, kvg_ref, wk2_ref, wv2_ref,
                   ck_ref, sn_ref, cqt_ref, snt_ref,
                   qt_ref, ksa_ref, kwa_ref, vst_ref, vwt_ref, cmp_ref, gt_ref, qmt_ref, km_ref, vmt_ref,
                   *, tm):
    h = _rms(x_ref[...], g_ref[...]).astype(BF16)
    z = _dot(h, wt_ref[...])
    zt = _dot_nt(wf_ref[...], h)
    qt_ref[...] = (zt[_F_QN:_F_VSEL] * NSA_D ** -0.5).astype(BF16)
    ones_rows = (lax.broadcasted_iota(jnp.int32, (V_EXTRA, tm), 0) == 0).astype(BF16)
    for g in range(NSA_GROUPS):
        for ref, base in ((vst_ref, _F_VSEL), (vwt_ref, _F_VWIN)):
            ref[g, :NSA_D] = zt[base + g * NSA_D:base + (g + 1) * NSA_D].astype(BF16)
            ref[g, NSA_D:] = ones_rows
    gt_ref[...] = jax.nn.sigmoid(zt[_F_GATE:_F_GATE + 3 * NSA_HEADS])
    pos = pl.program_id(1) * tm + lax.broadcasted_iota(jnp.int32, (tm, LANES), 0)
    lane = lax.broadcasted_iota(jnp.int32, (tm, LANES), 1)
    onehot = (lane == lax.shift_right_logical(pos, SEL_SHIFT)).astype(BF16)
    in_blk = (pos & (SEL_LEN - 1)).astype(F32)
    pos_lo_hi = jnp.where(lane == NSA_D, pos & (LANES - 1), lax.shift_right_logical(pos, POS_SHIFT)).astype(F32)
    for g in range(NSA_GROUPS):
        ksa_ref[g, :, :LANES] = onehot
        ksa_ref[g, :, LANES:] = jnp.where(
            lane == NSA_D, in_blk, z[:, _T_KSEL + g * LANES:_T_KSEL + (g + 1) * LANES]).astype(BF16)
        kwa_ref[g] = jnp.where(
            (lane == NSA_D) | (lane == NSA_D + 1), pos_lo_hi,
            z[:, _T_KWIN + g * LANES:_T_KWIN + (g + 1) * LANES]).astype(BF16)
    for j in range(4):
        cmp_ref[j] = z[:, _T_CMP + j * NSA_D:_T_CMP + (j + 1) * NSA_D]
    cqn = _rms(z[:, _T_CQ:_T_CKV], qg_ref[...]).astype(BF16)
    q2t = _dot_nt(wq2_ref[...], cqn)
    cqt, snt = cqt_ref[...], snt_ref[...]
    nq = MLA_HEADS * LANES
    scale = (MLA_NOPE + MLA_ROPE) ** -0.5 * LOG2E
    for hh in range(MLA_HEADS):
        qa = q2t[hh * LANES:(hh + 1) * LANES]
        qs = q2t[nq + hh * LANES:nq + (hh + 1) * LANES]
        qmt_ref[hh * LANES:(hh + 1) * LANES] = ((qa * cqt + qs * snt) * scale).astype(BF16)
    ckvn = _rms(z[:, _T_CKV:_T_KPE], kvg_ref[...]).astype(BF16)
    k2 = _dot(ckvn, wk2_ref[...])
    krot = z[:, _T_KPE:_T_KPE + LANES] * ck_ref[...] + z[:, _T_KPE + LANES:_T_END] * sn_ref[...]
    for hh in range(MLA_HEADS):
        km_ref[hh] = (k2[:, hh * LANES:(hh + 1) * LANES] + krot).astype(BF16)
    vt = _dot_nt(wv2_ref[...], ckvn)
    for hh in range(MLA_HEADS):
        vmt_ref[hh, :MLA_V] = vt[hh * MLA_V:(hh + 1) * MLA_V].astype(BF16)
        vmt_ref[hh, MLA_V:] = ones_rows


def _inproj(x1, mix_pre_g, weights, q_norm_g, kv_norm_g, tables, tm):
    b, s, d = x1.shape
    w_tok, w_feat, w_q2t, w_k2, w_v2t = weights
    ck, sn, cqt, snt = tables
    tok = lambda w: pl.BlockSpec((None, tm, w), lambda bi, i: (bi, i, 0))
    feat = lambda r: pl.BlockSpec((None, r, tm), lambda bi, i: (bi, 0, i))
    heads = lambda n, w: pl.BlockSpec((None, n, tm, w), lambda bi, i: (bi, 0, i, 0))
    slabs = lambda n, r: pl.BlockSpec((None, n, r, tm), lambda bi, i: (bi, 0, 0, i))
    tab = pl.BlockSpec((tm, LANES), lambda bi, i: (i, 0))
    tabt = pl.BlockSpec((LANES, tm), lambda bi, i: (0, i))
    sds = jax.ShapeDtypeStruct
    return pl.pallas_call(
        functools.partial(_inproj_kernel, tm=tm),
        grid=(b, s // tm),
        in_specs=[tok(d), _const_spec((1, d)), _const_spec(w_tok.shape), _const_spec(w_feat.shape),
                  _const_spec((1, MLA_Q_RANK)), _const_spec(w_q2t.shape),
                  _const_spec((1, MLA_KV_RANK)), _const_spec(w_k2.shape), _const_spec(w_v2t.shape),
                  tab, tab, tabt, tabt],
        out_specs=[feat(NSA_HEADS * NSA_D), heads(NSA_GROUPS, 2 * LANES), heads(NSA_GROUPS, LANES),
                   slabs(NSA_GROUPS, NSA_D + V_EXTRA), slabs(NSA_GROUPS, NSA_D + V_EXTRA),
                   heads(4, NSA_D), feat(3 * NSA_HEADS),
                   feat(MLA_HEADS * LANES), heads(MLA_HEADS, LANES), slabs(MLA_HEADS, MLA_V + V_EXTRA)],
        out_shape=[sds((b, NSA_HEADS * NSA_D, s), BF16), sds((b, NSA_GROUPS, s, 2 * LANES), BF16),
                   sds((b, NSA_GROUPS, s, LANES), BF16),
                   sds((b, NSA_GROUPS, NSA_D + V_EXTRA, s), BF16), sds((b, NSA_GROUPS, NSA_D + V_EXTRA, s), BF16),
                   sds((b, 4, s, NSA_D), F32), sds((b, 3 * NSA_HEADS, s), F32),
                   sds((b, MLA_HEADS * LANES, s), BF16), sds((b, MLA_HEADS, s, LANES), BF16),
                   sds((b, MLA_HEADS, MLA_V + V_EXTRA, s), BF16)],
        compiler_params=_params("parallel", "parallel"),
        name="inproj",
    )(x1, mix_pre_g.reshape(1, d), w_tok, w_feat, q_norm_g.reshape(1, -1), w_q2t,
      kv_norm_g.reshape(1, -1), w_k2, w_v2t, ck, sn, cqt, snt)


def _compress_kernel(c_ref, pos_ref, w1_ref, w2k_ref, w2vt_ref, kc_ref, vct_ref):
    nc = c_ref.shape[1]
    lane = lax.broadcasted_iota(jnp.int32, (nc, NSA_D), 1)
    blk = lax.broadcasted_iota(jnp.int32, (nc, NSA_D), 0)
    end_cols = jnp.where(lane == 0, lax.shift_right_logical(blk, END_SHIFT),
                         jnp.where(lane == 1, blk & ((1 << END_SHIFT) - 1), 0)).astype(F32)
    for j in range(4):
        kv, g = divmod(j, NSA_GROUPS)
        c = c_ref[j]
        a0 = _dot((c + pos_ref[kv, 0]).astype(BF16), w1_ref[kv, 0])
        a1 = _dot((c + pos_ref[kv, 1]).astype(BF16), w1_ref[kv, 1])
        pre = a0 + pltpu.roll(a1, nc - 1, 0)
        hid = jax.nn.gelu(pre).astype(BF16)
        if kv == 0:
            kc_ref[g] = jnp.concatenate([_dot(hid, w2k_ref[...]), end_cols], axis=1).astype(BF16)
        else:
            vct_ref[g * NSA_D:(g + 1) * NSA_D] = _dot_nt(w2vt_ref[...], hid).astype(BF16)


def _compress(cmp_in, pos_k, w1_k, w2_k, pos_v, w1_v, w2_v):
    b, _, s, dk = cmp_in.shape
    nc = s // CMP_STRIDE
    kdim = CMP_STRIDE * dk
    c4 = cmp_in.reshape(b, 4, nc, kdim)
    pos = jnp.stack([pos_k.reshape(2, 1, kdim), pos_v.reshape(2, 1, kdim)])
    w1 = jnp.stack([w1_k.reshape(2, kdim, CMP_HID), w1_v.reshape(2, kdim, CMP_HID)]).astype(BF16)
    return pl.pallas_call(
        _compress_kernel,
        grid=(b,),
        in_specs=[pl.BlockSpec((None, 4, nc, kdim), lambda bi: (bi, 0, 0, 0)),
                  _const_spec(pos.shape), _const_spec(w1.shape),
                  _const_spec((CMP_HID, dk)), _const_spec((dk, CMP_HID))],
        out_specs=[pl.BlockSpec((None, NSA_GROUPS, nc, LANES), lambda bi: (bi, 0, 0, 0)),
                   pl.BlockSpec((None, LANES, nc), lambda bi: (bi, 0, 0))],
        out_shape=[jax.ShapeDtypeStruct((b, NSA_GROUPS, nc, LANES), BF16),
                   jax.ShapeDtypeStruct((b, LANES, nc), BF16)],
        compiler_params=_params("parallel"),
        name="compress",
    )(c4, pos, w1, w2_k.astype(BF16), w2_v.T.astype(BF16))


def _nsa_cmp_kernel(qt_ref, kc_ref, vct_ref, ovt_ref, pool_ref, oct_ref, sbt_ref, act_ref, *, tq, tq_main):
    nc = kc_ref.shape[1]
    t = pl.program_id(1) * tq + lax.broadcasted_iota(jnp.int32, (1, tq), 1)
    cmp_end = lax.broadcasted_iota(jnp.int32, (nc, 1), 0) * CMP_STRIDE + (CMP_LEN - 1)
    valid = cmp_end <= t
    any_valid = t >= CMP_LEN - 1
    slopes = _alibi_slopes()
    row = lax.broadcasted_iota(jnp.int32, (NSA_D, tq), 0)
    blk = lax.broadcasted_iota(jnp.int32, (LANES, tq), 0)
    blkf = blk.astype(F32)
    cur = lax.shift_right_logical(t, SEL_SHIFT)
    forced = (blk == 0) | (blk == cur) | (blk == cur - 1)
    for g in range(NSA_GROUPS):
        vo = jnp.concatenate([vct_ref[g * NSA_D:(g + 1) * NSA_D], ovt_ref[...]], axis=0)
        imp = jnp.zeros((LANES, tq), F32)
        for hh in range(NSA_HPG):
            h = g * NSA_HPG + hh
            hi_w = slopes[h] * (CMP_STRIDE << END_SHIFT)
            tail = jnp.where(row == 0, hi_w, jnp.where(row == 1, slopes[h] * CMP_STRIDE, 0.0)).astype(BF16)
            q = jnp.concatenate([qt_ref[h * NSA_D:(h + 1) * NSA_D], tail], axis=0)
            s = jnp.where(valid, _dot(kc_ref[g], q), NEG)
            e = jnp.exp(s - jnp.max(s, axis=0, keepdims=True))
            norm = jnp.where(any_valid, 1.0 / jnp.maximum(jnp.sum(e, axis=0, keepdims=True), 1e-30), 0.0)
            r = _dot(vo, e.astype(BF16)) * norm
            oct_ref[h * NSA_D:(h + 1) * NSA_D] = r[:NSA_D]
            imp = imp + r[NSA_D:]
        work = jnp.where(blk <= cur, jnp.where(forced, FORCE_SCORE, imp), NEG)
        chosen = jnp.zeros((LANES, tq), jnp.bool_)
        for _ in range(SEL_TOPK):
            top = jnp.max(work, axis=0, keepdims=True)
            idx = jnp.min(jnp.where(work == top, blkf, float(LANES)), axis=0, keepdims=True)
            hit = blkf == idx
            chosen = chosen | hit
            work = jnp.where(hit, -jnp.inf, work)
        sbt_ref[g] = jnp.where(chosen, 0.0, NEG).astype(BF16)
        used = _dot(pool_ref[...], chosen.astype(BF16))
        for j in range(tq // tq_main):
            any_q = jnp.max(used[:, j * tq_main:(j + 1) * tq_main], axis=1, keepdims=True)
            act_ref[g, j] = jnp.broadcast_to(any_q, act_ref.shape[2:])


def _nsa_cmp(qt, kc, vct, overlap_t, pool, tq, tq_main):
    b, _, s = qt.shape
    nc = kc.shape[2]
    nkt = pool.shape[0]
    return pl.pallas_call(
        functools.partial(_nsa_cmp_kernel, tq=tq, tq_main=tq_main),
        grid=(b, s // tq),
        in_specs=[pl.BlockSpec((None, NSA_HEADS * NSA_D, tq), lambda bi, i: (bi, 0, i)),
                  pl.BlockSpec((None, NSA_GROUPS, nc, LANES), lambda bi, i: (bi, 0, 0, 0)),
                  pl.BlockSpec((None, LANES, nc), lambda bi, i: (bi, 0, 0)),
                  _const_spec(overlap_t.shape), _const_spec(pool.shape)],
        out_specs=[pl.BlockSpec((None, NSA_HEADS * NSA_D, tq), lambda bi, i: (bi, 0, i)),
                   pl.BlockSpec((None, NSA_GROUPS, LANES, tq), lambda bi, i: (bi, 0, 0, i)),
                   pl.BlockSpec((None, NSA_GROUPS, tq // tq_main, nkt, LANES), lambda bi, i: (bi, 0, i, 0, 0))],
        out_shape=[jax.ShapeDtypeStruct((b, NSA_HEADS * NSA_D, s), F32),
                   jax.ShapeDtypeStruct((b, NSA_GROUPS, LANES, s), BF16),
                   jax.ShapeDtypeStruct((b, NSA_GROUPS, s // tq_main, nkt, LANES), F32)],
        compiler_params=_params("parallel", "parallel"),
        name="nsa_cmp",
    )(qt, kc, vct, overlap_t, pool)


def _tile_pool_matrix(seq, tk_sel):
    per_tile = tk_sel // SEL_LEN
    nkt = -(-(seq // tk_sel) // 16) * 16
    pool = np.zeros((nkt, LANES), np.float32)
    for c in range(seq // SEL_LEN):
        pool[c // per_tile, c] = 1.0
    return jnp.asarray(pool, BF16)


def _overlap_matrix_t(seq):
    n_c = (seq - CMP_LEN) // CMP_STRIDE + 1
    n_sel = seq // SEL_LEN
    c0 = np.arange(n_c) * CMP_STRIDE
    s0 = np.arange(n_sel) * SEL_LEN
    ov = np.clip(np.minimum(c0[:, None] + CMP_LEN, s0[None, :] + SEL_LEN)
                 - np.maximum(c0[:, None], s0[None, :]), 0, None) / CMP_LEN
    full = np.zeros((LANES, seq // CMP_STRIDE), np.float32)
    full[:n_sel, :n_c] = ov.T
    return jnp.asarray(full, BF16)


def _flash_pipelined(n_tiles, first_scores, scores, values, dv, s_buf, p_buf, exp=jnp.exp, n_static=0):
    chains = range(len(first_scores))
    nq = first_scores[0].shape[1]
    state = tuple((jnp.full((1, nq), NEG, F32), jnp.zeros((dv + V_EXTRA, nq), F32)) for _ in chains)
    for c in chains:
        s_buf[0, c] = first_scores[c]
        p_buf[1, c] = jnp.zeros(p_buf.shape[2:], BF16)
    colmax = tuple(jnp.max(first_scores[c], axis=0, keepdims=True) for c in chains)
    ones = tuple(jnp.ones((1, nq), F32) for _ in chains)

    static = isinstance(n_tiles, int)

    def step(i, carry, cur):
        st, alphas, cmax = carry
        nxt = 1 - cur
        vts = values(max(i - 1, 0) if isinstance(i, int) else jnp.maximum(i - 1, 0))
        s_next = None if static and i + 1 == n_tiles else scores(i + 1)
        new_st, new_alpha, new_cmax = [], [], []
        for c in chains:
            m, acc = st[c]
            acc = alphas[c] * acc + _dot(vts[c], p_buf[nxt, c])
            m_new = jnp.maximum(m, cmax[c])
            p_buf[cur, c] = exp(s_buf[cur, c] - m_new).astype(BF16)
            new_st.append((m_new, acc))
            new_alpha.append(exp(m - m_new))
            if s_next is not None:
                s_buf[nxt, c] = s_next[c]
                new_cmax.append(jnp.max(s_next[c], axis=0, keepdims=True))
        return tuple(new_st), tuple(new_alpha), tuple(new_cmax)

    def steps(first, count, carry):
        for u in range(count):
            carry = step(first + u, carry, u % 2)
        return carry

    carry = (state, ones, colmax)
    if static:
        st, alphas, _ = steps(0, n_tiles, carry)
        vts = values(n_tiles - 1)
    else:
        carry = steps(0, n_static, carry)
        n_pairs = n_tiles // 2
        carry = lax.fori_loop(n_static // 2, n_pairs, lambda j, c: steps(2 * j, 2, c), carry)
        st, alphas, _ = lax.fori_loop(2 * n_pairs, n_tiles, lambda i, c: steps(i, 1, c), carry)
        vts = values(jnp.maximum(n_tiles - 1, 0))
    last = (n_tiles + 1) & 1
    outs = []
    for c, (_, acc) in enumerate(st):
        acc = alphas[c] * acc + _dot(vts[c], p_buf[last, c])
        outs.append(acc[:dv] * (1.0 / acc[dv:dv + 1]))
    return outs


def _nsa_main_kernel(act_ref, qt_ref, sbt_ref, oct_ref, gt_ref, ksa_ref, kwa_ref, vst_ref, vwt_ref, y_ref,
                     s_buf, p_buf, tiles_ref, *, tq, tk_sel, act_tiles):
    t0 = pl.program_id(1) * tq
    nq = NSA_HPG * tq
    slopes = _alibi_slopes()
    t_lane = jnp.concatenate([t0 + lax.broadcasted_iota(jnp.int32, (1, tq), 1)] * NSA_HPG, axis=1)
    row = lax.broadcasted_iota(jnp.int32, (NSA_D, nq), 0)
    blk_rel = (lax.broadcasted_iota(jnp.int32, (LANES, tq), 0)
               - lax.shift_right_logical(t0, SEL_SHIFT)).astype(F32)

    q_sel, q_win = [], []
    for g in range(NSA_GROUPS):
        heads = range(g * NSA_HPG, (g + 1) * NSA_HPG)
        sl = jnp.concatenate([jnp.full((1, tq), slopes[h], F32) for h in heads], axis=1)
        qg = jnp.concatenate(
            [qt_ref[h * NSA_D:(h + 1) * NSA_D] for h in heads], axis=1)
        tail = jnp.where(row == 0, sl, 0.0).astype(BF16)
        tail_win = jnp.where(row == 0, sl, jnp.where(row == 1, sl * LANES, 0.0)).astype(BF16)
        sb = sbt_ref[g].astype(F32)
        bias = jnp.concatenate([sb + (slopes[h] * SEL_LEN) * blk_rel for h in heads], axis=1)
        q_sel.append(jnp.concatenate([bias.astype(BF16), qg, tail], axis=0))
        q_win.append(jnp.concatenate([qg, tail_win], axis=0))

    groups = range(NSA_GROUPS)

    def sel_scores(kt, g):
        k0 = pl.multiple_of(kt * tk_sel, tk_sel)
        return _dot(ksa_ref[g, pl.ds(k0, tk_sel), :], q_sel[g])

    def sel_values(kt, g):
        k0 = pl.multiple_of(kt * tk_sel, tk_sel)
        return vst_ref[g, :, pl.ds(k0, tk_sel)]

    n_full = (t0 + 1) // tk_sel
    n_kt = tiles_ref.shape[1]
    counts, spare = [], []
    for g in groups:
        base = ((pl.program_id(0) * NSA_GROUPS + g) * pl.num_programs(1) + pl.program_id(1)) * act_tiles
        for k in range(n_kt):
            tiles_ref[g, k] = 0

        def scan(kt, carry, g=g, base=base):
            cnt, unused = carry
            used = act_ref[base + kt]
            tiles_ref[g, cnt] = kt
            return cnt + used, jnp.where(used == 0, kt, unused)

        cnt, unused = lax.fori_loop(0, n_full, scan, (jnp.int32(0), jnp.int32(0)))
        counts.append(cnt)
        spare.append(unused)

    def tile_of(k, g):
        return jnp.where(k < counts[g], tiles_ref[g, jnp.minimum(k, n_kt - 1)], spare[g])

    n_diag = tq // tk_sel
    first = t0 // tk_sel

    def at(k, g):
        if isinstance(k, int) and k < n_diag:
            return first + k
        return jnp.where(k < n_diag, first + k, tile_of(jnp.maximum(k - n_diag, 0), g))

    def sel_at(k):
        if isinstance(k, int) and k < n_diag:
            causal = (first + k) * tk_sel + lax.broadcasted_iota(jnp.int32, (tk_sel, 1), 0) <= t_lane
            return [jnp.where(causal, sel_scores(first + k, g), NEG) for g in groups]
        return [sel_scores(tile_of(k - n_diag, g), g) for g in groups]

    sel = _flash_pipelined(jnp.maximum(counts[0], counts[1]) + n_diag, sel_at(0), sel_at,
                           lambda k: [sel_values(at(k, g), g) for g in groups], NSA_D, s_buf, p_buf,
                           n_static=n_diag - n_diag % 2)

    assert WINDOW % tk_sel == 0 and tq % tk_sel == 0
    start = pl.multiple_of(jnp.maximum(t0 - WINDOW, 0), tk_sel)
    n_wt = (WINDOW + tq) // tk_sel

    def win_scores(k, g):
        w = n_wt - 1 - k
        k0 = pl.multiple_of(start + w * tk_sel, tk_sel)
        key = k0 + lax.broadcasted_iota(jnp.int32, (tk_sel, 1), 0)
        ok = key <= t_lane
        if w < n_diag:
            ok = ok & (key > t_lane - WINDOW)
        return jnp.where(ok, _dot(kwa_ref[g, pl.ds(k0, tk_sel), :], q_win[g]), NEG)

    def win_values(k, g):
        k0 = pl.multiple_of(start + (n_wt - 1 - k) * tk_sel, tk_sel)
        return vwt_ref[g, :, pl.ds(k0, tk_sel)]

    win = _flash_pipelined(n_wt, [win_scores(0, g) for g in groups],
                           lambda k: [win_scores(k, g) for g in groups],
                           lambda k: [win_values(k, g) for g in groups], NSA_D, s_buf, p_buf)

    gates = gt_ref[...]
    outs = []
    for h in range(NSA_HEADS):
        g, hh = divmod(h, NSA_HPG)
        cols = slice(hh * tq, (hh + 1) * tq)
        outs.append(gates[3 * h:3 * h + 1] * oct_ref[h * NSA_D:(h + 1) * NSA_D]
                    + gates[3 * h + 1:3 * h + 2] * sel[g][:, cols] + gates[3 * h + 2:3 * h + 3] * win[g][:, cols])
    y_ref[...] = jnp.concatenate(outs, axis=0).T.astype(BF16)


def _nsa_main(act, qt, sbt, oct, gt, ksa, kwa, vst, vwt, tq, tk_sel):
    b, _, s = qt.shape
    act_tiles = act.shape[0] // (b * NSA_GROUPS * (s // tq))
    feat = lambda r: pl.BlockSpec((None, r, tq), lambda bi, i, _: (bi, 0, i))
    whole = lambda *dims: pl.BlockSpec((None, NSA_GROUPS) + dims, lambda bi, i, _: (bi, 0, 0, 0))
    grid_spec = pltpu.PrefetchScalarGridSpec(
        num_scalar_prefetch=1,
        grid=(b, s // tq),
        in_specs=[feat(NSA_HEADS * NSA_D),
                  pl.BlockSpec((None, NSA_GROUPS, LANES, tq), lambda bi, i, _: (bi, 0, 0, i)),
                  feat(NSA_HEADS * NSA_D), feat(3 * NSA_HEADS),
                  whole(s, 2 * LANES), whole(s, LANES),
                  whole(NSA_D + V_EXTRA, s), whole(NSA_D + V_EXTRA, s)],
        out_specs=pl.BlockSpec((None, tq, NSA_HEADS * NSA_D), lambda bi, i, _: (bi, i, 0)),
        scratch_shapes=[pltpu.VMEM((2, NSA_GROUPS, tk_sel, NSA_HPG * tq), F32),
                        pltpu.VMEM((2, NSA_GROUPS, tk_sel, NSA_HPG * tq), BF16),
                        pltpu.SMEM((NSA_GROUPS, act_tiles), jnp.int32)])
    return pl.pallas_call(
        functools.partial(_nsa_main_kernel, tq=tq, tk_sel=tk_sel, act_tiles=act_tiles),
        grid_spec=grid_spec,
        out_shape=jax.ShapeDtypeStruct((b, s, NSA_HEADS * NSA_D), BF16),
        compiler_params=_params("parallel", "arbitrary"),
        name="nsa_main",
    )(act, qt, sbt, oct, gt, ksa, kwa, vst, vwt)


def _mla_kernel(qt_ref, k_ref, vt_ref, y_ref, s_buf, p_buf, *, tq, tk):
    t0 = pl.program_id(2) * tq
    t_lane = t0 + lax.broadcasted_iota(jnp.int32, (1, tq), 1)
    assert tq == 2 * tk
    first = 2 * pl.program_id(2)

    def at(k):
        return first + k if isinstance(k, int) and k < 2 else jnp.where(k < 2, first + k, k - 2)

    outs = []
    for base in range(0, k_ref.shape[0], 2):
        heads = range(base, base + 2)
        qts = {hh: qt_ref[hh * LANES:(hh + 1) * LANES] for hh in heads}

        def scores(kt, heads=heads, qts=qts):
            k0 = pl.multiple_of(kt * tk, tk)
            return [_dot(k_ref[hh, pl.ds(k0, tk), :], qts[hh]) for hh in heads]

        def values(kt, heads=heads):
            k0 = pl.multiple_of(kt * tk, tk)
            return [vt_ref[hh, :, pl.ds(k0, tk)] for hh in heads]

        def masked_scores(k, scores=scores):
            if isinstance(k, int) and k < 2:
                causal = (first + k) * tk + lax.broadcasted_iota(jnp.int32, (tk, 1), 0) <= t_lane
                return [jnp.where(causal, s, NEG) for s in scores(first + k)]
            return scores(k - 2)

        outs += _flash_pipelined(first + 2, masked_scores(0), masked_scores,
                                 lambda k, values=values: values(at(k)),
                                 MLA_V, s_buf, p_buf, exp=jnp.exp2, n_static=2)
    y_ref[...] = jnp.concatenate(outs, axis=0).T.astype(BF16)


def _mla(qmt, km, vmt, tq, tk):
    b, h, s, _ = km.shape
    hs = MLA_HEADS_PER_STEP
    k5 = km.reshape(b, h // hs, hs, s, LANES)
    vmt = vmt.reshape(b, h // hs, hs, MLA_V + V_EXTRA, s)
    return pl.pallas_call(
        functools.partial(_mla_kernel, tq=tq, tk=tk),
        grid=(b, h // hs, s // tq),
        in_specs=[pl.BlockSpec((None, hs * LANES, tq), lambda bi, hp, i: (bi, hp, i)),
                  pl.BlockSpec((None, None, hs, s, LANES), lambda bi, hp, i: (bi, hp, 0, 0, 0)),
                  pl.BlockSpec((None, None, hs, MLA_V + V_EXTRA, s), lambda bi, hp, i: (bi, hp, 0, 0, 0))],
        out_specs=pl.BlockSpec((None, tq, hs * MLA_V), lambda bi, hp, i: (bi, i, hp)),
        out_shape=jax.ShapeDtypeStruct((b, s, h * MLA_V), BF16),
        scratch_shapes=[pltpu.VMEM((2, 2, tk, tq), F32), pltpu.VMEM((2, 2, tk, tq), BF16)],
        compiler_params=_params("parallel", "parallel", "arbitrary"),
        name="mla",
    )(qmt, k5, vmt)


def _merge_kernel(x_ref, yn_ref, ym_ref, pre_ref, post_ref, wgm_ref, wpn_ref, wpm_ref, wo_ref, o_ref):
    x = x_ref[...]
    d = x.shape[1]
    h = _rms(x, pre_ref[...]).astype(BF16)
    gm = jax.nn.sigmoid(_dot(h, wgm_ref[...]))
    merged = gm[:, :d] * _dot(yn_ref[...], wpn_ref[...]) + gm[:, d:] * _dot(ym_ref[...], wpm_ref[...])
    y = _dot(merged.astype(BF16), wo_ref[...])
    o_ref[...] = x + _rms(y, post_ref[...])


def _merge(x1, y_nsa, y_mla, pre_g, post_g, w_gm, w_pn, w_pm, w_out, tm):
    n, d = x1.shape
    tok = lambda w: pl.BlockSpec((tm, w), lambda i: (i, 0))
    return pl.pallas_call(
        _merge_kernel,
        grid=(n // tm,),
        in_specs=[tok(d), tok(y_nsa.shape[1]), tok(y_mla.shape[1]), _const_spec((1, d)), _const_spec((1, d)),
                  _const_spec(w_gm.shape), _const_spec(w_pn.shape), _const_spec(w_pm.shape),
                  _const_spec(w_out.shape)],
        out_specs=tok(d),
        out_shape=jax.ShapeDtypeStruct((n, d), F32),
        compiler_params=_params("parallel"),
        name="merge",
    )(x1, y_nsa, y_mla, pre_g.reshape(1, d), post_g.reshape(1, d),
      w_gm.astype(BF16), w_pn.astype(BF16), w_pm.astype(BF16), w_out.astype(BF16))


def kernel(x, ff1_pre_g, ff1_post_g, ff1_w_gate, ff1_w_up, ff1_w_down, mix_pre_g, mix_post_g, w_in, cmp_pos_k, cmp_w1_k, cmp_w2_k, cmp_pos_v, cmp_w1_v, cmp_w2_v, mla_q_norm_g, mla_w_uq, mla_kv_norm_g, mla_w_ukv, w_proj_nsa, w_proj_mla, w_out, ff2_pre_g, ff2_post_g, ff2_w_gate, ff2_w_up, ff2_w_down):
    b, s, d = x.shape
    assert s % (SEL_LEN * SEL_TOPK) == 0 and s // SEL_LEN <= LANES
    tl = _tiles(s)
    n = b * s

    x1 = _ffn(x.reshape(n, d), ff1_pre_g, ff1_post_g, ff1_w_gate, ff1_w_up, ff1_w_down, tl["tm"])

    (qt, ksa, kwa, vst, vwt, cmp_in, gt, qmt, km, vmt) = _inproj(
        x1.reshape(b, s, d), mix_pre_g, _inproj_weights(w_in, mla_w_uq, mla_w_ukv),
        mla_q_norm_g, mla_kv_norm_g, _rope_tables(s), tl["tm"])

    kc, vct = _compress(cmp_in, cmp_pos_k, cmp_w1_k, cmp_w2_k, cmp_pos_v, cmp_w1_v, cmp_w2_v)
    oct, sbt, act = _nsa_cmp(qt, kc, vct, _overlap_matrix_t(s), _tile_pool_matrix(s, tl["tk_sel"]),
                             tl["tq_cmp"], tl["tq_nsa"])
    act = (act[..., 0] > 0).astype(jnp.int32).reshape(-1)
    y_nsa = _nsa_main(act, qt, sbt, oct, gt, ksa, kwa, vst, vwt, tl["tq_nsa"], tl["tk_sel"])
    y_mla = _mla(qmt, km, vmt, tl["tq_mla"], tl["tq_mla"] // 2)

    o_gm = sum((NSA_HEADS * NSA_D, 6 * NSA_GROUPS * NSA_D, 3 * NSA_HEADS, MLA_Q_RANK, MLA_KV_RANK, MLA_ROPE))
    x2 = _merge(x1, y_nsa.reshape(n, -1), y_mla.reshape(n, -1), mix_pre_g, mix_post_g,
                w_in[:, o_gm:], w_proj_nsa, w_proj_mla, w_out, tl["tm"])
    x3 = _ffn(x2, ff2_pre_g, ff2_post_g, ff2_w_gate, ff2_w_up, ff2_w_down, tl["tm"])
    return x3.reshape(b, s, d)
```
